```python
import math
import jax, jax.numpy as jnp
from jax import lax
import numpy as np

D_MODEL = 2048
BATCH = 1
SEQ = 8192
DEPTH = 2

HYENA_WIDTH = D_MODEL // 2
HYENA_ORDER = 2
SHORT_CONV = 3
POS_EMB_DIM = 33
POS_BANDS = (POS_EMB_DIM - 1) // 2
FILTER_HIDDEN = 64
FAST_DECAY_PCT = 0.3
SLOW_DECAY_PCT = 1.5
DECAY_TARGET = 1e-2
MIN_DECAY = math.log(DECAY_TARGET) / SLOW_DECAY_PCT
MAX_DECAY = math.log(DECAY_TARGET) / FAST_DECAY_PCT

GLA_WIDTH = D_MODEL // 2
GLA_HEADS = 4
GLA_DK_TOTAL = GLA_WIDTH // 2
GLA_DV_TOTAL = GLA_WIDTH
GLA_DK = GLA_DK_TOTAL // GLA_HEADS
GLA_DV = GLA_DV_TOTAL // GLA_HEADS
GLA_LOWRANK = 16
GLA_GATE_TAU = 16.0
GLA_CHUNK = 64

FFN_DIM = 5632
N_EXPERTS = 8
TOP_K = 2
EXPERT_DIM = 5632
N_DENSE = (DEPTH + 1) // 2
N_MOE = DEPTH // 2

RMS_EPS = 1e-6

IN_SPLITS = ((HYENA_ORDER + 1) * HYENA_WIDTH, GLA_DK_TOTAL, GLA_DK_TOTAL, GLA_DV_TOTAL,
             GLA_DV_TOTAL, 2 * GLA_LOWRANK, D_MODEL, D_MODEL)
P_IN = sum(IN_SPLITS)

kernel_name = "hyena_gla_gated_parallel_moe_encoder"


def rmsnorm(x, g):
    xf = x.astype(jnp.float32)
    y = xf * lax.rsqrt(jnp.mean(xf * xf, axis=-1, keepdims=True) + RMS_EPS)
    return (y * g.astype(jnp.float32)).astype(x.dtype)


def split_cols(p, sizes):
    offs = np.cumsum(np.array(sizes))[:-1].tolist()
    return jnp.split(p, offs, axis=-1)


def short_conv(u, w, b):
    L = u.shape[1]
    pad = SHORT_CONV // 2
    up = jnp.pad(u, ((0, 0), (pad, SHORT_CONV - 1 - pad), (0, 0)))
    return sum(up[:, i:i + L] * w[i] for i in range(SHORT_CONV)) + b


def hyena_kernel_spectra(L, w1, b1, f1, w2, b2, f2, w3, b3, f3, w4):
    t = jnp.linspace(0.0, 1.0, L, dtype=jnp.float32)[:, None]
    omega = 2.0 * math.pi * jnp.arange(L, dtype=jnp.float32)[:, None] / L
    bands = jnp.linspace(1e-4, POS_BANDS - 1, POS_BANDS, dtype=jnp.float32)[None, :]
    z = jnp.concatenate([t, jnp.cos(bands * omega), -jnp.sin(bands * omega)], axis=-1)
    h = jnp.sin(f1 * (z @ w1 + b1))
    h = jnp.sin(f2 * (h @ w2 + b2))
    h = jnp.sin(f3 * (h @ w3 + b3))
    h = (h @ w4).astype(jnp.float32).reshape(L, HYENA_ORDER, 2, HYENA_WIDTH)
    deltas = jnp.abs(jnp.linspace(MIN_DECAY, MAX_DECAY, HYENA_WIDTH, dtype=jnp.float32))
    h = h * jnp.exp(-t * deltas)[:, None, None, :]
    fwd, bwd = h[:, :, 0], h[:, :, 1]
    k = jnp.concatenate([fwd, jnp.zeros((1, HYENA_ORDER, HYENA_WIDTH), jnp.float32), bwd[:0:-1]], axis=0)
    k = k / jnp.sum(jnp.abs(k), axis=0, keepdims=True)
    return jnp.fft.rfft(k, axis=0)


def long_conv(u, k_f, skip):
    L = u.shape[1]
    y = jnp.fft.irfft(jnp.fft.rfft(u, n=2 * L, axis=1) * k_f[None], n=2 * L, axis=1)[:, :L]
    return y + u * skip


def hyena_mixer(u, conv_w, conv_b, filt, skip):
    L = u.shape[1]
    u = short_conv(u, conv_w, conv_b)
    parts = jnp.split(u, HYENA_ORDER + 1, axis=-1)
    z = parts[0].astype(jnp.float32)
    k_f = hyena_kernel_spectra(L, *filt)
    for n in range(HYENA_ORDER):
        z = parts[n + 1].astype(jnp.float32) * long_conv(z, k_f[:, n], skip[n].astype(jnp.float32))
    return z.astype(u.dtype)


def gla_chunked(q, k, v, log_a, strict):
    B, L, H, DK = q.shape
    DV = v.shape[-1]
    N = L // GLA_CHUNK

    def to_chunks(t):
        return t.reshape(B, N, GLA_CHUNK, H, t.shape[-1]).transpose(0, 3, 1, 2, 4)

    q, k, v, la = to_chunks(q), to_chunks(k), to_chunks(v), to_chunks(log_a)
    b = jnp.cumsum(la, axis=-2)
    b_last = b[..., -1:, :]
    q_in = q * jnp.exp(b)
    k_in = k * jnp.exp(-b)
    k_out = k * jnp.exp(b_last - b)
    mask = jnp.tril(jnp.ones((GLA_CHUNK, GLA_CHUNK), dtype=bool), k=-1 if strict else 0)
    scores = jnp.where(mask, jnp.einsum("bhnik,bhnjk->bhnij", q_in, k_in), 0.0)
    o_intra = jnp.einsum("bhnij,bhnjv->bhniv", scores, v)
    d_state = jnp.einsum("bhnjk,bhnjv->bhnkv", k_out, v)
    decay = jnp.exp(b_last[..., 0, :])

    def step(S, inp):
        dS_n, d_n = inp
        return S * d_n[..., None] + dS_n, S

    S0 = jnp.zeros_like(d_state[:, :, 0])
    _, S_in = lax.scan(step, S0, (jnp.moveaxis(d_state, 2, 0), jnp.moveaxis(decay, 2, 0)))
    S_in = jnp.moveaxis(S_in, 0, 2)
    o = o_intra + jnp.einsum("bhnik,bhnkv->bhniv", q_in, S_in)
    return o.transpose(0, 2, 3, 1, 4).reshape(B, L, H, DV)


def gla_mixer(q, k, v, r, lr, gate_w2, gate_b, norm_g):
    B, L, _ = q.shape
    f32 = jnp.float32
    q = q.astype(f32).reshape(B, L, GLA_HEADS, GLA_DK) * (GLA_DK ** -0.5)
    k = k.astype(f32).reshape(B, L, GLA_HEADS, GLA_DK)
    v = v.astype(f32).reshape(B, L, GLA_HEADS, GLA_DV)
    lr_f, lr_b = jnp.split(lr, 2, axis=-1)

    def log_gate(lr_d, w2, bias):
        g = jax.nn.log_sigmoid((lr_d @ w2 + bias).astype(f32)) / GLA_GATE_TAU
        return g.reshape(B, L, GLA_HEADS, GLA_DK)

    la_f = log_gate(lr_f, gate_w2[0], gate_b[0])
    la_b = log_gate(lr_b, gate_w2[1], gate_b[1])
    flip = lambda t: t[:, ::-1]
    o = gla_chunked(q, k, v, la_f, strict=False) + flip(
        gla_chunked(flip(q), flip(k), flip(v), flip(la_b), strict=True))
    o = rmsnorm(o, norm_g).reshape(B, L, GLA_DV_TOTAL)
    return (o * jax.nn.silu(r.astype(f32))).astype(r.dtype)


def swiglu(h, wg, wu, wd):
    return (jax.nn.silu(h @ wg) * (h @ wu)) @ wd


def moe_ffn(h, router_w, wg, wu, wd):
    logits = (h @ router_w).astype(jnp.float32)
    top_vals, top_idx = lax.top_k(logits, TOP_K)
    gates = jax.nn.softmax(top_vals, axis=-1)
    combine = jnp.einsum("blk,blke->ble", gates, jax.nn.one_hot(top_idx, N_EXPERTS, dtype=jnp.float32))
    out = jnp.zeros_like(h)
    for e in range(N_EXPERTS):
        out = out + combine[..., e:e + 1].astype(h.dtype) * swiglu(h, wg[e], wu[e], wd[e])
    return out


def setup_inputs(seed: int = 0) -> dict:
    key = jax.random.key(seed)
    ks = iter(jax.random.split(key, 40))
    D, NL = D_MODEL, DEPTH
    HC = (HYENA_ORDER + 1) * HYENA_WIDTH

    def nrm(shape, scale):
        return scale * jax.random.normal(next(ks), shape, jnp.float32)

    def gain(shape):
        return 1.0 + 0.02 * jax.random.normal(next(ks), shape, jnp.float32)

    return {
        "x": nrm((BATCH, SEQ, D), 1.0),
        "norm_mix": gain((NL, D)),
        "w_in": nrm((NL, D, P_IN), D ** -0.5),
        "conv_w": nrm((NL, SHORT_CONV, HC), SHORT_CONV ** -0.5),
        "conv_b": nrm((NL, HC), 0.02),
        "filt_w1": nrm((NL, POS_EMB_DIM, FILTER_HIDDEN), POS_EMB_DIM ** -0.5),
        "filt_b1": nrm((NL, FILTER_HIDDEN), 0.1),
        "filt_freq1": 1.0 + nrm((NL, FILTER_HIDDEN), 0.1),
        "filt_w2": nrm((NL, FILTER_HIDDEN, FILTER_HIDDEN), FILTER_HIDDEN ** -0.5),
        "filt_b2": nrm((NL, FILTER_HIDDEN), 0.1),
        "filt_freq2": 1.0 + nrm((NL, FILTER_HIDDEN), 0.1),
        "filt_w3": nrm((NL, FILTER_HIDDEN, FILTER_HIDDEN), FILTER_HIDDEN ** -0.5),
        "filt_b3": nrm((NL, FILTER_HIDDEN), 0.1),
        "filt_freq3": 1.0 + nrm((NL, FILTER_HIDDEN), 0.1),
        "filt_w4": nrm((NL, FILTER_HIDDEN, HYENA_ORDER * 2 * HYENA_WIDTH), FILTER_HIDDEN ** -0.5),
        "hyena_skip": nrm((NL, HYENA_ORDER, HYENA_WIDTH), 1.0),
        "w_branch_a": nrm((NL, HYENA_WIDTH, D), HYENA_WIDTH ** -0.5),
        "gla_gate_w2": nrm((NL, 2, GLA_LOWRANK, GLA_DK_TOTAL), GLA_LOWRANK ** -0.5),
        "gla_gate_b": nrm((NL, 2, GLA_DK_TOTAL), 0.1),
        "gla_norm": gain((NL, GLA_DV)),
        "w_branch_b": nrm((NL, GLA_DV_TOTAL, D), GLA_DV_TOTAL ** -0.5),
        "w_out": nrm((NL, D, D), D ** -0.5),
        "norm_ffn": gain((NL, D)),
        "dense_w_gate": nrm((N_DENSE, D, FFN_DIM), D ** -0.5),
        "dense_w_up": nrm((N_DENSE, D, FFN_DIM), D ** -0.5),
        "dense_w_down": nrm((N_DENSE, FFN_DIM, D), FFN_DIM ** -0.5),
        "router_w": nrm((N_MOE, D, N_EXPERTS), D ** -0.5),
        "moe_w_gate": nrm((N_MOE, N_EXPERTS, D, EXPERT_DIM), D ** -0.5),
        "moe_w_up": nrm((N_MOE, N_EXPERTS, D, EXPERT_DIM), D ** -0.5),
        "moe_w_down": nrm((N_MOE, N_EXPERTS, EXPERT_DIM, D), EXPERT_DIM ** -0.5),
        "norm_final": gain((D,)),
    }


def reference(x, norm_mix, w_in, conv_w, conv_b, filt_w1, filt_b1, filt_freq1, filt_w2, filt_b2,
              filt_freq2, filt_w3, filt_b3, filt_freq3, filt_w4, hyena_skip, w_branch_a,
              gla_gate_w2, gla_gate_b, gla_norm, w_branch_b, w_out, norm_ffn, dense_w_gate,
              dense_w_up, dense_w_down, router_w, moe_w_gate, moe_w_up, moe_w_down, norm_final):
    for l in range(DEPTH):
        h = rmsnorm(x, norm_mix[l])
        p = h @ w_in[l]
        u_hy, q, k, v, r, lr, g_a, g_b = split_cols(p, IN_SPLITS)
        filt = (filt_w1[l], filt_b1[l], filt_freq1[l], filt_w2[l], filt_b2[l], filt_freq2[l],
                filt_w3[l], filt_b3[l], filt_freq3[l], filt_w4[l])
        y_a = hyena_mixer(u_hy, conv_w[l], conv_b[l], filt, hyena_skip[l]) @ w_branch_a[l]
        y_b = gla_mixer(q, k, v, r, lr, gla_gate_w2[l], gla_gate_b[l], gla_norm[l]) @ w_branch_b[l]
        mixed = jax.nn.sigmoid(g_a) * y_a + jax.nn.sigmoid(g_b) * y_b
        x = x + mixed @ w_out[l]
        h = rmsnorm(x, norm_ffn[l])
        if l % 2 == 0:
            i = l // 2
            x = x + swiglu(h, dense_w_gate[i], dense_w_up[i], dense_w_down[i])
        else:
            i = l // 2
            x = x + moe_ffn(h, router_w[i], moe_w_gate[i], moe_w_up[i], moe_w_down[i])
    return rmsnorm(x, norm_final)
```

```python
import functools
import math

import jax
import jax.numpy as jnp
from jax import lax
from jax.experimental import pallas as pl
from jax.experimental.pallas import tpu as pltpu

F32 = jnp.float32
BF16 = jnp.bfloat16

D_MODEL = 2048
DEPTH = 2
HYENA_WIDTH = D_MODEL // 2
HYENA_ORDER = 2
SHORT_CONV = 3
POS_EMB_DIM = 33
POS_BANDS = (POS_EMB_DIM - 1) // 2
FILTER_HIDDEN = 64
MIN_DECAY = math.log(1e-2) / 1.5
MAX_DECAY = math.log(1e-2) / 0.3
GLA_HEADS = 4
GLA_DK = 128
GLA_DV = 256
GLA_DK_TOTAL = GLA_HEADS * GLA_DK
GLA_DV_TOTAL = GLA_HEADS * GLA_DV
GLA_LOWRANK = 16
GLA_GATE_TAU = 16.0
GLA_CHUNK = 64
N_EXPERTS = 8
RMS_EPS = 1e-6

LANES = 128
VMEM_LIMIT_BYTES = 56 * 1024 * 1024


def _params(*sem):
    return pltpu.CompilerParams(dimension_semantics=sem, vmem_limit_bytes=VMEM_LIMIT_BYTES)


def _dot(a, b):
    return jnp.dot(a, b, preferred_element_type=F32)


def _split(a):
    hi = a.astype(BF16)
    lo = (a - hi.astype(F32)).astype(BF16)
    return hi, lo


def _dot3(a, b):
    ah, al = _split(a)
    bh, bl = _split(b)
    return _dot(ah, bh) + (_dot(ah, bl) + _dot(al, bh))


def _sigmoid(x):
    return 1.0 / (1.0 + jnp.exp(-x))


def _silu(x):
    return x * _sigmoid(x)


def _log_sigmoid(x):
    return jnp.minimum(x, 0.0) - jnp.log(1.0 + jnp.exp(-jnp.abs(x)))


def _rmsnorm_kernel(x_ref, g_ref, o_ref):
    x = x_ref[...]
    ms = jnp.mean(x * x, axis=-1, keepdims=True)
    o_ref[...] = (x * lax.rsqrt(ms + RMS_EPS) * g_ref[...]).astype(o_ref.dtype)


def rmsnorm(x, g, out_dtype, tm=256):
    m, d = x.shape
    return pl.pallas_call(
        _rmsnorm_kernel,
        out_shape=jax.ShapeDtypeStruct((m, d), out_dtype),
        grid=(m // tm,),
        in_specs=[pl.BlockSpec((tm, d), lambda i: (i, 0)),
                  pl.BlockSpec((1, d), lambda i: (0, 0))],
        out_specs=pl.BlockSpec((tm, d), lambda i: (i, 0)),
        compiler_params=_params("parallel"),
        name="rmsnorm",
    )(x, g.reshape(1, d))


def _rmsnorm_router_kernel(x_ref, g_ref, rw_ref, o_ref, cw_ref):
    x = x_ref[...]
    ms = jnp.mean(x * x, axis=-1, keepdims=True)
    h = x * lax.rsqrt(ms + RMS_EPS) * g_ref[...]
    o_ref[...] = h.astype(o_ref.dtype)
    logits = _dot3(h, rw_ref[...])
    col = lax.broadcasted_iota(jnp.int32, logits.shape, 1).astype(F32)
    neg = jnp.float32(-jnp.inf)
    logits = jnp.where(col < N_EXPERTS, logits, neg)
    m1 = jnp.max(logits, axis=-1, keepdims=True)
    i1 = jnp.min(jnp.where(logits == m1, col, float(LANES)), axis=-1, keepdims=True)
    rest = jnp.where(col == i1, neg, logits)
    m2 = jnp.max(rest, axis=-1, keepdims=True)
    i2 = jnp.min(jnp.where(rest == m2, col, float(LANES)), axis=-1, keepdims=True)
    e2 = jnp.exp(m2 - m1)
    g1 = 1.0 / (1.0 + e2)
    g2 = e2 / (1.0 + e2)
    cw_ref[...] = jnp.where(col == i1, g1, 0.0) + jnp.where(col == i2, g2, 0.0)


def rmsnorm_router(x, g, router_w, tm=256):
    m, d = x.shape
    rw = jnp.zeros((d, LANES), F32).at[:, :N_EXPERTS].set(router_w)
    return pl.pallas_call(
        _rmsnorm_router_kernel,
        out_shape=(jax.ShapeDtypeStruct((m, d), BF16), jax.ShapeDtypeStruct((m, LANES), F32)),
        grid=(m // tm,),
        in_specs=[pl.BlockSpec((tm, d), lambda i: (i, 0)),
                  pl.BlockSpec((1, d), lambda i: (0, 0)),
                  pl.BlockSpec((d, LANES), lambda i: (0, 0))],
        out_specs=(pl.BlockSpec((tm, d), lambda i: (i, 0)),
                   pl.BlockSpec((tm, LANES), lambda i: (i, 0))),
        compiler_params=_params("parallel"),
        name="rmsnorm_router",
    )(x, g.reshape(1, d), rw)


def _mm_kernel(*refs, n_a, n_w, dots, n_extra, epilogue):
    a_refs = refs[:n_a]
    w_refs = refs[n_a:n_a + n_w]
    e_refs = refs[n_a + n_w:n_a + n_w + n_extra]
    o_ref = refs[n_a + n_w + n_extra]
    a_vals = [a[...].astype(BF16) for a in a_refs]
    accs = [_dot(a_vals[ai], w_refs[wi][...]) for ai, wi in dots]
    o_ref[...] = epilogue(accs, [e[...] for e in e_refs]).astype(o_ref.dtype)


def matmul(a_list, w_list, dots, epilogue, extras, out_dtype, *, tm, tn, name):
    m = a_list[0].shape[0]
    n = w_list[0].shape[1]
    assert m % tm == 0 and n % tn == 0
    in_specs = []
    for a in a_list:
        in_specs.append(pl.BlockSpec((tm, a.shape[1]), lambda i, j: (i, 0)))
    for w in w_list:
        in_specs.append(pl.BlockSpec((w.shape[0], tn), lambda i, j: (0, j)))
    extra_arrays = []
    for arr, kind in extras:
        extra_arrays.append(arr)
        if kind == "mn":
            in_specs.append(pl.BlockSpec((tm, tn), lambda i, j: (i, j)))
        else:
            in_specs.append(pl.BlockSpec((tm, arr.shape[1]), lambda i, j: (i, 0)))
    kern = functools.partial(_mm_kernel, n_a=len(a_list), n_w=len(w_list), dots=tuple(dots),
                             n_extra=len(extras), epilogue=epilogue)
    return pl.pallas_call(
        kern,
        out_shape=jax.ShapeDtypeStruct((m, n), out_dtype),
        grid=(m // tm, n // tn),
        in_specs=in_specs,
        out_specs=pl.BlockSpec((tm, tn), lambda i, j: (i, j)),
        compiler_params=_params("parallel", "parallel"),
        name=name,
    )(*a_list, *w_list, *extra_arrays)


def _ep_plain(accs, ex):
    return accs[0]


def _ep_sigmoid(accs, ex):
    return _sigmoid(accs[0])


def _ep_residual(accs, ex):
    return ex[0] + accs[0]


def _ep_merge(accs, ex):
    return ex[0].astype(F32) * accs[0] + ex[1].astype(F32) * accs[1]


def _ep_swiglu(accs, ex):
    return _silu(accs[0]) * accs[1]


def _ep_swiglu_scaled(accs, ex):
    return _silu(accs[0]) * accs[1] * ex[0][:, 0:1]


def _gate_kernel(h_ref, wlr_ref, w2_ref, b_ref, o_ref):
    lr = _dot(h_ref[...], wlr_ref[...])
    z = _dot3(lr, w2_ref[...]) + b_ref[...]
    o_ref[...] = _log_sigmoid(z) * (1.0 / GLA_GATE_TAU)


def gla_gates(h, w_lr, gate_w2, gate_b, tm=512):
    m, d = h.shape
    r = GLA_LOWRANK
    n = 2 * GLA_DK_TOTAL
    wlr = jnp.zeros((d, LANES), BF16).at[:, :2 * r].set(w_lr)
    w2 = jnp.zeros((LANES, n), F32)
    w2 = w2.at[:r, :GLA_DK_TOTAL].set(gate_w2[0]).at[r:2 * r, GLA_DK_TOTAL:].set(gate_w2[1])
    b = gate_b.reshape(1, n)
    return pl.pallas_call(
        _gate_kernel,
        out_shape=jax.ShapeDtypeStruct((m, n), F32),
        grid=(m // tm,),
        in_specs=[pl.BlockSpec((tm, d), lambda i: (i, 0)),
                  pl.BlockSpec((d, LANES), lambda i: (0, 0)),
                  pl.BlockSpec((LANES, n), lambda i: (0, 0)),
                  pl.BlockSpec((1, n), lambda i: (0, 0))],
        out_specs=pl.BlockSpec((tm, n), lambda i: (i, 0)),
        compiler_params=_params("parallel"),
        name="gla_gates",
    )(h, wlr, w2, b)


GLA_SUB = 4


def _gla_chunk(q, k, v, la, st_ref, h, reverse):
    c = GLA_CHUNK
    row = lax.broadcasted_iota(jnp.int32, (c, c), 0)
    col = lax.broadcasted_iota(jnp.int32, (c, c), 1)
    keep = (col > row) if reverse else (col <= row)
    csum = (col >= row) if reverse else (col <= row)
    tri = jnp.where(csum, 1.0, 0.0).astype(BF16)
    la_hi, la_lo = _split(la)
    b = _dot(tri, la_hi) + _dot(tri, la_lo)
    tot = jnp.sum(la, axis=0, keepdims=True)
    q_in = (q * ((GLA_DK ** -0.5) * jnp.exp(b))).astype(BF16)
    k_in = (k * jnp.exp(-b)).astype(BF16)
    k_out = (k * jnp.exp(tot - b)).astype(BF16)
    vb = v.astype(BF16)
    s = lax.dot_general(q_in, k_in, (((1,), (1,)), ((), ())), preferred_element_type=F32)
    s = jnp.where(keep, s, 0.0).astype(BF16)
    st = st_ref[h]
    o = _dot(s, vb) + lax.dot_general(q_in, st.astype(BF16), (((1,), (1,)), ((), ())),
                                      preferred_element_type=F32)
    upd = lax.dot_general(vb, k_out, (((0,), (0,)), ((), ())), preferred_element_type=F32)
    st_ref[h] = st * jnp.exp(tot) + upd
    return o


def _gla_kernel(qf_ref, kf_ref, vf_ref, lf_ref, qb_ref, kb_ref, vb_ref, lb_ref,
                of_ref, ob_ref, sf_ref, sb_ref):
    @pl.when(pl.program_id(0) == 0)
    def _():
        sf_ref[...] = jnp.zeros_like(sf_ref)
        sb_ref[...] = jnp.zeros_like(sb_ref)

    def body(s, carry):
        rf = pl.multiple_of(s * GLA_CHUNK, GLA_CHUNK)
        rb = pl.multiple_of((GLA_SUB - 1 - s) * GLA_CHUNK, GLA_CHUNK)
        for h in range(GLA_HEADS):
            kk = slice(h * GLA_DK, (h + 1) * GLA_DK)
            vv = slice(h * GLA_DV, (h + 1) * GLA_DV)
            of_ref[pl.ds(rf, GLA_CHUNK), vv] = _gla_chunk(
                qf_ref[pl.ds(rf, GLA_CHUNK), kk], kf_ref[pl.ds(rf, GLA_CHUNK), kk],
                vf_ref[pl.ds(rf, GLA_CHUNK), vv], lf_ref[pl.ds(rf, GLA_CHUNK), kk],
                sf_ref, h, False)
            ob_ref[pl.ds(rb, GLA_CHUNK), vv] = _gla_chunk(
                qb_ref[pl.ds(rb, GLA_CHUNK), kk], kb_ref[pl.ds(rb, GLA_CHUNK), kk],
                vb_ref[pl.ds(rb, GLA_CHUNK), vv], lb_ref[pl.ds(rb, GLA_CHUNK), kk],
                sb_ref, h, True)
        return carry

    lax.fori_loop(0, GLA_SUB, body, 0)


def gla_scan(qkvr, la):
    L = qkvr.shape[0]
    rows = GLA_CHUNK * GLA_SUB
    n = L // rows
    dk, dv = GLA_DK_TOTAL, GLA_DV_TOTAL
    specs = []
    for rev in (False, True):
        blk = (lambda i: n - 1 - i) if rev else (lambda i: i)
        specs += [pl.BlockSpec((rows, dk), lambda i, blk=blk: (blk(i), 0)),
                  pl.BlockSpec((rows, dk), lambda i, blk=blk: (blk(i), 1)),
                  pl.BlockSpec((rows, dv), lambda i, blk=blk: (blk(i), 1)),
                  pl.BlockSpec((rows, dk), lambda i, blk=blk, c=int(rev): (blk(i), c))]
    return pl.pallas_call(
        _gla_kernel,
        out_shape=(jax.ShapeDtypeStruct((L, dv), F32), jax.ShapeDtypeStruct((L, dv), F32)),
        grid=(n,),
        in_specs=specs,
        out_specs=(pl.BlockSpec((rows, dv), lambda i: (i, 0)),
                   pl.BlockSpec((rows, dv), lambda i: (n - 1 - i, 0))),
        scratch_shapes=[pltpu.VMEM((GLA_HEADS, GLA_DV, GLA_DK), F32),
                        pltpu.VMEM((GLA_HEADS, GLA_DV, GLA_DK), F32)],
        compiler_params=_params("arbitrary"),
        name="gla_scan",
    )(qkvr, qkvr, qkvr, la, qkvr, qkvr, qkvr, la)


def _gla_post_kernel(of_ref, ob_ref, r_ref, g_ref, o_ref):
    g = g_ref[...]
    for h in range(GLA_HEADS):
        vv = slice(h * GLA_DV, (h + 1) * GLA_DV)
        o = of_ref[:, vv] + ob_ref[:, vv]
        ms = jnp.mean(o * o, axis=-1, keepdims=True)
        y = o * lax.rsqrt(ms + RMS_EPS) * g
        o_ref[:, vv] = (y * _silu(r_ref[:, vv])).astype(o_ref.dtype)


def gla_post(o_f, o_b, qkvr, norm_g, tm=512):
    L, dv = o_f.shape
    return pl.pallas_call(
        _gla_post_kernel,
        out_shape=jax.ShapeDtypeStruct((L, dv), BF16),
        grid=(L // tm,),
        in_specs=[pl.BlockSpec((tm, dv), lambda i: (i, 0)),
                  pl.BlockSpec((tm, dv), lambda i: (i, 0)),
                  pl.BlockSpec((tm, dv), lambda i: (i, 2)),
                  pl.BlockSpec((1, GLA_DV), lambda i: (0, 0))],
        out_specs=pl.BlockSpec((tm, dv), lambda i: (i, 0)),
        compiler_params=_params("parallel"),
        name="gla_post",
    )(o_f, o_b, qkvr, norm_g.reshape(1, GLA_DV))


CT = 256
CT_COL = 128
MID_SLABS = 8


def _fft_dims(L):
    n = 2 * L
    na = 1 << ((n.bit_length() - 1) // 2)
    nb = n // na
    assert na * nb == n and nb % 8 == 0 and (na // 2) % 8 == 0
    kp = -(-(na // 2 + 1) // 8) * 8
    return n, na, nb, kp


def _cis(num, den):
    ang = (2.0 * math.pi / den) * (num % den).astype(F32)
    return jnp.cos(ang), jnp.sin(ang)


def _fft_tables(L):
    n, na, nb, kp = _fft_dims(L)
    ha = na // 2
    i32 = jnp.int32
    b = jnp.arange(nb, dtype=i32)[:, None, None]
    ka = jnp.arange(kp, dtype=i32)[None, :, None]
    a = jnp.arange(ha, dtype=i32)[None, None, :]
    c, s = _cis(ka * (nb * a + b), n)
    m1 = jnp.concatenate([c, -s], axis=1)
    kb = jnp.arange(nb, dtype=i32)[:, None]
    bb = jnp.arange(nb, dtype=i32)[None, :]
    c, s = _cis(kb * bb, nb)
    f2 = jnp.block([[c, s], [-s, c]])
    k = jnp.arange(kp, dtype=i32)[:, None, None]
    c, s = _cis(bb.T[None] * (k + na * kb.T[None]), n)
    c, s = c / n, s / n
    g = jnp.concatenate([jnp.concatenate([c, -s], axis=2),
                         jnp.concatenate([s, c], axis=2)], axis=1)
    ar = jnp.arange(ha, dtype=i32)[:, None]
    kc = jnp.arange(kp, dtype=i32)[None, :]
    wgt = jnp.where((kc == 0) | (kc == ha), 1.0, jnp.where(kc < ha, 2.0, 0.0))
    c, s = _cis(ar * kc, na)
    p = jnp.concatenate([wgt * c, -wgt * s], axis=1)
    return tuple(t.astype(BF16) for t in (m1, f2, g, p))


def _short_conv_kernel(u_ref, w_ref, b_ref, o_ref):
    u = u_ref[...]
    L = u.shape[0]
    row = lax.broadcasted_iota(jnp.int32, u.shape, 0)
    prev = jnp.where(row == 0, 0.0, pltpu.roll(u, 1, 0))
    nxt = jnp.where(row == L - 1, 0.0, pltpu.roll(u, L - 1, 0))
    w = w_ref[...]
    o_ref[...] = prev * w[0:1] + u * w[1:2] + nxt * w[2:3] + b_ref[...]


def short_conv(u, w, b):
    L, c = u.shape
    w8 = jnp.zeros((8, c), F32).at[:SHORT_CONV].set(w)
    return pl.pallas_call(
        _short_conv_kernel,
        out_shape=jax.ShapeDtypeStruct((L, c), F32),
        grid=(c // CT_COL,),
        in_specs=[pl.BlockSpec((L, CT_COL), lambda j: (0, j)),
                  pl.BlockSpec((8, CT_COL), lambda j: (0, j)),
                  pl.BlockSpec((1, CT_COL), lambda j: (0, j))],
        out_specs=pl.BlockSpec((L, CT_COL), lambda j: (0, j)),
        compiler_params=_params("parallel"),
        name="short_conv",
    )(u, w8, b.reshape(1, c))


def _filter_gen_kernel(z_ref, w1_ref, b1_ref, f1_ref, w2_ref, b2_ref, f2_ref, w3_ref, b3_ref,
                       f3_ref, w4_ref, dl_ref, k_ref, nrm_ref, *, tt, L):
    i = pl.program_id(0)
    h = jnp.sin(f1_ref[...] * (_dot3(z_ref[...], w1_ref[...]) + b1_ref[...]))
    h = jnp.sin(f2_ref[...] * (_dot3(h, w2_ref[...]) + b2_ref[...]))
    h = jnp.sin(f3_ref[...] * (_dot3(h, w3_ref[...]) + b3_ref[...]))
    c = dl_ref.shape[1]
    rows = lax.broadcasted_iota(jnp.int32, (tt, c), 0) + i * tt
    t = rows.astype(F32) * (1.0 / (L - 1))
    decay = jnp.exp(-t * dl_ref[...])
    first = rows == 0

    @pl.when(i == 0)
    def _():
        nrm_ref[...] = jnp.zeros_like(nrm_ref)

    for g in range(2 * HYENA_ORDER):
        kg = _dot3(h, w4_ref[:, g * c:(g + 1) * c]) * decay
        if g % 2 == 1:
            kg = jnp.where(first, 0.0, kg)
        k_ref[:, g * c:(g + 1) * c] = kg
        nrm_ref[:, g * c:(g + 1) * c] += jnp.sum(jnp.abs(kg), axis=0, keepdims=True)


def filter_gen(L, w1, b1, f1, w2, b2, f2, w3, b3, f3, w4, tt=256):
    c = HYENA_WIDTH
    hdim = FILTER_HIDDEN
    tt = min(tt, L)
    t = jnp.linspace(0.0, 1.0, L, dtype=F32)[:, None]
    omega = 2.0 * math.pi * jnp.arange(L, dtype=F32)[:, None] / L
    bands = jnp.linspace(1e-4, POS_BANDS - 1, POS_BANDS, dtype=F32)[None, :]
    z = jnp.concatenate([t, jnp.cos(bands * omega), -jnp.sin(bands * omega)], axis=-1)
    z = jnp.zeros((L, LANES), F32).at[:, :POS_EMB_DIM].set(z)
    w1p = jnp.zeros((LANES, hdim), F32).at[:POS_EMB_DIM].set(w1)
    deltas = jnp.abs(jnp.linspace(MIN_DECAY, MAX_DECAY, c, dtype=F32)).reshape(1, c)
    nc = w4.shape[1]
    row = lambda v: v.reshape(1, hdim)
    full = lambda shape: pl.BlockSpec(shape, lambda i: (0, 0))
    kern = functools.partial(_filter_gen_kernel, tt=tt, L=L)
    return pl.pallas_call(
        kern,
        out_shape=(jax.ShapeDtypeStruct((L, nc), F32), jax.ShapeDtypeStruct((1, nc), F32)),
        grid=(L // tt,),
        in_specs=[pl.BlockSpec((tt, LANES), lambda i: (i, 0)),
                  full((LANES, hdim)), full((1, hdim)), full((1, hdim)),
                  full((hdim, hdim)), full((1, hdim)), full((1, hdim)),
                  full((hdim, hdim)), full((1, hdim)), full((1, hdim)),
                  full((hdim, nc)), full((1, c))],
        out_specs=(pl.BlockSpec((tt, nc), lambda i: (i, 0)), full((1, nc))),
        compiler_params=_params("arbitrary"),
        name="hyena_filter_gen",
    )(z, w1p, row(b1), row(f1), w2, row(b2), row(f2), w3, row(b3), row(f3), w4, deltas)


def _fft_s1_kernel(u_ref, m1_ref, a_ref, *, nb, kp, ha):
    def body(b, carry):
        x = u_ref[pl.ds(b, ha, stride=nb), :].astype(BF16)
        r = _dot(m1_ref[b], x)
        a_ref[0, pl.ds(b, kp, stride=nb), :] = r[:kp]
        a_ref[1, pl.ds(b, kp, stride=nb), :] = r[kp:]
        return carry
    lax.fori_loop(0, nb, body, 0)


def fft_stage1(u, ncols, m1, L):
    n, na, nb, kp = _fft_dims(L)
    ha = na // 2
    kern = functools.partial(_fft_s1_kernel, nb=nb, kp=kp, ha=ha)
    return pl.pallas_call(
        kern,
        out_shape=jax.ShapeDtypeStruct((2, kp * nb, ncols), F32),
        grid=(ncols // CT_COL,),
        in_specs=[pl.BlockSpec((L, CT_COL), lambda j: (0, j)),
                  pl.BlockSpec((nb, 2 * kp, ha), lambda j: (0, 0, 0))],
        out_specs=pl.BlockSpec((2, kp * nb, CT_COL), lambda j: (0, 0, j)),
        compiler_params=_params("parallel"),
        name="fft_stage1",
    )(u, m1)


def _slab(ref, part, s, nb):
    return ref[part, s * nb:(s + 1) * nb, :]


def _filter_spec_kernel(af_ref, ab_ref, f2_ref, nf_ref, nbk_ref, h_ref, *, nb):
    f2 = f2_ref[...]
    inv = 1.0 / (nf_ref[...] + nbk_ref[...])
    for s in range(MID_SLABS):
        xf = _dot(f2, jnp.concatenate([_slab(af_ref, 0, s, nb), _slab(af_ref, 1, s, nb)],
                                      axis=0).astype(BF16))
        xb = _dot(f2, jnp.concatenate([_slab(ab_ref, 0, s, nb), _slab(ab_ref, 1, s, nb)],
                                      axis=0).astype(BF16))
        h_ref[0, s * nb:(s + 1) * nb, :] = (xf[:nb] + xb[:nb]) * inv
        h_ref[1, s * nb:(s + 1) * nb, :] = (xf[nb:] - xb[nb:]) * inv


def filter_spectrum(a, nrm, f2, L):
    n, na, nb, kp = _fft_dims(L)
    c = HYENA_WIDTH
    cb = c // CT
    rows = MID_SLABS * nb
    fcol = lambda j: (j // cb) * 2 * cb + j % cb
    kern = functools.partial(_filter_spec_kernel, nb=nb)
    return pl.pallas_call(
        kern,
        out_shape=jax.ShapeDtypeStruct((2, kp * nb, HYENA_ORDER * c), F32),
        grid=(HYENA_ORDER * cb, kp // MID_SLABS),
        in_specs=[pl.BlockSpec((2, rows, CT), lambda j, k: (0, k, fcol(j))),
                  pl.BlockSpec((2, rows, CT), lambda j, k: (0, k, fcol(j) + cb)),
                  pl.BlockSpec((2 * nb, 2 * nb), lambda j, k: (0, 0)),
                  pl.BlockSpec((1, CT), lambda j, k: (0, fcol(j))),
                  pl.BlockSpec((1, CT), lambda j, k: (0, fcol(j) + cb))],
        out_specs=pl.BlockSpec((2, rows, CT), lambda j, k: (0, k, j)),
        compiler_params=_params("parallel", "parallel"),
        name="hyena_filter_spectrum",
    )(a, a, f2, nrm, nrm)


def _fft_mid_kernel(a_ref, h_ref, f2_ref, g_ref, q_ref, *, nb):
    f2 = f2_ref[...]
    for s in range(MID_SLABS):
        x = _dot(f2, jnp.concatenate([_slab(a_ref, 0, s, nb), _slab(a_ref, 1, s, nb)],
                                     axis=0).astype(BF16))
        xr, xi = x[:nb], x[nb:]
        hr, hi = _slab(h_ref, 0, s, nb), _slab(h_ref, 1, s, nb)
        y = jnp.concatenate([xr * hr - xi * hi, xr * hi + xi * hr], axis=0).astype(BF16)
        q = _dot(g_ref[s], y)
        q_ref[0, s * nb:(s + 1) * nb, :] = q[:nb]
        q_ref[1, s * nb:(s + 1) * nb, :] = q[nb:]


def fft_mid(a, hf, order, f2, g, L):
    n, na, nb, kp = _fft_dims(L)
    c = a.shape[2]
    cb = c // CT
    rows = MID_SLABS * nb
    kern = functools.partial(_fft_mid_kernel, nb=nb)
    return pl.pallas_call(
        kern,
        out_shape=jax.ShapeDtypeStruct((2, kp * nb, c), F32),
        grid=(kp // MID_SLABS, cb),
        in_specs=[pl.BlockSpec((2, rows, CT), lambda k, j: (0, k, j)),
                  pl.BlockSpec((2, rows, CT), lambda k, j: (0, k, order * cb + j)),
                  pl.BlockSpec((2 * nb, 2 * nb), lambda k, j: (0, 0)),
                  pl.BlockSpec((MID_SLABS, 2 * nb, 2 * nb), lambda k, j: (k, 0, 0))],
        out_specs=pl.BlockSpec((2, rows, CT), lambda k, j: (0, k, j)),
        compiler_params=_params("parallel", "parallel"),
        name="fft_mid",
    )(a, hf, f2, g)


def _fft_i2_kernel(q_ref, p_ref, u_ref, x_ref, s_ref, o_ref, *, nb, kp, ha):
    skip = s_ref[...]

    def body(b, carry):
        qb = jnp.concatenate([q_ref[0, pl.ds(b, kp, stride=nb), :],
                              q_ref[1, pl.ds(b, kp, stride=nb), :]], axis=0).astype(BF16)
        y = _dot(p_ref[...], qb)
        u = u_ref[pl.ds(b, ha, stride=nb), :]
        gate = x_ref[pl.ds(b, ha, stride=nb), :]
        o_ref[pl.ds(b, ha, stride=nb), :] = gate * (y + u * skip)
        return carry
    lax.fori_loop(0, nb, body, 0)


def fft_stage_out(q, u, gate_src, gate_blk, skip, p, L):
    n, na, nb, kp = _fft_dims(L)
    ha = na // 2
    c = q.shape[2]
    kern = functools.partial(_fft_i2_kernel, nb=nb, kp=kp, ha=ha)
    return pl.pallas_call(
        kern,
        out_shape=jax.ShapeDtypeStruct((L, c), F32),
        grid=(c // CT_COL,),
        in_specs=[pl.BlockSpec((2, kp * nb, CT_COL), lambda j: (0, 0, j)),
                  pl.BlockSpec((ha, 2 * kp), lambda j: (0, 0)),
                  pl.BlockSpec((L, CT_COL), lambda j: (0, j)),
                  pl.BlockSpec((L, CT_COL), lambda j: (0, gate_blk + j)),
                  pl.BlockSpec((1, CT_COL), lambda j: (0, j))],
        out_specs=pl.BlockSpec((L, CT_COL), lambda j: (0, j)),
        compiler_params=_params("parallel"),
        name="fft_stage_out",
    )(q, p, u, gate_src, skip.reshape(1, c))


def hyena_mixer(u_hy, conv_w, conv_b, filt, skip, tables):
    L = u_hy.shape[0]
    c = HYENA_WIDTH
    m1, f2, g, p = tables
    uc = short_conv(u_hy, conv_w, conv_b)
    k_time, nrm = filter_gen(L, *filt)
    hf = filter_spectrum(fft_stage1(k_time, k_time.shape[1], m1, L), nrm, f2, L)
    z = uc
    for order in range(HYENA_ORDER):
        a = fft_stage1(z, c, m1, L)
        q = fft_mid(a, hf, order, f2, g, L)
        z = fft_stage_out(q, z, uc, (order + 1) * (c // CT_COL), skip[order], p, L)
    return z


def _layer_mixers(x, l, tables, norm_mix, w_in, conv_w, conv_b, filt, hyena_skip, w_branch_a,
                  gla_gate_w2, gla_gate_b, gla_norm, w_branch_b, w_out):
    hc = (HYENA_ORDER + 1) * HYENA_WIDTH
    o_lr = hc + 2 * GLA_DK_TOTAL + 2 * GLA_DV_TOTAL
    o_g = o_lr + 2 * GLA_LOWRANK
    wl = w_in[l].astype(BF16)
    h = rmsnorm(x, norm_mix[l], BF16)
    proj = functools.partial(matmul, [h], dots=[(0, 0)], extras=[], tm=1024, tn=512)
    u_hy = proj([wl[:, :hc]], epilogue=_ep_plain, out_dtype=F32, name="in_proj_hyena")
    qkvr = proj([wl[:, hc:o_lr]], epilogue=_ep_plain, out_dtype=F32, name="in_proj_gla")
    gate_a = proj([wl[:, o_g:o_g + D_MODEL]], epilogue=_ep_sigmoid, out_dtype=BF16,
                  name="in_proj_gate_a")
    gate_b = proj([wl[:, o_g + D_MODEL:]], epilogue=_ep_sigmoid, out_dtype=BF16,
                  name="in_proj_gate_b")
    la = gla_gates(h, wl[:, o_lr:o_g], gla_gate_w2[l], gla_gate_b[l])

    z_a = hyena_mixer(u_hy, conv_w[l], conv_b[l], filt, hyena_skip[l], tables)
    o_f, o_b = gla_scan(qkvr, la)
    z_b = gla_post(o_f, o_b, qkvr, gla_norm[l])

    mixed = matmul([z_a, z_b], [w_branch_a[l].astype(BF16), w_branch_b[l].astype(BF16)],
                   [(0, 0), (1, 1)], _ep_merge, [(gate_a, "mn"), (gate_b, "mn")], BF16,
                   tm=1024, tn=512, name="branch_merge")
    return matmul([mixed], [w_out[l].astype(BF16)], [(0, 0)], _ep_residual, [(x, "mn")], F32,
                  tm=1024, tn=512, name="out_proj")


def _swiglu_ffn(x, h, wg, wu, wd, scale=None):
    if scale is None:
        t = matmul([h], [wg, wu], [(0, 0), (0, 1)], _ep_swiglu, [], BF16,
                   tm=1024, tn=512, name="ffn_up")
    else:
        t = matmul([h], [wg, wu], [(0, 0), (0, 1)], _ep_swiglu_scaled, [(scale, "m")], BF16,
                   tm=1024, tn=512, name="ffn_up_scaled")
    return matmul([t], [wd], [(0, 0)], _ep_residual, [(x, "mn")], F32,
                  tm=512, tn=512, name="ffn_down")


def kernel(x, norm_mix, w_in, conv_w, conv_b, filt_w1, filt_b1, filt_freq1, filt_w2, filt_b2, filt_freq2, filt_w3, filt_b3, filt_freq3, filt_w4, hyena_skip, w_branch_a, gla_gate_w2, gla_gate_b, gla_norm, w_branch_b, w_out, norm_ffn, dense_w_gate, dense_w_up, dense_w_down, router_w, moe_w_gate, moe_w_up, moe_w_down, norm_final):
    b, L, d = x.shape
    assert b == 1
    x = x.reshape(L, d)
    tables = _fft_tables(L)
    for l in range(DEPTH):
        filt = (filt_w1[l], filt_b1[l], filt_freq1[l], filt_w2[l], filt_b2[l], filt_freq2[l],
                filt_w3[l], filt_b3[l], filt_freq3[l], filt_w4[l])
        x = _layer_mixers(x, l, tables, norm_mix, w_in, conv_w, conv_b, filt, hyena_skip,
                          w_branch_a, gla_gate_w2, gla_gate_b, gla_norm, w_branch_b, w_out)
        i = l // 2
        if l % 2 == 0:
            h = rmsnorm(x, norm_ffn[l], BF16)
            x = _swiglu_ffn(x, h, dense_w_gate[i].astype(BF16), dense_w_up[i].astype(BF16),
                            dense_w_down[i].astype(BF16))
        else:
            h, cw = rmsnorm_router(x, norm_ffn[l], router_w[i])
            for e in range(N_EXPERTS):
                scale = jnp.broadcast_to(cw[:, e:e + 1], (L, LANES))
                x = _swiglu_ffn(x, h, moe_w_gate[i, e].astype(BF16), moe_w_up[i, e].astype(BF16),
                                moe_w_down[i, e].astype(BF16), scale)
    return rmsnorm(x, norm_final, F32).reshape(b, L, d)
```

```python
import functools
import math

import jax
import jax.numpy as jnp
from jax import lax
from jax.experimental import pallas as pl
from jax.experimental.pallas import tpu as pltpu

F32 = jnp.float32
BF16 = jnp.bfloat16

D_MODEL = 2048
DEPTH = 2
HYENA_WIDTH = D_MODEL // 2
HYENA_ORDER = 2
SHORT_CONV = 3
POS_EMB_DIM = 33
POS_BANDS = (POS_EMB_DIM - 1) // 2
FILTER_HIDDEN = 64
MIN_DECAY = math.log(1e-2) / 1.5
MAX_DECAY = math.log(1e-2) / 0.3
GLA_HEADS = 4
GLA_DK = 128
GLA_DV = 256
GLA_DK_TOTAL = GLA_HEADS * GLA_DK
GLA_DV_TOTAL = GLA_HEADS * GLA_DV
GLA_LOWRANK = 16
GLA_GATE_TAU = 16.0
GLA_CHUNK = 64
N_EXPERTS = 8
RMS_EPS = 1e-6

LANES = 128
VMEM_LIMIT_BYTES = 56 * 1024 * 1024


def _params(*sem):
    return pltpu.CompilerParams(dimension_semantics=sem, vmem_limit_bytes=VMEM_LIMIT_BYTES)


def _dot(a, b):
    return jnp.dot(a, b, preferred_element_type=F32)


def _split(a):
    hi = a.astype(BF16)
    lo = (a - hi.astype(F32)).astype(BF16)
    return hi, lo


def _dot3(a, b):
    ah, al = _split(a)
    bh, bl = _split(b)
    return _dot(ah, bh) + (_dot(ah, bl) + _dot(al, bh))


def _sigmoid(x):
    return 1.0 / (1.0 + jnp.exp(-x))


def _silu(x):
    return x * _sigmoid(x)


def _log_sigmoid(x):
    return jnp.minimum(x, 0.0) - jnp.log(1.0 + jnp.exp(-jnp.abs(x)))


def _rmsnorm_kernel(x_ref, g_ref, o_ref):
    x = x_ref[...]
    ms = jnp.mean(x * x, axis=-1, keepdims=True)
    o_ref[...] = (x * lax.rsqrt(ms + RMS_EPS) * g_ref[...]).astype(o_ref.dtype)


def rmsnorm(x, g, out_dtype, tm=256):
    m, d = x.shape
    return pl.pallas_call(
        _rmsnorm_kernel,
        out_shape=jax.ShapeDtypeStruct((m, d), out_dtype),
        grid=(m // tm,),
        in_specs=[pl.BlockSpec((tm, d), lambda i: (i, 0)),
                  pl.BlockSpec((1, d), lambda i: (0, 0))],
        out_specs=pl.BlockSpec((tm, d), lambda i: (i, 0)),
        compiler_params=_params("parallel"),
        name="rmsnorm",
    )(x, g.reshape(1, d))


def _rmsnorm_router_kernel(x_ref, g_ref, rw_ref, o_ref, cw_ref):
    x = x_ref[...]
    ms = jnp.mean(x * x, axis=-1, keepdims=True)
    h = x * lax.rsqrt(ms + RMS_EPS) * g_ref[...]
    o_ref[...] = h.astype(o_ref.dtype)
    logits = _dot3(h, rw_ref[...])
    col = lax.broadcasted_iota(jnp.int32, logits.shape, 1).astype(F32)
    neg = jnp.float32(-jnp.inf)
    logits = jnp.where(col < N_EXPERTS, logits, neg)
    m1 = jnp.max(logits, axis=-1, keepdims=True)
    i1 = jnp.min(jnp.where(logits == m1, col, float(LANES)), axis=-1, keepdims=True)
    rest = jnp.where(col == i1, neg, logits)
    m2 = jnp.max(rest, axis=-1, keepdims=True)
    i2 = jnp.min(jnp.where(rest == m2, col, float(LANES)), axis=-1, keepdims=True)
    e2 = jnp.exp(m2 - m1)
    g1 = 1.0 / (1.0 + e2)
    g2 = e2 / (1.0 + e2)
    cw_ref[...] = (jnp.where(col == ROUTE_E1, i1, 0.0) + jnp.where(col == ROUTE_E2, i2, 0.0)
                   + jnp.where(col == ROUTE_G1, g1, 0.0) + jnp.where(col == ROUTE_G2, g2, 0.0))


ROUTE_E1, ROUTE_E2, ROUTE_G1, ROUTE_G2 = 0, 1, 2, 3


def rmsnorm_router(x, g, router_w, tm=256):
    m, d = x.shape
    rw = jnp.zeros((d, LANES), F32).at[:, :N_EXPERTS].set(router_w)
    return pl.pallas_call(
        _rmsnorm_router_kernel,
        out_shape=(jax.ShapeDtypeStruct((m, d), F32), jax.ShapeDtypeStruct((m, LANES), F32)),
        grid=(m // tm,),
        in_specs=[pl.BlockSpec((tm, d), lambda i: (i, 0)),
                  pl.BlockSpec((1, d), lambda i: (0, 0)),
                  pl.BlockSpec((d, LANES), lambda i: (0, 0))],
        out_specs=(pl.BlockSpec((tm, d), lambda i: (i, 0)),
                   pl.BlockSpec((tm, LANES), lambda i: (i, 0))),
        compiler_params=_params("parallel"),
        name="rmsnorm_router",
    )(x, g.reshape(1, d), rw)


def _mm_kernel(*refs, n_a, n_w, dots, n_extra, epilogue):
    a_refs = refs[:n_a]
    w_refs = refs[n_a:n_a + n_w]
    e_refs = refs[n_a + n_w:n_a + n_w + n_extra]
    o_ref = refs[n_a + n_w + n_extra]
    a_vals = [a[...].astype(BF16) for a in a_refs]
    accs = [_dot(a_vals[ai], w_refs[wi][...]) for ai, wi in dots]
    o_ref[...] = epilogue(accs, [e[...] for e in e_refs]).astype(o_ref.dtype)


def matmul(a_list, w_list, dots, epilogue, extras, out_dtype, *, tm, tn, name):
    m = a_list[0].shape[0]
    n = w_list[0].shape[1]
    assert m % tm == 0 and n % tn == 0
    in_specs = []
    for a in a_list:
        in_specs.append(pl.BlockSpec((tm, a.shape[1]), lambda i, j: (i, 0)))
    for w in w_list:
        in_specs.append(pl.BlockSpec((w.shape[0], tn), lambda i, j: (0, j)))
    extra_arrays = []
    for arr, kind in extras:
        extra_arrays.append(arr)
        if kind == "mn":
            in_specs.append(pl.BlockSpec((tm, tn), lambda i, j: (i, j)))
        else:
            in_specs.append(pl.BlockSpec((tm, arr.shape[1]), lambda i, j: (i, 0)))
    kern = functools.partial(_mm_kernel, n_a=len(a_list), n_w=len(w_list), dots=tuple(dots),
                             n_extra=len(extras), epilogue=epilogue)
    return pl.pallas_call(
        kern,
        out_shape=jax.ShapeDtypeStruct((m, n), out_dtype),
        grid=(m // tm, n // tn),
        in_specs=in_specs,
        out_specs=pl.BlockSpec((tm, tn), lambda i, j: (i, j)),
        compiler_params=_params("parallel", "parallel"),
        name=name,
    )(*a_list, *w_list, *extra_arrays)


def _ep_plain(accs, ex):
    return accs[0]


def _ep_sigmoid(accs, ex):
    return _sigmoid(accs[0])


def _ep_residual(accs, ex):
    return ex[0] + accs[0]


def _ep_merge(accs, ex):
    return ex[0].astype(F32) * accs[0] + ex[1].astype(F32) * accs[1]


def _ep_swiglu(accs, ex):
    return _silu(accs[0]) * accs[1]


def _gate_kernel(h_ref, wlr_ref, w2_ref, b_ref, o_ref):
    lr = _dot(h_ref[...], wlr_ref[...])
    z = _dot3(lr, w2_ref[...]) + b_ref[...]
    o_ref[...] = _log_sigmoid(z) * (1.0 / GLA_GATE_TAU)


def gla_gates(h, w_lr, gate_w2, gate_b, tm=512):
    m, d = h.shape
    r = GLA_LOWRANK
    n = 2 * GLA_DK_TOTAL
    wlr = jnp.zeros((d, LANES), BF16).at[:, :2 * r].set(w_lr)
    w2 = jnp.zeros((LANES, n), F32)
    w2 = w2.at[:r, :GLA_DK_TOTAL].set(gate_w2[0]).at[r:2 * r, GLA_DK_TOTAL:].set(gate_w2[1])
    b = gate_b.reshape(1, n)
    return pl.pallas_call(
        _gate_kernel,
        out_shape=jax.ShapeDtypeStruct((m, n), F32),
        grid=(m // tm,),
        in_specs=[pl.BlockSpec((tm, d), lambda i: (i, 0)),
                  pl.BlockSpec((d, LANES), lambda i: (0, 0)),
                  pl.BlockSpec((LANES, n), lambda i: (0, 0)),
                  pl.BlockSpec((1, n), lambda i: (0, 0))],
        out_specs=pl.BlockSpec((tm, n), lambda i: (i, 0)),
        compiler_params=_params("parallel"),
        name="gla_gates",
    )(h, wlr, w2, b)


GLA_SUB = 4


def _gla_chunk(q, k, v, la, st_ref, h, reverse):
    c = GLA_CHUNK
    row = lax.broadcasted_iota(jnp.int32, (c, c), 0)
    col = lax.broadcasted_iota(jnp.int32, (c, c), 1)
    keep = (col > row) if reverse else (col <= row)
    csum = (col >= row) if reverse else (col <= row)
    tri = jnp.where(csum, 1.0, 0.0).astype(BF16)
    la_hi, la_lo = _split(la)
    b = _dot(tri, la_hi) + _dot(tri, la_lo)
    tot = jnp.sum(la, axis=0, keepdims=True)
    q_in = (q * ((GLA_DK ** -0.5) * jnp.exp(b))).astype(BF16)
    k_in = (k * jnp.exp(-b)).astype(BF16)
    k_out = (k * jnp.exp(tot - b)).astype(BF16)
    vb = v.astype(BF16)
    s = lax.dot_general(q_in, k_in, (((1,), (1,)), ((), ())), preferred_element_type=F32)
    s = jnp.where(keep, s, 0.0).astype(BF16)
    st = st_ref[h]
    o = _dot(s, vb) + lax.dot_general(q_in, st.astype(BF16), (((1,), (1,)), ((), ())),
                                      preferred_element_type=F32)
    upd = lax.dot_general(vb, k_out, (((0,), (0,)), ((), ())), preferred_element_type=F32)
    st_ref[h] = st * jnp.exp(tot) + upd
    return o


def _gla_kernel(qf_ref, kf_ref, vf_ref, lf_ref, qb_ref, kb_ref, vb_ref, lb_ref,
                of_ref, ob_ref, sf_ref, sb_ref):
    @pl.when(pl.program_id(0) == 0)
    def _():
        sf_ref[...] = jnp.zeros_like(sf_ref)
        sb_ref[...] = jnp.zeros_like(sb_ref)

    def body(s, carry):
        rf = pl.multiple_of(s * GLA_CHUNK, GLA_CHUNK)
        rb = pl.multiple_of((GLA_SUB - 1 - s) * GLA_CHUNK, GLA_CHUNK)
        for h in range(GLA_HEADS):
            kk = slice(h * GLA_DK, (h + 1) * GLA_DK)
            vv = slice(h * GLA_DV, (h + 1) * GLA_DV)
            of_ref[pl.ds(rf, GLA_CHUNK), vv] = _gla_chunk(
                qf_ref[pl.ds(rf, GLA_CHUNK), kk], kf_ref[pl.ds(rf, GLA_CHUNK), kk],
                vf_ref[pl.ds(rf, GLA_CHUNK), vv], lf_ref[pl.ds(rf, GLA_CHUNK), kk],
                sf_ref, h, False)
            ob_ref[pl.ds(rb, GLA_CHUNK), vv] = _gla_chunk(
                qb_ref[pl.ds(rb, GLA_CHUNK), kk], kb_ref[pl.ds(rb, GLA_CHUNK), kk],
                vb_ref[pl.ds(rb, GLA_CHUNK), vv], lb_ref[pl.ds(rb, GLA_CHUNK), kk],
                sb_ref, h, True)
        return carry

    lax.fori_loop(0, GLA_SUB, body, 0)


def gla_scan(qkvr, la):
    L = qkvr.shape[0]
    rows = GLA_CHUNK * GLA_SUB
    n = L // rows
    dk, dv = GLA_DK_TOTAL, GLA_DV_TOTAL
    specs = []
    for rev in (False, True):
        blk = (lambda i: n - 1 - i) if rev else (lambda i: i)
        specs += [pl.BlockSpec((rows, dk), lambda i, blk=blk: (blk(i), 0)),
                  pl.BlockSpec((rows, dk), lambda i, blk=blk: (blk(i), 1)),
                  pl.BlockSpec((rows, dv), lambda i, blk=blk: (blk(i), 1)),
                  pl.BlockSpec((rows, dk), lambda i, blk=blk, c=int(rev): (blk(i), c))]
    return pl.pallas_call(
        _gla_kernel,
        out_shape=(jax.ShapeDtypeStruct((L, dv), F32), jax.ShapeDtypeStruct((L, dv), F32)),
        grid=(n,),
        in_specs=specs,
        out_specs=(pl.BlockSpec((rows, dv), lambda i: (i, 0)),
                   pl.BlockSpec((rows, dv), lambda i: (n - 1 - i, 0))),
        scratch_shapes=[pltpu.VMEM((GLA_HEADS, GLA_DV, GLA_DK), F32),
                        pltpu.VMEM((GLA_HEADS, GLA_DV, GLA_DK), F32)],
        compiler_params=_params("arbitrary"),
        name="gla_scan",
    )(qkvr, qkvr, qkvr, la, qkvr, qkvr, qkvr, la)


def _gla_post_kernel(of_ref, ob_ref, r_ref, g_ref, o_ref):
    g = g_ref[...]
    for h in range(GLA_HEADS):
        vv = slice(h * GLA_DV, (h + 1) * GLA_DV)
        o = of_ref[:, vv] + ob_ref[:, vv]
        ms = jnp.mean(o * o, axis=-1, keepdims=True)
        y = o * lax.rsqrt(ms + RMS_EPS) * g
        o_ref[:, vv] = (y * _silu(r_ref[:, vv])).astype(o_ref.dtype)


def gla_post(o_f, o_b, qkvr, norm_g, tm=512):
    L, dv = o_f.shape
    return pl.pallas_call(
        _gla_post_kernel,
        out_shape=jax.ShapeDtypeStruct((L, dv), BF16),
        grid=(L // tm,),
        in_specs=[pl.BlockSpec((tm, dv), lambda i: (i, 0)),
                  pl.BlockSpec((tm, dv), lambda i: (i, 0)),
                  pl.BlockSpec((tm, dv), lambda i: (i, 2)),
                  pl.BlockSpec((1, GLA_DV), lambda i: (0, 0))],
        out_specs=pl.BlockSpec((tm, dv), lambda i: (i, 0)),
        compiler_params=_params("parallel"),
        name="gla_post",
    )(o_f, o_b, qkvr, norm_g.reshape(1, GLA_DV))


CT = 256
CT_COL = 128
MID_SLABS = 8
SUB = 8


def _fft_dims(L):
    n = 2 * L
    na = 1 << ((n.bit_length() - 1) // 2)
    nb = n // na
    assert na * nb == n and nb % SUB == 0 and (na // 2) % 8 == 0
    kp = -(-(na // 2 + 1) // 8) * 8
    return n, na, nb, kp


def _cis(num, den):
    ang = (2.0 * math.pi / den) * (num % den).astype(F32)
    return jnp.cos(ang), jnp.sin(ang)


def _fft_tables(L):
    n, na, nb, kp = _fft_dims(L)
    ha, ng = na // 2, nb // SUB
    ar = lambda m: jnp.arange(m, dtype=jnp.int32)
    eye = jnp.eye(SUB, dtype=F32)
    g_, ka_, bl_, a_ = ar(ng)[:, None, None, None], ar(kp)[None, :, None, None], \
        ar(SUB)[None, None, :, None], ar(ha)[None, None, None, :]
    c, s = _cis(ka_ * (nb * a_ + SUB * g_ + bl_), n)
    cs = jnp.stack([c, -s], axis=2)
    m8 = cs[..., None] * eye[None, None, None, :, None, :]
    m8 = m8.reshape(ng, kp * 2 * SUB, ha * SUB)
    c, s = _cis(ar(nb)[:, None] * ar(nb)[None, :], nb)
    f2 = jnp.stack([jnp.stack([c, s], axis=1), jnp.stack([-s, c], axis=1)], axis=0)
    f2 = f2.reshape(2, nb, 2, ng, SUB).transpose(0, 1, 3, 2, 4).reshape(2 * nb, 2 * nb)
    c, s = _cis(ar(nb)[None, :, None] * (ar(kp)[:, None, None] + na * ar(nb)[None, None, :]), n)
    c, s = c / n, s / n
    gi = jnp.stack([jnp.stack([c, -s], axis=2), jnp.stack([s, c], axis=2)], axis=1)
    gi = gi.reshape(kp, 2, ng, SUB, 2, nb).transpose(0, 2, 1, 3, 4, 5).reshape(kp, 2 * nb, 2 * nb)
    kc = ar(kp)[None, :]
    wgt = jnp.where((kc == 0) | (kc == ha), 1.0, jnp.where(kc < ha, 2.0, 0.0))
    c, s = _cis(ar(ha)[:, None] * kc, na)
    pm = jnp.stack([wgt * c, -wgt * s], axis=2)
    p8 = pm[:, None, :, :, None] * eye[None, :, None, None, :]
    p8 = p8.reshape(ha * SUB, kp * 2 * SUB)
    return tuple(t.astype(BF16) for t in (m8, f2, gi, p8))


def _short_conv_kernel(u_ref, w_ref, b_ref, o_ref):
    u = u_ref[...]
    L = u.shape[0]
    row = lax.broadcasted_iota(jnp.int32, u.shape, 0)
    prev = jnp.where(row == 0, 0.0, pltpu.roll(u, 1, 0))
    nxt = jnp.where(row == L - 1, 0.0, pltpu.roll(u, L - 1, 0))
    w = w_ref[...]
    o_ref[...] = prev * w[0:1] + u * w[1:2] + nxt * w[2:3] + b_ref[...]


def short_conv(u, w, b):
    L, c = u.shape
    w8 = jnp.zeros((8, c), F32).at[:SHORT_CONV].set(w)
    return pl.pallas_call(
        _short_conv_kernel,
        out_shape=jax.ShapeDtypeStruct((L, c), F32),
        grid=(c // CT_COL,),
        in_specs=[pl.BlockSpec((L, CT_COL), lambda j: (0, j)),
                  pl.BlockSpec((8, CT_COL), lambda j: (0, j)),
                  pl.BlockSpec((1, CT_COL), lambda j: (0, j))],
        out_specs=pl.BlockSpec((L, CT_COL), lambda j: (0, j)),
        compiler_params=_params("parallel"),
        name="short_conv",
    )(u, w8, b.reshape(1, c))


def _fft_s1_kernel(u_ref, m8_ref, a_ref):
    ha, sub, ct = u_ref.shape
    x = u_ref[...].reshape(ha * sub, ct).astype(BF16)
    r = _dot(m8_ref[0], x)
    a_ref[:, 0, :, :] = r.astype(a_ref.dtype).reshape(a_ref.shape[0], 2 * sub, ct)


def fft_stage1(u, ncols, m8, L):
    n, na, nb, kp = _fft_dims(L)
    ha, ng = na // 2, nb // SUB
    u3 = u.reshape(ha, nb, u.shape[1])
    return pl.pallas_call(
        _fft_s1_kernel,
        out_shape=jax.ShapeDtypeStruct((kp, ng, 2 * SUB, ncols), BF16),
        grid=(ng, ncols // CT),
        in_specs=[pl.BlockSpec((ha, SUB, CT), lambda g, j: (0, g, j)),
                  pl.BlockSpec((1, kp * 2 * SUB, ha * SUB), lambda g, j: (g, 0, 0))],
        out_specs=pl.BlockSpec((kp, 1, 2 * SUB, CT), lambda g, j: (0, g, 0, j)),
        compiler_params=_params("parallel", "parallel"),
        name="fft_stage1",
    )(u3, m8)


def _filter_s1_kernel(z_ref, w1_ref, b1_ref, f1_ref, w2_ref, b2_ref, f2_ref, w3_ref, b3_ref,
                      f3_ref, w4_ref, dl_ref, m8_ref, a_ref, nrm_ref, *, nb, L, cb):
    g, j = pl.program_id(0), pl.program_id(1)
    ha, sub, _ = z_ref.shape
    rows = ha * sub
    z = z_ref[...].reshape(rows, z_ref.shape[2])
    h = jnp.sin(f1_ref[...] * (_dot3(z, w1_ref[...]) + b1_ref[...]))
    h = jnp.sin(f2_ref[...] * (_dot3(h, w2_ref[...]) + b2_ref[...]))
    h = jnp.sin(f3_ref[...] * (_dot3(h, w3_ref[...]) + b3_ref[...]))
    r = lax.broadcasted_iota(jnp.int32, (rows, CT), 0)
    tpos = (r // sub) * nb + g * sub + r % sub
    decay = jnp.exp(-(tpos.astype(F32) * (1.0 / (L - 1))) * dl_ref[...])
    k = _dot3(h, w4_ref[...]) * decay
    backward = (j // cb) % 2 == 1
    k = jnp.where(jnp.logical_and(backward, tpos == 0), 0.0, k)
    nrm_ref[0] = jnp.sum(jnp.abs(k), axis=0, keepdims=True)
    q = _dot(m8_ref[0], k.astype(BF16))
    a_ref[:, 0, :, :] = q.astype(a_ref.dtype).reshape(a_ref.shape[0], 2 * sub, CT)


def filter_stage1(L, w1, b1, f1, w2, b2, f2, w3, b3, f3, w4, m8):
    n, na, nb, kp = _fft_dims(L)
    ha, ng = na // 2, nb // SUB
    c = HYENA_WIDTH
    hdim = FILTER_HIDDEN
    t = jnp.linspace(0.0, 1.0, L, dtype=F32)[:, None]
    omega = 2.0 * math.pi * jnp.arange(L, dtype=F32)[:, None] / L
    bands = jnp.linspace(1e-4, POS_BANDS - 1, POS_BANDS, dtype=F32)[None, :]
    z = jnp.concatenate([t, jnp.cos(bands * omega), -jnp.sin(bands * omega)], axis=-1)
    z = jnp.zeros((L, LANES), F32).at[:, :POS_EMB_DIM].set(z).reshape(ha, nb, LANES)
    w1p = jnp.zeros((LANES, hdim), F32).at[:POS_EMB_DIM].set(w1)
    deltas = jnp.abs(jnp.linspace(MIN_DECAY, MAX_DECAY, c, dtype=F32)).reshape(1, c)
    nc = w4.shape[1]
    cb = c // CT
    row = lambda v: v.reshape(1, hdim)
    full = lambda shape: pl.BlockSpec(shape, lambda g, j: (0,) * len(shape))
    kern = functools.partial(_filter_s1_kernel, nb=nb, L=L, cb=cb)
    return pl.pallas_call(
        kern,
        out_shape=(jax.ShapeDtypeStruct((kp, ng, 2 * SUB, nc), BF16),
                   jax.ShapeDtypeStruct((ng, 1, nc), F32)),
        grid=(ng, nc // CT),
        in_specs=[pl.BlockSpec((ha, SUB, LANES), lambda g, j: (0, g, 0)),
                  full((LANES, hdim)), full((1, hdim)), full((1, hdim)),
                  full((hdim, hdim)), full((1, hdim)), full((1, hdim)),
                  full((hdim, hdim)), full((1, hdim)), full((1, hdim)),
                  pl.BlockSpec((hdim, CT), lambda g, j: (0, j)),
                  pl.BlockSpec((1, CT), lambda g, j: (0, j % cb)),
                  pl.BlockSpec((1, kp * 2 * SUB, ha * SUB), lambda g, j: (g, 0, 0))],
        out_specs=(pl.BlockSpec((kp, 1, 2 * SUB, CT), lambda g, j: (0, g, 0, j)),
                   pl.BlockSpec((1, 1, CT), lambda g, j: (g, 0, j))),
        compiler_params=_params("parallel", "parallel"),
        name="hyena_filter_stage1",
    )(z, w1p, row(b1), row(f1), w2, row(b2), row(f2), w3, row(b3), row(f3), w4, deltas, m8)


def _slab(ref, s):
    _, ng, rows, ct = ref.shape
    return ref[s].reshape(ng * rows, ct)


def _filter_spec_kernel(af_ref, ab_ref, f2_ref, nf_ref, nbk_ref, h_ref):
    f2 = f2_ref[...]
    nb = f2.shape[0] // 2
    inv = 1.0 / (jnp.sum(nf_ref[...], axis=0) + jnp.sum(nbk_ref[...], axis=0))
    for s in range(MID_SLABS):
        xf = _dot(f2, _slab(af_ref, s))
        xb = _dot(f2, _slab(ab_ref, s))
        h_ref[0, s] = ((xf[:nb] + xb[:nb]) * inv).astype(h_ref.dtype)
        h_ref[1, s] = ((xf[nb:] - xb[nb:]) * inv).astype(h_ref.dtype)


def filter_spectrum(a, nrm, f2, L):
    n, na, nb, kp = _fft_dims(L)
    ng = nb // SUB
    c = HYENA_WIDTH
    cb = c // CT
    fcol = lambda j: (j // cb) * 2 * cb + j % cb
    return pl.pallas_call(
        _filter_spec_kernel,
        out_shape=jax.ShapeDtypeStruct((2, kp, nb, HYENA_ORDER * c), BF16),
        grid=(HYENA_ORDER * cb, kp // MID_SLABS),
        in_specs=[pl.BlockSpec((MID_SLABS, ng, 2 * SUB, CT), lambda j, k: (k, 0, 0, fcol(j))),
                  pl.BlockSpec((MID_SLABS, ng, 2 * SUB, CT), lambda j, k: (k, 0, 0, fcol(j) + cb)),
                  pl.BlockSpec((2 * nb, 2 * nb), lambda j, k: (0, 0)),
                  pl.BlockSpec((ng, 1, CT), lambda j, k: (0, 0, fcol(j))),
                  pl.BlockSpec((ng, 1, CT), lambda j, k: (0, 0, fcol(j) + cb))],
        out_specs=pl.BlockSpec((2, MID_SLABS, nb, CT), lambda j, k: (0, k, 0, j)),
        compiler_params=_params("parallel", "parallel"),
        name="hyena_filter_spectrum",
    )(a, a, f2, nrm, nrm)


def _fft_mid_kernel(a_ref, h_ref, f2_ref, g_ref, q_ref):
    f2 = f2_ref[...]
    nb = f2.shape[0] // 2
    _, ng, rows, ct = q_ref.shape
    for s in range(MID_SLABS):
        x = _dot(f2, _slab(a_ref, s))
        xr, xi = x[:nb], x[nb:]
        hr, hi = h_ref[0, s].astype(F32), h_ref[1, s].astype(F32)
        y = jnp.concatenate([xr * hr - xi * hi, xr * hi + xi * hr], axis=0).astype(BF16)
        q_ref[s] = _dot(g_ref[s], y).astype(q_ref.dtype).reshape(ng, rows, ct)


def fft_mid(a, hf, order, f2, g, L):
    n, na, nb, kp = _fft_dims(L)
    ng = nb // SUB
    c = a.shape[3]
    cb = c // CT
    return pl.pallas_call(
        _fft_mid_kernel,
        out_shape=jax.ShapeDtypeStruct(a.shape, BF16),
        grid=(kp // MID_SLABS, cb),
        in_specs=[pl.BlockSpec((MID_SLABS, ng, 2 * SUB, CT), lambda k, j: (k, 0, 0, j)),
                  pl.BlockSpec((2, MID_SLABS, nb, CT), lambda k, j: (0, k, 0, order * cb + j)),
                  pl.BlockSpec((2 * nb, 2 * nb), lambda k, j: (0, 0)),
                  pl.BlockSpec((MID_SLABS, 2 * nb, 2 * nb), lambda k, j: (k, 0, 0))],
        out_specs=pl.BlockSpec((MID_SLABS, ng, 2 * SUB, CT), lambda k, j: (k, 0, 0, j)),
        compiler_params=_params("parallel", "parallel"),
        name="fft_mid",
    )(a, hf, f2, g)


def _fft_i2_kernel(q_ref, p8_ref, u_ref, x_ref, s_ref, o_ref):
    kp, _, rows, ct = q_ref.shape
    q = q_ref[:, 0, :, :].reshape(kp * rows, ct)
    y = _dot(p8_ref[...], q).reshape(o_ref.shape)
    o_ref[...] = x_ref[...] * (y + u_ref[...] * s_ref[...])


def fft_stage_out(q, u, gate_src, gate_blk, skip, p8, L):
    n, na, nb, kp = _fft_dims(L)
    ha, ng = na // 2, nb // SUB
    c = q.shape[3]
    u3 = u.reshape(ha, nb, u.shape[1])
    g3 = gate_src.reshape(ha, nb, gate_src.shape[1])
    out = pl.pallas_call(
        _fft_i2_kernel,
        out_shape=jax.ShapeDtypeStruct((ha, nb, c), F32),
        grid=(ng, c // CT),
        in_specs=[pl.BlockSpec((kp, 1, 2 * SUB, CT), lambda g, j: (0, g, 0, j)),
                  pl.BlockSpec((ha * SUB, kp * 2 * SUB), lambda g, j: (0, 0)),
                  pl.BlockSpec((ha, SUB, CT), lambda g, j: (0, g, j)),
                  pl.BlockSpec((ha, SUB, CT), lambda g, j: (0, g, gate_blk + j)),
                  pl.BlockSpec((1, 1, CT), lambda g, j: (0, 0, j))],
        out_specs=pl.BlockSpec((ha, SUB, CT), lambda g, j: (0, g, j)),
        compiler_params=_params("parallel", "parallel"),
        name="fft_stage_out",
    )(q, p8, u3, g3, skip.reshape(1, 1, c))
    return out.reshape(L, c)


def hyena_mixer(u_hy, conv_w, conv_b, filt, skip, tables):
    L = u_hy.shape[0]
    c = HYENA_WIDTH
    m8, f2, g, p8 = tables
    uc = short_conv(u_hy, conv_w, conv_b)
    a_filt, nrm = filter_stage1(L, *filt, m8)
    hf = filter_spectrum(a_filt, nrm, f2, L)
    z = uc
    for order in range(HYENA_ORDER):
        a = fft_stage1(z, c, m8, L)
        q = fft_mid(a, hf, order, f2, g, L)
        z = fft_stage_out(q, z, uc, (order + 1) * (c // CT), skip[order], p8, L)
    return z


MOE_TM = 512
MOE_TN = 512
COMBINE_TM = 256


def _moe_plan(route, L):
    tm = MOE_TM
    nt = -(-(2 * L + N_EXPERTS * (tm - 1)) // tm)
    e_flat = jnp.concatenate([route[:, ROUTE_E1], route[:, ROUTE_E2]]).astype(jnp.int32)
    onehot = (e_flat[:, None] == jnp.arange(N_EXPERTS, dtype=jnp.int32)[None, :]).astype(jnp.int32)
    csum = jnp.cumsum(onehot, axis=0)
    rank = jnp.sum(onehot * (csum - 1), axis=1)
    counts = csum[-1]
    padded = ((counts + tm - 1) // tm) * tm
    ends = jnp.cumsum(padded)
    dest = (ends - padded)[e_flat] + rank
    tok = jnp.tile(jnp.arange(L, dtype=jnp.int32), 2)
    row_token = jnp.zeros((nt * tm,), jnp.int32).at[dest].set(tok, unique_indices=True)
    n_used = ends[-1] // tm
    tile_row = jnp.arange(nt, dtype=jnp.int32) * tm
    tile_expert = jnp.sum((tile_row[:, None] >= ends[None, :]).astype(jnp.int32), axis=1)
    last = jnp.take(tile_expert, jnp.maximum(n_used - 1, 0))
    tile_expert = jnp.where(jnp.arange(nt) < n_used, tile_expert, last)
    tile_first = jnp.concatenate([jnp.ones((1,), jnp.int32),
                                  (tile_expert[1:] != tile_expert[:-1]).astype(jnp.int32)])
    return dict(nt=nt, row_token=row_token, dest=dest, tile_expert=tile_expert,
                tile_first=tile_first, n_used=n_used.reshape(1).astype(jnp.int32))


def _row_copy(src_hbm, idx, buf, slot, r, sem):
    return pltpu.make_async_copy(src_hbm.at[pl.ds(idx, 1)], buf.at[slot, pl.ds(r, 1)],
                                 sem.at[slot])


def _gather_rows(idx_ref, src_hbm, buf, slot, sem, n_rows, start):
    def body(r, carry):
        cp = _row_copy(src_hbm, idx_ref[0, 0, r], buf, slot, r, sem)
        if start:
            cp.start()
        else:
            cp.wait()
        return carry
    lax.fori_loop(0, n_rows, body, 0)


def _pipelined_gather(cur_ref, nxt_ref, src_hbm, buf, sem, n_rows):
    i, n = pl.program_id(0), pl.num_programs(0)
    slot = lax.rem(i, 2)

    @pl.when(i == 0)
    def _():
        _gather_rows(cur_ref, src_hbm, buf, 0, sem, n_rows, True)

    @pl.when(i + 1 < n)
    def _():
        _gather_rows(nxt_ref, src_hbm, buf, 1 - slot, sem, n_rows, True)

    _gather_rows(cur_ref, src_hbm, buf, slot, sem, n_rows, False)
    return slot


def _dispatch_kernel(cur_ref, nxt_ref, h_hbm, o_ref, buf, sem):
    slot = _pipelined_gather(cur_ref, nxt_ref, h_hbm, buf, sem, MOE_TM)
    o_ref[...] = buf[slot].astype(o_ref.dtype)


def moe_dispatch(h, row_token, nt):
    L, d = h.shape
    idx = row_token.reshape(nt, 1, MOE_TM)
    smem = lambda f: pl.BlockSpec((1, 1, MOE_TM), f, memory_space=pltpu.SMEM)
    return pl.pallas_call(
        _dispatch_kernel,
        out_shape=jax.ShapeDtypeStruct((nt * MOE_TM, d), BF16),
        grid=(nt,),
        in_specs=[smem(lambda i: (i, 0, 0)),
                  smem(lambda i: (jnp.minimum(i + 1, nt - 1), 0, 0)),
                  pl.BlockSpec(memory_space=pl.ANY)],
        out_specs=pl.BlockSpec((MOE_TM, d), lambda i: (i, 0)),
        scratch_shapes=[pltpu.VMEM((2, MOE_TM, d), F32), pltpu.SemaphoreType.DMA((2,))],
        compiler_params=_params("arbitrary"),
        name="moe_dispatch",
    )(idx, idx, h)


def _moe_up_kernel(te_ref, tf_ref, nu_ref, a_ref, wg_ref, wu_ref, o_ref, wgb, wub):
    i = pl.program_id(1)

    @pl.when(tf_ref[i] == 1)
    def _():
        wgb[...] = wg_ref[0].astype(BF16)
        wub[...] = wu_ref[0].astype(BF16)

    @pl.when(i < nu_ref[0])
    def _():
        a = a_ref[...]
        o_ref[...] = (_silu(_dot(a, wgb[...])) * _dot(a, wub[...])).astype(o_ref.dtype)

    @pl.when(i >= nu_ref[0])
    def _():
        o_ref[...] = jnp.zeros_like(o_ref)


def moe_up(hs, w_gate, w_up, plan):
    r, d = hs.shape
    f = w_gate.shape[2]
    nt = plan["nt"]
    wspec = pl.BlockSpec((1, d, MOE_TN), lambda j, i, te, tf, nu: (te[i], 0, j))
    return pl.pallas_call(
        _moe_up_kernel,
        out_shape=jax.ShapeDtypeStruct((r, f), BF16),
        grid_spec=pltpu.PrefetchScalarGridSpec(
            num_scalar_prefetch=3,
            grid=(f // MOE_TN, nt),
            in_specs=[pl.BlockSpec((MOE_TM, d), lambda j, i, te, tf, nu: (i, 0)), wspec, wspec],
            out_specs=pl.BlockSpec((MOE_TM, MOE_TN), lambda j, i, te, tf, nu: (i, j)),
            scratch_shapes=[pltpu.VMEM((d, MOE_TN), BF16), pltpu.VMEM((d, MOE_TN), BF16)]),
        compiler_params=_params("arbitrary", "arbitrary"),
        name="moe_up",
    )(plan["tile_expert"], plan["tile_first"], plan["n_used"], hs, w_gate, w_up)


def _moe_down_kernel(te_ref, tf_ref, nu_ref, t_ref, wd_ref, o_ref, wdb):
    i = pl.program_id(1)

    @pl.when(tf_ref[i] == 1)
    def _():
        wdb[...] = wd_ref[0].astype(BF16)

    @pl.when(i < nu_ref[0])
    def _():
        o_ref[...] = _dot(t_ref[...], wdb[...])

    @pl.when(i >= nu_ref[0])
    def _():
        o_ref[...] = jnp.zeros_like(o_ref)


def moe_down(t, w_down, plan):
    r, f = t.shape
    d = w_down.shape[2]
    nt = plan["nt"]
    return pl.pallas_call(
        _moe_down_kernel,
        out_shape=jax.ShapeDtypeStruct((r, d), F32),
        grid_spec=pltpu.PrefetchScalarGridSpec(
            num_scalar_prefetch=3,
            grid=(d // MOE_TN, nt),
            in_specs=[pl.BlockSpec((MOE_TM, f), lambda j, i, te, tf, nu: (i, 0)),
                      pl.BlockSpec((1, f, MOE_TN), lambda j, i, te, tf, nu: (te[i], 0, j))],
            out_specs=pl.BlockSpec((MOE_TM, MOE_TN), lambda j, i, te, tf, nu: (i, j)),
            scratch_shapes=[pltpu.VMEM((f, MOE_TN), BF16)]),
        compiler_params=_params("arbitrary", "arbitrary"),
        name="moe_down",
    )(plan["tile_expert"], plan["tile_first"], plan["n_used"], t, w_down)


def _combine_kernel(cur_ref, nxt_ref, y_hbm, x_ref, route_ref, g_ref, o_ref, buf, sem, *, final):
    tm = x_ref.shape[0]
    slot = _pipelined_gather(cur_ref, nxt_ref, y_hbm, buf, sem, 2 * tm)
    route = route_ref[...]
    g1 = route[:, ROUTE_G1:ROUTE_G1 + 1]
    g2 = route[:, ROUTE_G2:ROUTE_G2 + 1]
    x = x_ref[...] + g1 * buf[slot, :tm, :] + g2 * buf[slot, tm:, :]
    if final:
        ms = jnp.mean(x * x, axis=-1, keepdims=True)
        x = x * lax.rsqrt(ms + RMS_EPS) * g_ref[...]
    o_ref[...] = x


def moe_combine(x, y, dest, route, final_gain):
    L, d = x.shape
    tm = COMBINE_TM
    nt = L // tm
    idx = jnp.concatenate([dest[:L].reshape(nt, 1, tm), dest[L:].reshape(nt, 1, tm)], axis=2)
    smem = lambda f: pl.BlockSpec((1, 1, 2 * tm), f, memory_space=pltpu.SMEM)
    final = final_gain is not None
    gain = (final_gain if final else jnp.ones((d,), F32)).reshape(1, d)
    kern = functools.partial(_combine_kernel, final=final)
    return pl.pallas_call(
        kern,
        out_shape=jax.ShapeDtypeStruct((L, d), F32),
        grid=(nt,),
        in_specs=[smem(lambda i: (i, 0, 0)),
                  smem(lambda i: (jnp.minimum(i + 1, nt - 1), 0, 0)),
                  pl.BlockSpec(memory_space=pl.ANY),
                  pl.BlockSpec((tm, d), lambda i: (i, 0)),
                  pl.BlockSpec((tm, LANES), lambda i: (i, 0)),
                  pl.BlockSpec((1, d), lambda i: (0, 0))],
        out_specs=pl.BlockSpec((tm, d), lambda i: (i, 0)),
        scratch_shapes=[pltpu.VMEM((2, 2 * tm, d), F32), pltpu.SemaphoreType.DMA((2,))],
        compiler_params=_params("arbitrary"),
        name="moe_combine",
    )(idx, idx, y, x, route, gain)


def moe_ffn(x, norm_g, router_w, w_gate, w_up, w_down, final_gain):
    L = x.shape[0]
    h, route = rmsnorm_router(x, norm_g, router_w)
    plan = _moe_plan(route, L)
    hs = moe_dispatch(h, plan["row_token"], plan["nt"])
    t = moe_up(hs, w_gate, w_up, plan)
    y = moe_down(t, w_down, plan)
    return moe_combine(x, y, plan["dest"], route, final_gain)


def _layer_mixers(x, l, tables, norm_mix, w_in, conv_w, conv_b, filt, hyena_skip, w_branch_a,
                  gla_gate_w2, gla_gate_b, gla_norm, w_branch_b, w_out):
    hc = (HYENA_ORDER + 1) * HYENA_WIDTH
    o_lr = hc + 2 * GLA_DK_TOTAL + 2 * GLA_DV_TOTAL
    o_g = o_lr + 2 * GLA_LOWRANK
    wl = w_in[l].astype(BF16)
    h = rmsnorm(x, norm_mix[l], BF16)
    proj = functools.partial(matmul, [h], dots=[(0, 0)], extras=[], tm=1024, tn=512)
    u_hy = proj([wl[:, :hc]], epilogue=_ep_plain, out_dtype=F32, name="in_proj_hyena")
    qkvr = proj([wl[:, hc:o_lr]], epilogue=_ep_plain, out_dtype=F32, name="in_proj_gla")
    gate_a = proj([wl[:, o_g:o_g + D_MODEL]], epilogue=_ep_sigmoid, out_dtype=BF16,
                  name="in_proj_gate_a")
    gate_b = proj([wl[:, o_g + D_MODEL:]], epilogue=_ep_sigmoid, out_dtype=BF16,
                  name="in_proj_gate_b")
    la = gla_gates(h, wl[:, o_lr:o_g], gla_gate_w2[l], gla_gate_b[l])

    z_a = hyena_mixer(u_hy, conv_w[l], conv_b[l], filt, hyena_skip[l], tables)
    o_f, o_b = gla_scan(qkvr, la)
    z_b = gla_post(o_f, o_b, qkvr, gla_norm[l])

    mixed = matmul([z_a, z_b], [w_branch_a[l].astype(BF16), w_branch_b[l].astype(BF16)],
                   [(0, 0), (1, 1)], _ep_merge, [(gate_a, "mn"), (gate_b, "mn")], BF16,
                   tm=1024, tn=512, name="branch_merge")
    return matmul([mixed], [w_out[l].astype(BF16)], [(0, 0)], _ep_residual, [(x, "mn")], F32,
                  tm=1024, tn=512, name="out_proj")


def _swiglu_ffn(x, h, wg, wu, wd):
    t = matmul([h], [wg, wu], [(0, 0), (0, 1)], _ep_swiglu, [], BF16,
               tm=1024, tn=512, name="ffn_up")
    return matmul([t], [wd], [(0, 0)], _ep_residual, [(x, "mn")], F32,
                  tm=512, tn=512, name="ffn_down")


def kernel(x, norm_mix, w_in, conv_w, conv_b, filt_w1, filt_b1, filt_freq1, filt_w2, filt_b2, filt_freq2, filt_w3, filt_b3, filt_freq3, filt_w4, hyena_skip, w_branch_a, gla_gate_w2, gla_gate_b, gla_norm, w_branch_b, w_out, norm_ffn, dense_w_gate, dense_w_up, dense_w_down, router_w, moe_w_gate, moe_w_up, moe_w_down, norm_final):
    b, L, d = x.shape
    assert b == 1
    x = x.reshape(L, d)
    tables = _fft_tables(L)
    for l in range(DEPTH):
        filt = (filt_w1[l], filt_b1[l], filt_freq1[l], filt_w2[l], filt_b2[l], filt_freq2[l],
                filt_w3[l], filt_b3[l], filt_freq3[l], filt_w4[l])
        x = _layer_mixers(x, l, tables, norm_mix, w_in, conv_w, conv_b, filt, hyena_skip,
                          w_branch_a, gla_gate_w2, gla_gate_b, gla_norm, w_branch_b, w_out)
        i = l // 2
        last = l == DEPTH - 1
        if l % 2 == 0:
            h = rmsnorm(x, norm_ffn[l], BF16)
            x = _swiglu_ffn(x, h, dense_w_gate[i].astype(BF16), dense_w_up[i].astype(BF16),
                            dense_w_down[i].astype(BF16))
            if last:
                x = rmsnorm(x, norm_final, F32)
        else:
            x = moe_ffn(x, norm_ffn[l], router_w[i], moe_w_gate[i], moe_w_up[i], moe_w_down[i],
                        norm_final if last else None)
    return x.reshape(b, L, d)
```

```python
import functools
import math

import jax
import jax.numpy as jnp
from jax import lax
from jax.experimental import pallas as pl
from jax.experimental.pallas import tpu as pltpu

F32 = jnp.float32
BF16 = jnp.bfloat16

D_MODEL = 2048
DEPTH = 2
HYENA_WIDTH = D_MODEL // 2
HYENA_ORDER = 2
SHORT_CONV = 3
POS_EMB_DIM = 33
POS_BANDS = (POS_EMB_DIM - 1) // 2
FILTER_HIDDEN = 64
MIN_DECAY = math.log(1e-2) / 1.5
MAX_DECAY = math.log(1e-2) / 0.3
GLA_HEADS = 4
GLA_DK = 128
GLA_DV = 256
GLA_DK_TOTAL = GLA_HEADS * GLA_DK
GLA_DV_TOTAL = GLA_HEADS * GLA_DV
GLA_LOWRANK = 16
GLA_GATE_TAU = 16.0
GLA_CHUNK = 64
N_EXPERTS = 8
RMS_EPS = 1e-6

LANES = 128
VMEM_LIMIT_BYTES = 56 * 1024 * 1024


def _params(*sem):
    return pltpu.CompilerParams(dimension_semantics=sem, vmem_limit_bytes=VMEM_LIMIT_BYTES)


def _dot(a, b):
    return jnp.dot(a, b, preferred_element_type=F32)


def _split(a):
    hi = a.astype(BF16)
    lo = (a - hi.astype(F32)).astype(BF16)
    return hi, lo


def _dot3(a, b):
    ah, al = _split(a)
    bh, bl = _split(b)
    return _dot(ah, bh) + (_dot(ah, bl) + _dot(al, bh))


def _sigmoid(x):
    return 1.0 / (1.0 + jnp.exp(-x))


def _silu(x):
    return x * _sigmoid(x)


def _log_sigmoid(x):
    return jnp.minimum(x, 0.0) - jnp.log(1.0 + jnp.exp(-jnp.abs(x)))


def _rmsnorm_kernel(x_ref, g_ref, o_ref):
    x = x_ref[...]
    ms = jnp.mean(x * x, axis=-1, keepdims=True)
    o_ref[...] = (x * lax.rsqrt(ms + RMS_EPS) * g_ref[...]).astype(o_ref.dtype)


def rmsnorm(x, g, out_dtype, tm=256):
    m, d = x.shape
    return pl.pallas_call(
        _rmsnorm_kernel,
        out_shape=jax.ShapeDtypeStruct((m, d), out_dtype),
        grid=(m // tm,),
        in_specs=[pl.BlockSpec((tm, d), lambda i: (i, 0)),
                  pl.BlockSpec((1, d), lambda i: (0, 0))],
        out_specs=pl.BlockSpec((tm, d), lambda i: (i, 0)),
        compiler_params=_params("parallel"),
        name="rmsnorm",
    )(x, g.reshape(1, d))


def _rmsnorm_router_kernel(x_ref, g_ref, rw_ref, o_ref, cw_ref):
    x = x_ref[...]
    ms = jnp.mean(x * x, axis=-1, keepdims=True)
    h = x * lax.rsqrt(ms + RMS_EPS) * g_ref[...]
    o_ref[...] = h.astype(o_ref.dtype)
    logits = _dot3(h, rw_ref[...])
    col = lax.broadcasted_iota(jnp.int32, logits.shape, 1).astype(F32)
    neg = jnp.float32(-jnp.inf)
    logits = jnp.where(col < N_EXPERTS, logits, neg)
    m1 = jnp.max(logits, axis=-1, keepdims=True)
    i1 = jnp.min(jnp.where(logits == m1, col, float(LANES)), axis=-1, keepdims=True)
    rest = jnp.where(col == i1, neg, logits)
    m2 = jnp.max(rest, axis=-1, keepdims=True)
    i2 = jnp.min(jnp.where(rest == m2, col, float(LANES)), axis=-1, keepdims=True)
    e2 = jnp.exp(m2 - m1)
    g1 = 1.0 / (1.0 + e2)
    g2 = e2 / (1.0 + e2)
    cw_ref[...] = (jnp.where(col == ROUTE_E1, i1, 0.0) + jnp.where(col == ROUTE_E2, i2, 0.0)
                   + jnp.where(col == ROUTE_G1, g1, 0.0) + jnp.where(col == ROUTE_G2, g2, 0.0))


ROUTE_E1, ROUTE_E2, ROUTE_G1, ROUTE_G2 = 0, 1, 2, 3


def rmsnorm_router(x, g, router_w, tm=256):
    m, d = x.shape
    rw = jnp.zeros((d, LANES), F32).at[:, :N_EXPERTS].set(router_w)
    return pl.pallas_call(
        _rmsnorm_router_kernel,
        out_shape=(jax.ShapeDtypeStruct((m, d), F32), jax.ShapeDtypeStruct((m, LANES), F32)),
        grid=(m // tm,),
        in_specs=[pl.BlockSpec((tm, d), lambda i: (i, 0)),
                  pl.BlockSpec((1, d), lambda i: (0, 0)),
                  pl.BlockSpec((d, LANES), lambda i: (0, 0))],
        out_specs=(pl.BlockSpec((tm, d), lambda i: (i, 0)),
                   pl.BlockSpec((tm, LANES), lambda i: (i, 0))),
        compiler_params=_params("parallel"),
        name="rmsnorm_router",
    )(x, g.reshape(1, d), rw)


def _mm_kernel(*refs, n_a, n_w, dots, n_extra, epilogue):
    a_refs = refs[:n_a]
    w_refs = refs[n_a:n_a + n_w]
    e_refs = refs[n_a + n_w:n_a + n_w + n_extra]
    o_ref = refs[n_a + n_w + n_extra]
    a_vals = [a[...].astype(BF16) for a in a_refs]
    accs = [_dot(a_vals[ai], w_refs[wi][...]) for ai, wi in dots]
    o_ref[...] = epilogue(accs, [e[...] for e in e_refs]).astype(o_ref.dtype)


def matmul(a_list, w_list, dots, epilogue, extras, out_dtype, *, tm, tn, name):
    m = a_list[0].shape[0]
    n = w_list[0].shape[1]
    assert m % tm == 0 and n % tn == 0
    in_specs = []
    for a in a_list:
        in_specs.append(pl.BlockSpec((tm, a.shape[1]), lambda i, j: (i, 0)))
    for w in w_list:
        in_specs.append(pl.BlockSpec((w.shape[0], tn), lambda i, j: (0, j)))
    extra_arrays = []
    for arr, kind in extras:
        extra_arrays.append(arr)
        if kind == "mn":
            in_specs.append(pl.BlockSpec((tm, tn), lambda i, j: (i, j)))
        else:
            in_specs.append(pl.BlockSpec((tm, arr.shape[1]), lambda i, j: (i, 0)))
    kern = functools.partial(_mm_kernel, n_a=len(a_list), n_w=len(w_list), dots=tuple(dots),
                             n_extra=len(extras), epilogue=epilogue)
    return pl.pallas_call(
        kern,
        out_shape=jax.ShapeDtypeStruct((m, n), out_dtype),
        grid=(m // tm, n // tn),
        in_specs=in_specs,
        out_specs=pl.BlockSpec((tm, tn), lambda i, j: (i, j)),
        compiler_params=_params("parallel", "parallel"),
        name=name,
    )(*a_list, *w_list, *extra_arrays)


def _ep_plain(accs, ex):
    return accs[0]


def _ep_sigmoid(accs, ex):
    return _sigmoid(accs[0])


def _ep_residual(accs, ex):
    return ex[0] + accs[0]


def _ep_merge(accs, ex):
    return ex[0].astype(F32) * accs[0] + ex[1].astype(F32) * accs[1]


def _ep_swiglu(accs, ex):
    return _silu(accs[0]) * accs[1]


def _gate_kernel(h_ref, wlr_ref, w2_ref, b_ref, o_ref):
    lr = _dot(h_ref[...], wlr_ref[...])
    z = _dot3(lr, w2_ref[...]) + b_ref[...]
    o_ref[...] = _log_sigmoid(z) * (1.0 / GLA_GATE_TAU)


def gla_gates(h, w_lr, gate_w2, gate_b, tm=512):
    m, d = h.shape
    r = GLA_LOWRANK
    n = 2 * GLA_DK_TOTAL
    wlr = jnp.zeros((d, LANES), BF16).at[:, :2 * r].set(w_lr)
    w2 = jnp.zeros((LANES, n), F32)
    w2 = w2.at[:r, :GLA_DK_TOTAL].set(gate_w2[0]).at[r:2 * r, GLA_DK_TOTAL:].set(gate_w2[1])
    b = gate_b.reshape(1, n)
    return pl.pallas_call(
        _gate_kernel,
        out_shape=jax.ShapeDtypeStruct((m, n), F32),
        grid=(m // tm,),
        in_specs=[pl.BlockSpec((tm, d), lambda i: (i, 0)),
                  pl.BlockSpec((d, LANES), lambda i: (0, 0)),
                  pl.BlockSpec((LANES, n), lambda i: (0, 0)),
                  pl.BlockSpec((1, n), lambda i: (0, 0))],
        out_specs=pl.BlockSpec((tm, n), lambda i: (i, 0)),
        compiler_params=_params("parallel"),
        name="gla_gates",
    )(h, wlr, w2, b)


GLA_SUB = 4


def _gla_chunk(q, k, v, la, st_ref, h, reverse):
    c = GLA_CHUNK
    row = lax.broadcasted_iota(jnp.int32, (c, c), 0)
    col = lax.broadcasted_iota(jnp.int32, (c, c), 1)
    keep = (col > row) if reverse else (col <= row)
    b = la
    rows = lax.broadcasted_iota(jnp.int32, la.shape, 0)
    step = 1
    while step < c:
        if reverse:
            b = b + jnp.where(rows < c - step, pltpu.roll(b, c - step, 0), 0.0)
        else:
            b = b + jnp.where(rows >= step, pltpu.roll(b, step, 0), 0.0)
        step *= 2
    tot = b[0:1] if reverse else b[c - 1:c]
    q_in = (q * ((GLA_DK ** -0.5) * jnp.exp(b))).astype(BF16)
    k_dec = k * jnp.exp(-b)
    k_in = k_dec.astype(BF16)
    k_out = (k_dec * jnp.exp(tot)).astype(BF16)
    vb = v.astype(BF16)
    s = lax.dot_general(q_in, k_in, (((1,), (1,)), ((), ())), preferred_element_type=F32)
    s = jnp.where(keep, s, 0.0).astype(BF16)
    st = st_ref[h]
    o = _dot(s, vb) + lax.dot_general(q_in, st.astype(BF16), (((1,), (1,)), ((), ())),
                                      preferred_element_type=F32)
    upd = lax.dot_general(vb, k_out, (((0,), (0,)), ((), ())), preferred_element_type=F32)
    st_ref[h] = st * jnp.exp(tot) + upd
    return o


def _gla_kernel(qf_ref, kf_ref, vf_ref, lf_ref, qb_ref, kb_ref, vb_ref, lb_ref,
                of_ref, ob_ref, sf_ref, sb_ref):
    @pl.when(pl.program_id(0) == 0)
    def _():
        sf_ref[...] = jnp.zeros_like(sf_ref)
        sb_ref[...] = jnp.zeros_like(sb_ref)

    def body(s, carry):
        rf = pl.multiple_of(s * GLA_CHUNK, GLA_CHUNK)
        rb = pl.multiple_of((GLA_SUB - 1 - s) * GLA_CHUNK, GLA_CHUNK)
        for h in range(GLA_HEADS):
            kk = slice(h * GLA_DK, (h + 1) * GLA_DK)
            vv = slice(h * GLA_DV, (h + 1) * GLA_DV)
            of_ref[pl.ds(rf, GLA_CHUNK), vv] = _gla_chunk(
                qf_ref[pl.ds(rf, GLA_CHUNK), kk], kf_ref[pl.ds(rf, GLA_CHUNK), kk],
                vf_ref[pl.ds(rf, GLA_CHUNK), vv], lf_ref[pl.ds(rf, GLA_CHUNK), kk],
                sf_ref, h, False)
            ob_ref[pl.ds(rb, GLA_CHUNK), vv] = _gla_chunk(
                qb_ref[pl.ds(rb, GLA_CHUNK), kk], kb_ref[pl.ds(rb, GLA_CHUNK), kk],
                vb_ref[pl.ds(rb, GLA_CHUNK), vv], lb_ref[pl.ds(rb, GLA_CHUNK), kk],
                sb_ref, h, True)
        return carry

    lax.fori_loop(0, GLA_SUB, body, 0, unroll=True)


def gla_scan(qkvr, la):
    L = qkvr.shape[0]
    rows = GLA_CHUNK * GLA_SUB
    n = L // rows
    dk, dv = GLA_DK_TOTAL, GLA_DV_TOTAL
    specs = []
    for rev in (False, True):
        blk = (lambda i: n - 1 - i) if rev else (lambda i: i)
        specs += [pl.BlockSpec((rows, dk), lambda i, blk=blk: (blk(i), 0)),
                  pl.BlockSpec((rows, dk), lambda i, blk=blk: (blk(i), 1)),
                  pl.BlockSpec((rows, dv), lambda i, blk=blk: (blk(i), 1)),
                  pl.BlockSpec((rows, dk), lambda i, blk=blk, c=int(rev): (blk(i), c))]
    return pl.pallas_call(
        _gla_kernel,
        out_shape=(jax.ShapeDtypeStruct((L, dv), F32), jax.ShapeDtypeStruct((L, dv), F32)),
        grid=(n,),
        in_specs=specs,
        out_specs=(pl.BlockSpec((rows, dv), lambda i: (i, 0)),
                   pl.BlockSpec((rows, dv), lambda i: (n - 1 - i, 0))),
        scratch_shapes=[pltpu.VMEM((GLA_HEADS, GLA_DV, GLA_DK), F32),
                        pltpu.VMEM((GLA_HEADS, GLA_DV, GLA_DK), F32)],
        compiler_params=_params("arbitrary"),
        name="gla_scan",
    )(qkvr, qkvr, qkvr, la, qkvr, qkvr, qkvr, la)


def _gla_post_kernel(of_ref, ob_ref, r_ref, g_ref, o_ref):
    g = g_ref[...]
    for h in range(GLA_HEADS):
        vv = slice(h * GLA_DV, (h + 1) * GLA_DV)
        o = of_ref[:, vv] + ob_ref[:, vv]
        ms = jnp.mean(o * o, axis=-1, keepdims=True)
        y = o * lax.rsqrt(ms + RMS_EPS) * g
        o_ref[:, vv] = (y * _silu(r_ref[:, vv])).astype(o_ref.dtype)


def gla_post(o_f, o_b, qkvr, norm_g, tm=512):
    L, dv = o_f.shape
    return pl.pallas_call(
        _gla_post_kernel,
        out_shape=jax.ShapeDtypeStruct((L, dv), BF16),
        grid=(L // tm,),
        in_specs=[pl.BlockSpec((tm, dv), lambda i: (i, 0)),
                  pl.BlockSpec((tm, dv), lambda i: (i, 0)),
                  pl.BlockSpec((tm, dv), lambda i: (i, 2)),
                  pl.BlockSpec((1, GLA_DV), lambda i: (0, 0))],
        out_specs=pl.BlockSpec((tm, dv), lambda i: (i, 0)),
        compiler_params=_params("parallel"),
        name="gla_post",
    )(o_f, o_b, qkvr, norm_g.reshape(1, GLA_DV))


CT = 256
CT_COL = 128
MID_SLABS = 8
SUB = 8


def _fft_dims(L):
    n = 2 * L
    na = 1 << ((n.bit_length() - 1) // 2)
    nb = n // na
    assert na * nb == n and nb % SUB == 0 and (na // 2) % 8 == 0
    kp = -(-(na // 2 + 1) // 8) * 8
    return n, na, nb, kp


def _cis(num, den):
    ang = (2.0 * math.pi / den) * (num % den).astype(F32)
    return jnp.cos(ang), jnp.sin(ang)


def _expand_kernel(t_ref, o_ref):
    t = t_ref[0].astype(BF16)
    cc, cols = t.shape[1], o_ref.shape[2]
    shift, low = SUB.bit_length() - 1, SUB - 1
    src = lax.broadcasted_iota(jnp.int32, (cc, cols), 0)
    dst = lax.broadcasted_iota(jnp.int32, (cc, cols), 1)
    spread = jnp.where(lax.shift_right_logical(dst, shift) == src, 1.0, 0.0).astype(BF16)
    x = _dot(t, spread)
    row = lax.broadcasted_iota(jnp.int32, x.shape, 0)
    col = lax.broadcasted_iota(jnp.int32, x.shape, 1)
    o_ref[0] = jnp.where((col & low) == (row & low), x, 0.0).astype(o_ref.dtype)


def _expand_block_diag(t):
    g, r, c = t.shape
    return pl.pallas_call(
        _expand_kernel,
        out_shape=jax.ShapeDtypeStruct((g, r, c * SUB), BF16),
        grid=(g,),
        in_specs=[pl.BlockSpec((1, r, c), lambda i: (i, 0, 0))],
        out_specs=pl.BlockSpec((1, r, c * SUB), lambda i: (i, 0, 0)),
        compiler_params=_params("parallel"),
        name="dft_table_expand",
    )(t)


def _fft_tables(L):
    n, na, nb, kp = _fft_dims(L)
    ha, ng = na // 2, nb // SUB
    ar = lambda m: jnp.arange(m, dtype=jnp.int32)
    g_, ka_, bl_, a_ = ar(ng)[:, None, None, None], ar(kp)[None, :, None, None], \
        ar(SUB)[None, None, :, None], ar(ha)[None, None, None, :]
    c, s = _cis(ka_ * (nb * a_ + SUB * g_ + bl_), n)
    cs = jnp.stack([c, -s], axis=2)
    m8 = _expand_block_diag(cs.reshape(ng, kp * 2 * SUB, ha))
    c, s = _cis(ar(nb)[:, None] * ar(nb)[None, :], nb)
    f2 = jnp.stack([jnp.stack([c, s], axis=1), jnp.stack([-s, c], axis=1)], axis=0)
    f2 = f2.reshape(2, nb, 2, ng, SUB).transpose(0, 1, 3, 2, 4).reshape(2 * nb, 2 * nb)
    ca, sa = _cis(ar(kp)[:, None] * ar(nb)[None, :], n)
    cb, sb = _cis(ar(nb)[:, None] * ar(nb)[None, :], nb)
    ca, sa = (t.reshape(kp, ng, 1, SUB, 1, 1) / n for t in (ca, sa))
    cb, sb = (t.reshape(1, ng, 1, SUB, 1, nb) for t in (cb, sb))
    c, s = ca * cb - sa * sb, sa * cb + ca * sb
    gi = jnp.concatenate([jnp.concatenate([c, -s], axis=4),
                          jnp.concatenate([s, c], axis=4)], axis=2)
    gi = gi.reshape(kp, 2 * nb, 2 * nb)
    kc = ar(kp)[None, :]
    wgt = jnp.where((kc == 0) | (kc == ha), 1.0, jnp.where(kc < ha, 2.0, 0.0))
    c, s = _cis(ar(ha)[:, None] * kc, na)
    pm = jnp.stack([wgt * c, -wgt * s], axis=2).reshape(ha, 1, kp * 2)
    p8 = _expand_block_diag(jnp.broadcast_to(pm, (ha, SUB, kp * 2)).reshape(1, ha * SUB, kp * 2))[0]
    return m8, f2.astype(BF16), gi.astype(BF16), p8


def _short_conv_kernel(u_ref, w_ref, b_ref, o_ref):
    u = u_ref[...]
    L = u.shape[0]
    row = lax.broadcasted_iota(jnp.int32, u.shape, 0)
    prev = jnp.where(row == 0, 0.0, pltpu.roll(u, 1, 0))
    nxt = jnp.where(row == L - 1, 0.0, pltpu.roll(u, L - 1, 0))
    w = w_ref[...]
    o_ref[...] = prev * w[0:1] + u * w[1:2] + nxt * w[2:3] + b_ref[...]


def short_conv(u, w, b):
    L, c = u.shape
    w8 = jnp.zeros((8, c), F32).at[:SHORT_CONV].set(w)
    return pl.pallas_call(
        _short_conv_kernel,
        out_shape=jax.ShapeDtypeStruct((L, c), F32),
        grid=(c // CT_COL,),
        in_specs=[pl.BlockSpec((L, CT_COL), lambda j: (0, j)),
                  pl.BlockSpec((8, CT_COL), lambda j: (0, j)),
                  pl.BlockSpec((1, CT_COL), lambda j: (0, j))],
        out_specs=pl.BlockSpec((L, CT_COL), lambda j: (0, j)),
        compiler_params=_params("parallel"),
        name="short_conv",
    )(u, w8, b.reshape(1, c))


def _fft_s1_kernel(u_ref, m8_ref, a_ref):
    ha, sub, ct = u_ref.shape
    x = u_ref[...].reshape(ha * sub, ct).astype(BF16)
    r = _dot(m8_ref[0], x)
    a_ref[:, 0, :, :] = r.astype(a_ref.dtype).reshape(a_ref.shape[0], 2 * sub, ct)


def fft_stage1(u, ncols, m8, L):
    n, na, nb, kp = _fft_dims(L)
    ha, ng = na // 2, nb // SUB
    u3 = u.reshape(ha, nb, u.shape[1])
    return pl.pallas_call(
        _fft_s1_kernel,
        out_shape=jax.ShapeDtypeStruct((kp, ng, 2 * SUB, ncols), BF16),
        grid=(ng, ncols // CT),
        in_specs=[pl.BlockSpec((ha, SUB, CT), lambda g, j: (0, g, j)),
                  pl.BlockSpec((1, kp * 2 * SUB, ha * SUB), lambda g, j: (g, 0, 0))],
        out_specs=pl.BlockSpec((kp, 1, 2 * SUB, CT), lambda g, j: (0, g, 0, j)),
        compiler_params=_params("parallel", "parallel"),
        name="fft_stage1",
    )(u3, m8)


def _filter_s1_kernel(z_ref, w1_ref, b1_ref, f1_ref, w2_ref, b2_ref, f2_ref, w3_ref, b3_ref,
                      f3_ref, w4_ref, dl_ref, m8_ref, a_ref, nrm_ref, h_ref, *, nb, L, cb):
    g, j = pl.program_id(0), pl.program_id(1)
    ha, sub, _ = z_ref.shape
    rows = ha * sub

    @pl.when(j == 0)
    def _():
        z = z_ref[...].reshape(rows, z_ref.shape[2])
        h = jnp.sin(f1_ref[...] * (_dot3(z, w1_ref[...]) + b1_ref[...]))
        h = jnp.sin(f2_ref[...] * (_dot3(h, w2_ref[...]) + b2_ref[...]))
        h_ref[...] = jnp.sin(f3_ref[...] * (_dot3(h, w3_ref[...]) + b3_ref[...]))

    h = h_ref[...]
    r = lax.broadcasted_iota(jnp.int32, (rows, CT), 0)
    tpos = (r // sub) * nb + g * sub + r % sub
    decay = jnp.exp(-(tpos.astype(F32) * (1.0 / (L - 1))) * dl_ref[...])
    k = _dot3(h, w4_ref[...]) * decay
    backward = (j // cb) % 2 == 1
    k = jnp.where(jnp.logical_and(backward, tpos == 0), 0.0, k)
    nrm_ref[0] = jnp.sum(jnp.abs(k), axis=0, keepdims=True)
    q = _dot(m8_ref[0], k.astype(BF16))
    a_ref[:, 0, :, :] = q.astype(a_ref.dtype).reshape(a_ref.shape[0], 2 * sub, CT)


def filter_stage1(L, w1, b1, f1, w2, b2, f2, w3, b3, f3, w4, m8):
    n, na, nb, kp = _fft_dims(L)
    ha, ng = na // 2, nb // SUB
    c = HYENA_WIDTH
    hdim = FILTER_HIDDEN
    t = jnp.linspace(0.0, 1.0, L, dtype=F32)[:, None]
    omega = 2.0 * math.pi * jnp.arange(L, dtype=F32)[:, None] / L
    bands = jnp.linspace(1e-4, POS_BANDS - 1, POS_BANDS, dtype=F32)[None, :]
    z = jnp.concatenate([t, jnp.cos(bands * omega), -jnp.sin(bands * omega)], axis=-1)
    z = jnp.zeros((L, LANES), F32).at[:, :POS_EMB_DIM].set(z).reshape(ha, nb, LANES)
    w1p = jnp.zeros((LANES, hdim), F32).at[:POS_EMB_DIM].set(w1)
    deltas = jnp.abs(jnp.linspace(MIN_DECAY, MAX_DECAY, c, dtype=F32)).reshape(1, c)
    nc = w4.shape[1]
    cb = c // CT
    row = lambda v: v.reshape(1, hdim)
    full = lambda shape: pl.BlockSpec(shape, lambda g, j: (0,) * len(shape))
    kern = functools.partial(_filter_s1_kernel, nb=nb, L=L, cb=cb)
    return pl.pallas_call(
        kern,
        out_shape=(jax.ShapeDtypeStruct((kp, ng, 2 * SUB, nc), BF16),
                   jax.ShapeDtypeStruct((ng, 1, nc), F32)),
        grid=(ng, nc // CT),
        in_specs=[pl.BlockSpec((ha, SUB, LANES), lambda g, j: (0, g, 0)),
                  full((LANES, hdim)), full((1, hdim)), full((1, hdim)),
                  full((hdim, hdim)), full((1, hdim)), full((1, hdim)),
                  full((hdim, hdim)), full((1, hdim)), full((1, hdim)),
                  pl.BlockSpec((hdim, CT), lambda g, j: (0, j)),
                  pl.BlockSpec((1, CT), lambda g, j: (0, j % cb)),
                  pl.BlockSpec((1, kp * 2 * SUB, ha * SUB), lambda g, j: (g, 0, 0))],
        out_specs=(pl.BlockSpec((kp, 1, 2 * SUB, CT), lambda g, j: (0, g, 0, j)),
                   pl.BlockSpec((1, 1, CT), lambda g, j: (g, 0, j))),
        scratch_shapes=[pltpu.VMEM((ha * SUB, hdim), F32)],
        compiler_params=_params("arbitrary", "arbitrary"),
        name="hyena_filter_stage1",
    )(z, w1p, row(b1), row(f1), w2, row(b2), row(f2), w3, row(b3), row(f3), w4, deltas, m8)


def _slab(ref, s):
    _, ng, rows, ct = ref.shape
    return ref[s].reshape(ng * rows, ct)


def _filter_spec_kernel(af_ref, ab_ref, f2_ref, nf_ref, nbk_ref, h_ref):
    f2 = f2_ref[...]
    nb = f2.shape[0] // 2
    inv = 1.0 / (jnp.sum(nf_ref[...], axis=0) + jnp.sum(nbk_ref[...], axis=0))
    for s in range(MID_SLABS):
        xf = _dot(f2, _slab(af_ref, s))
        xb = _dot(f2, _slab(ab_ref, s))
        h_ref[0, s] = ((xf[:nb] + xb[:nb]) * inv).astype(h_ref.dtype)
        h_ref[1, s] = ((xf[nb:] - xb[nb:]) * inv).astype(h_ref.dtype)


def filter_spectrum(a, nrm, f2, L):
    n, na, nb, kp = _fft_dims(L)
    ng = nb // SUB
    c = HYENA_WIDTH
    cb = c // CT
    fcol = lambda j: (j // cb) * 2 * cb + j % cb
    return pl.pallas_call(
        _filter_spec_kernel,
        out_shape=jax.ShapeDtypeStruct((2, kp, nb, HYENA_ORDER * c), BF16),
        grid=(HYENA_ORDER * cb, kp // MID_SLABS),
        in_specs=[pl.BlockSpec((MID_SLABS, ng, 2 * SUB, CT), lambda j, k: (k, 0, 0, fcol(j))),
                  pl.BlockSpec((MID_SLABS, ng, 2 * SUB, CT), lambda j, k: (k, 0, 0, fcol(j) + cb)),
                  pl.BlockSpec((2 * nb, 2 * nb), lambda j, k: (0, 0)),
                  pl.BlockSpec((ng, 1, CT), lambda j, k: (0, 0, fcol(j))),
                  pl.BlockSpec((ng, 1, CT), lambda j, k: (0, 0, fcol(j) + cb))],
        out_specs=pl.BlockSpec((2, MID_SLABS, nb, CT), lambda j, k: (0, k, 0, j)),
        compiler_params=_params("parallel", "parallel"),
        name="hyena_filter_spectrum",
    )(a, a, f2, nrm, nrm)


def _fft_mid_kernel(a_ref, h_ref, f2_ref, g_ref, q_ref):
    f2 = f2_ref[...]
    nb = f2.shape[0] // 2
    _, ng, rows, ct = q_ref.shape
    for s in range(MID_SLABS):
        x = _dot(f2, _slab(a_ref, s))
        xr, xi = x[:nb], x[nb:]
        hr, hi = h_ref[0, s].astype(F32), h_ref[1, s].astype(F32)
        y = jnp.concatenate([xr * hr - xi * hi, xr * hi + xi * hr], axis=0).astype(BF16)
        q_ref[s] = _dot(g_ref[s], y).astype(q_ref.dtype).reshape(ng, rows, ct)


def fft_mid(a, hf, order, f2, g, L):
    n, na, nb, kp = _fft_dims(L)
    ng = nb // SUB
    c = a.shape[3]
    cb = c // CT
    return pl.pallas_call(
        _fft_mid_kernel,
        out_shape=jax.ShapeDtypeStruct(a.shape, BF16),
        grid=(kp // MID_SLABS, cb),
        in_specs=[pl.BlockSpec((MID_SLABS, ng, 2 * SUB, CT), lambda k, j: (k, 0, 0, j)),
                  pl.BlockSpec((2, MID_SLABS, nb, CT), lambda k, j: (0, k, 0, order * cb + j)),
                  pl.BlockSpec((2 * nb, 2 * nb), lambda k, j: (0, 0)),
                  pl.BlockSpec((MID_SLABS, 2 * nb, 2 * nb), lambda k, j: (k, 0, 0))],
        out_specs=pl.BlockSpec((MID_SLABS, ng, 2 * SUB, CT), lambda k, j: (k, 0, 0, j)),
        compiler_params=_params("parallel", "parallel"),
        name="fft_mid",
    )(a, hf, f2, g)


def _fft_i2_kernel(q_ref, p8_ref, u_ref, x_ref, s_ref, o_ref):
    kp, _, rows, ct = q_ref.shape
    q = q_ref[:, 0, :, :].reshape(kp * rows, ct)
    y = _dot(p8_ref[...], q).reshape(o_ref.shape)
    o_ref[...] = x_ref[...] * (y + u_ref[...] * s_ref[...])


def fft_stage_out(q, u, gate_src, gate_blk, skip, p8, L):
    n, na, nb, kp = _fft_dims(L)
    ha, ng = na // 2, nb // SUB
    c = q.shape[3]
    u3 = u.reshape(ha, nb, u.shape[1])
    g3 = gate_src.reshape(ha, nb, gate_src.shape[1])
    out = pl.pallas_call(
        _fft_i2_kernel,
        out_shape=jax.ShapeDtypeStruct((ha, nb, c), F32),
        grid=(ng, c // CT),
        in_specs=[pl.BlockSpec((kp, 1, 2 * SUB, CT), lambda g, j: (0, g, 0, j)),
                  pl.BlockSpec((ha * SUB, kp * 2 * SUB), lambda g, j: (0, 0)),
                  pl.BlockSpec((ha, SUB, CT), lambda g, j: (0, g, j)),
                  pl.BlockSpec((ha, SUB, CT), lambda g, j: (0, g, gate_blk + j)),
                  pl.BlockSpec((1, 1, CT), lambda g, j: (0, 0, j))],
        out_specs=pl.BlockSpec((ha, SUB, CT), lambda g, j: (0, g, j)),
        compiler_params=_params("parallel", "parallel"),
        name="fft_stage_out",
    )(q, p8, u3, g3, skip.reshape(1, 1, c))
    return out.reshape(L, c)


def hyena_mixer(u_hy, conv_w, conv_b, filt, skip, tables):
    L = u_hy.shape[0]
    c = HYENA_WIDTH
    m8, f2, g, p8 = tables
    uc = short_conv(u_hy, conv_w, conv_b)
    a_filt, nrm = filter_stage1(L, *filt, m8)
    hf = filter_spectrum(a_filt, nrm, f2, L)
    z = uc
    for order in range(HYENA_ORDER):
        a = fft_stage1(z, c, m8, L)
        q = fft_mid(a, hf, order, f2, g, L)
        z = fft_stage_out(q, z, uc, (order + 1) * (c // CT), skip[order], p8, L)
    return z


MOE_TM = 512
MOE_TN = 512
COMBINE_TM = 256


def _moe_plan(route, L):
    tm = MOE_TM
    nt = -(-(2 * L + N_EXPERTS * (tm - 1)) // tm)
    e_flat = jnp.concatenate([route[:, ROUTE_E1], route[:, ROUTE_E2]]).astype(jnp.int32)
    onehot = (e_flat[:, None] == jnp.arange(N_EXPERTS, dtype=jnp.int32)[None, :]).astype(jnp.int32)
    csum = jnp.cumsum(onehot, axis=0)
    rank = jnp.sum(onehot * (csum - 1), axis=1)
    counts = csum[-1]
    padded = ((counts + tm - 1) // tm) * tm
    ends = jnp.cumsum(padded)
    dest = (ends - padded)[e_flat] + rank
    tok = jnp.tile(jnp.arange(L, dtype=jnp.int32), 2)
    row_token = jnp.zeros((nt * tm,), jnp.int32).at[dest].set(tok, unique_indices=True)
    n_used = ends[-1] // tm
    tile_row = jnp.arange(nt, dtype=jnp.int32) * tm
    tile_expert = jnp.sum((tile_row[:, None] >= ends[None, :]).astype(jnp.int32), axis=1)
    last = jnp.take(tile_expert, jnp.maximum(n_used - 1, 0))
    tile_expert = jnp.where(jnp.arange(nt) < n_used, tile_expert, last)
    tile_first = jnp.concatenate([jnp.ones((1,), jnp.int32),
                                  (tile_expert[1:] != tile_expert[:-1]).astype(jnp.int32)])
    return dict(nt=nt, row_token=row_token, dest=dest, tile_expert=tile_expert,
                tile_first=tile_first, n_used=n_used.reshape(1).astype(jnp.int32))


def _row_copy(src_hbm, idx, buf, slot, r, sem):
    return pltpu.make_async_copy(src_hbm.at[pl.ds(idx, 1)], buf.at[slot, pl.ds(r, 1)],
                                 sem.at[slot])


def _gather_rows(idx_ref, src_hbm, buf, slot, sem, n_rows, start):
    if not start:
        pltpu.make_async_copy(src_hbm.at[pl.ds(0, n_rows)], buf.at[slot], sem.at[slot]).wait()
        return

    def body(r, carry):
        _row_copy(src_hbm, idx_ref[0, 0, r], buf, slot, r, sem).start()
        return carry
    lax.fori_loop(0, n_rows, body, 0, unroll=8)


def _pipelined_gather(cur_ref, nxt_ref, src_hbm, buf, sem, n_rows):
    i, n = pl.program_id(0), pl.num_programs(0)
    slot = lax.rem(i, 2)

    @pl.when(i == 0)
    def _():
        _gather_rows(cur_ref, src_hbm, buf, 0, sem, n_rows, True)

    @pl.when(i + 1 < n)
    def _():
        _gather_rows(nxt_ref, src_hbm, buf, 1 - slot, sem, n_rows, True)

    _gather_rows(cur_ref, src_hbm, buf, slot, sem, n_rows, False)
    return slot


def _dispatch_kernel(cur_ref, nxt_ref, h_hbm, o_ref, buf, sem):
    slot = _pipelined_gather(cur_ref, nxt_ref, h_hbm, buf, sem, MOE_TM)
    o_ref[...] = buf[slot].astype(o_ref.dtype)


def moe_dispatch(h, row_token, nt):
    L, d = h.shape
    idx = row_token.reshape(nt, 1, MOE_TM)
    smem = lambda f: pl.BlockSpec((1, 1, MOE_TM), f, memory_space=pltpu.SMEM)
    return pl.pallas_call(
        _dispatch_kernel,
        out_shape=jax.ShapeDtypeStruct((nt * MOE_TM, d), BF16),
        grid=(nt,),
        in_specs=[smem(lambda i: (i, 0, 0)),
                  smem(lambda i: (jnp.minimum(i + 1, nt - 1), 0, 0)),
                  pl.BlockSpec(memory_space=pl.ANY)],
        out_specs=pl.BlockSpec((MOE_TM, d), lambda i: (i, 0)),
        scratch_shapes=[pltpu.VMEM((2, MOE_TM, d), F32), pltpu.SemaphoreType.DMA((2,))],
        compiler_params=_params("arbitrary"),
        name="moe_dispatch",
    )(idx, idx, h)


def _moe_up_kernel(te_ref, tf_ref, nu_ref, a_ref, wg_ref, wu_ref, o_ref, wgb, wub):
    i = pl.program_id(1)

    @pl.when(tf_ref[i] == 1)
    def _():
        wgb[...] = wg_ref[0].astype(BF16)
        wub[...] = wu_ref[0].astype(BF16)

    @pl.when(i < nu_ref[0])
    def _():
        a = a_ref[...]
        o_ref[...] = (_silu(_dot(a, wgb[...])) * _dot(a, wub[...])).astype(o_ref.dtype)

    @pl.when(i >= nu_ref[0])
    def _():
        o_ref[...] = jnp.zeros_like(o_ref)


def moe_up(hs, w_gate, w_up, plan):
    r, d = hs.shape
    f = w_gate.shape[2]
    nt = plan["nt"]
    wspec = pl.BlockSpec((1, d, MOE_TN), lambda j, i, te, tf, nu: (te[i], 0, j))
    return pl.pallas_call(
        _moe_up_kernel,
        out_shape=jax.ShapeDtypeStruct((r, f), BF16),
        grid_spec=pltpu.PrefetchScalarGridSpec(
            num_scalar_prefetch=3,
            grid=(f // MOE_TN, nt),
            in_specs=[pl.BlockSpec((MOE_TM, d), lambda j, i, te, tf, nu: (i, 0)), wspec, wspec],
            out_specs=pl.BlockSpec((MOE_TM, MOE_TN), lambda j, i, te, tf, nu: (i, j)),
            scratch_shapes=[pltpu.VMEM((d, MOE_TN), BF16), pltpu.VMEM((d, MOE_TN), BF16)]),
        compiler_params=_params("arbitrary", "arbitrary"),
        name="moe_up",
    )(plan["tile_expert"], plan["tile_first"], plan["n_used"], hs, w_gate, w_up)


def _moe_down_kernel(te_ref, tf_ref, nu_ref, t_ref, wd_ref, o_ref, wdb):
    i = pl.program_id(1)

    @pl.when(tf_ref[i] == 1)
    def _():
        wdb[...] = wd_ref[0].astype(BF16)

    @pl.when(i < nu_ref[0])
    def _():
        o_ref[...] = _dot(t_ref[...], wdb[...])

    @pl.when(i >= nu_ref[0])
    def _():
        o_ref[...] = jnp.zeros_like(o_ref)


def moe_down(t, w_down, plan):
    r, f = t.shape
    d = w_down.shape[2]
    nt = plan["nt"]
    return pl.pallas_call(
        _moe_down_kernel,
        out_shape=jax.ShapeDtypeStruct((r, d), F32),
        grid_spec=pltpu.PrefetchScalarGridSpec(
            num_scalar_prefetch=3,
            grid=(d // MOE_TN, nt),
            in_specs=[pl.BlockSpec((MOE_TM, f), lambda j, i, te, tf, nu: (i, 0)),
                      pl.BlockSpec((1, f, MOE_TN), lambda j, i, te, tf, nu: (te[i], 0, j))],
            out_specs=pl.BlockSpec((MOE_TM, MOE_TN), lambda j, i, te, tf, nu: (i, j)),
            scratch_shapes=[pltpu.VMEM((f, MOE_TN), BF16)]),
        compiler_params=_params("arbitrary", "arbitrary"),
        name="moe_down",
    )(plan["tile_expert"], plan["tile_first"], plan["n_used"], t, w_down)


def _combine_kernel(cur_ref, nxt_ref, y_hbm, x_ref, route_ref, g_ref, o_ref, buf, sem, *, final):
    tm = x_ref.shape[0]
    slot = _pipelined_gather(cur_ref, nxt_ref, y_hbm, buf, sem, 2 * tm)
    route = route_ref[...]
    g1 = route[:, ROUTE_G1:ROUTE_G1 + 1]
    g2 = route[:, ROUTE_G2:ROUTE_G2 + 1]
    x = x_ref[...] + g1 * buf[slot, :tm, :] + g2 * buf[slot, tm:, :]
    if final:
        ms = jnp.mean(x * x, axis=-1, keepdims=True)
        x = x * lax.rsqrt(ms + RMS_EPS) * g_ref[...]
    o_ref[...] = x


def moe_combine(x, y, dest, route, final_gain):
    L, d = x.shape
    tm = COMBINE_TM
    nt = L // tm
    idx = jnp.concatenate([dest[:L].reshape(nt, 1, tm), dest[L:].reshape(nt, 1, tm)], axis=2)
    smem = lambda f: pl.BlockSpec((1, 1, 2 * tm), f, memory_space=pltpu.SMEM)
    final = final_gain is not None
    gain = (final_gain if final else jnp.ones((d,), F32)).reshape(1, d)
    kern = functools.partial(_combine_kernel, final=final)
    return pl.pallas_call(
        kern,
        out_shape=jax.ShapeDtypeStruct((L, d), F32),
        grid=(nt,),
        in_specs=[smem(lambda i: (i, 0, 0)),
                  smem(lambda i: (jnp.minimum(i + 1, nt - 1), 0, 0)),
                  pl.BlockSpec(memory_space=pl.ANY),
                  pl.BlockSpec((tm, d), lambda i: (i, 0)),
                  pl.BlockSpec((tm, LANES), lambda i: (i, 0)),
                  pl.BlockSpec((1, d), lambda i: (0, 0))],
        out_specs=pl.BlockSpec((tm, d), lambda i: (i, 0)),
        scratch_shapes=[pltpu.VMEM((2, 2 * tm, d), F32), pltpu.SemaphoreType.DMA((2,))],
        compiler_params=_params("arbitrary"),
        name="moe_combine",
    )(idx, idx, y, x, route, gain)


def moe_ffn(x, norm_g, router_w, w_gate, w_up, w_down, final_gain):
    L = x.shape[0]
    h, route = rmsnorm_router(x, norm_g, router_w)
    plan = _moe_plan(route, L)
    hs = moe_dispatch(h, plan["row_token"], plan["nt"])
    t = moe_up(hs, w_gate, w_up, plan)
    y = moe_down(t, w_down, plan)
    return moe_combine(x, y, plan["dest"], route, final_gain)


def _layer_mixers(x, l, tables, norm_mix, w_in, conv_w, conv_b, filt, hyena_skip, w_branch_a,
                  gla_gate_w2, gla_gate_b, gla_norm, w_branch_b, w_out):
    hc = (HYENA_ORDER + 1) * HYENA_WIDTH
    o_lr = hc + 2 * GLA_DK_TOTAL + 2 * GLA_DV_TOTAL
    o_g = o_lr + 2 * GLA_LOWRANK
    wl = w_in[l].astype(BF16)
    h = rmsnorm(x, norm_mix[l], BF16)
    proj = functools.partial(matmul, [h], dots=[(0, 0)], extras=[], tm=1024, tn=512)
    u_hy = proj([wl[:, :hc]], epilogue=_ep_plain, out_dtype=F32, name="in_proj_hyena")
    qkvr = proj([wl[:, hc:o_lr]], epilogue=_ep_plain, out_dtype=F32, name="in_proj_gla")
    gate_a = proj([wl[:, o_g:o_g + D_MODEL]], epilogue=_ep_sigmoid, out_dtype=BF16,
                  name="in_proj_gate_a")
    gate_b = proj([wl[:, o_g + D_MODEL:]], epilogue=_ep_sigmoid, out_dtype=BF16,
                  name="in_proj_gate_b")
    la = gla_gates(h, wl[:, o_lr:o_g], gla_gate_w2[l], gla_gate_b[l])

    z_a = hyena_mixer(u_hy, conv_w[l], conv_b[l], filt, hyena_skip[l], tables)
    o_f, o_b = gla_scan(qkvr, la)
    z_b = gla_post(o_f, o_b, qkvr, gla_norm[l])

    mixed = matmul([z_a, z_b], [w_branch_a[l].astype(BF16), w_branch_b[l].astype(BF16)],
                   [(0, 0), (1, 1)], _ep_merge, [(gate_a, "mn"), (gate_b, "mn")], BF16,
                   tm=1024, tn=512, name="branch_merge")
    return matmul([mixed], [w_out[l].astype(BF16)], [(0, 0)], _ep_residual, [(x, "mn")], F32,
                  tm=1024, tn=512, name="out_proj")


def _swiglu_ffn(x, h, wg, wu, wd):
    t = matmul([h], [wg, wu], [(0, 0), (0, 1)], _ep_swiglu, [], BF16,
               tm=1024, tn=512, name="ffn_up")
    return matmul([t], [wd], [(0, 0)], _ep_residual, [(x, "mn")], F32,
                  tm=512, tn=512, name="ffn_down")


def kernel(x, norm_mix, w_in, conv_w, conv_b, filt_w1, filt_b1, filt_freq1, filt_w2, filt_b2, filt_freq2, filt_w3, filt_b3, filt_freq3, filt_w4, hyena_skip, w_branch_a, gla_gate_w2, gla_gate_b, gla_norm, w_branch_b, w_out, norm_ffn, dense_w_gate, dense_w_up, dense_w_down, router_w, moe_w_gate, moe_w_up, moe_w_down, norm_final):
    b, L, d = x.shape
    assert b == 1
    x = x.reshape(L, d)
    tables = _fft_tables(L)
    for l in range(DEPTH):
        filt = (filt_w1[l], filt_b1[l], filt_freq1[l], filt_w2[l], filt_b2[l], filt_freq2[l],
                filt_w3[l], filt_b3[l], filt_freq3[l], filt_w4[l])
        x = _layer_mixers(x, l, tables, norm_mix, w_in, conv_w, conv_b, filt, hyena_skip,
                          w_branch_a, gla_gate_w2, gla_gate_b, gla_norm, w_branch_b, w_out)
        i = l // 2
        last = l == DEPTH - 1
        if l % 2 == 0:
            h = rmsnorm(x, norm_ffn[l], BF16)
            x = _swiglu_ffn(x, h, dense_w_gate[i].astype(BF16), dense_w_up[i].astype(BF16),
                            dense_w_down[i].astype(BF16))
            if last:
                x = rmsnorm(x, norm_final, F32)
        else:
            x = moe_ffn(x, norm_ffn[l], router_w[i], moe_w_gate[i], moe_w_up[i], moe_w_down[i],
                        norm_final if last else None)
    return x.reshape(b, L, d)
```

```python
import functools
import math

import jax
import jax.numpy as jnp
from jax import lax
from jax.experimental import pallas as pl
from jax.experimental.pallas import tpu as pltpu

F32 = jnp.float32
BF16 = jnp.bfloat16

D_MODEL = 2048
DEPTH = 2
HYENA_WIDTH = D_MODEL // 2
HYENA_ORDER = 2
SHORT_CONV = 3
POS_EMB_DIM = 33
POS_BANDS = (POS_EMB_DIM - 1) // 2
FILTER_HIDDEN = 64
MIN_DECAY = math.log(1e-2) / 1.5
MAX_DECAY = math.log(1e-2) / 0.3
GLA_HEADS = 4
GLA_DK = 128
GLA_DV = 256
GLA_DK_TOTAL = GLA_HEADS * GLA_DK
GLA_DV_TOTAL = GLA_HEADS * GLA_DV
GLA_LOWRANK = 16
GLA_GATE_TAU = 16.0
GLA_CHUNK = 64
N_EXPERTS = 8
RMS_EPS = 1e-6

LANES = 128
VMEM_LIMIT_BYTES = 56 * 1024 * 1024


def _params(*sem):
    return pltpu.CompilerParams(dimension_semantics=sem, vmem_limit_bytes=VMEM_LIMIT_BYTES)


def _dot(a, b):
    return jnp.dot(a, b, preferred_element_type=F32)


def _split(a):
    hi = a.astype(BF16)
    lo = (a - hi.astype(F32)).astype(BF16)
    return hi, lo


def _dot3(a, b):
    ah, al = _split(a)
    bh, bl = _split(b)
    return _dot(ah, bh) + (_dot(ah, bl) + _dot(al, bh))


def _sigmoid(x):
    return 1.0 / (1.0 + jnp.exp(-x))


def _silu(x):
    return x * _sigmoid(x)


def _log_sigmoid(x):
    return jnp.minimum(x, 0.0) - jnp.log(1.0 + jnp.exp(-jnp.abs(x)))


def _rmsnorm_kernel(x_ref, g_ref, o_ref):
    x = x_ref[...]
    ms = jnp.mean(x * x, axis=-1, keepdims=True)
    o_ref[...] = (x * lax.rsqrt(ms + RMS_EPS) * g_ref[...]).astype(o_ref.dtype)


def rmsnorm(x, g, out_dtype, tm=256):
    m, d = x.shape
    return pl.pallas_call(
        _rmsnorm_kernel,
        out_shape=jax.ShapeDtypeStruct((m, d), out_dtype),
        grid=(m // tm,),
        in_specs=[pl.BlockSpec((tm, d), lambda i: (i, 0)),
                  pl.BlockSpec((1, d), lambda i: (0, 0))],
        out_specs=pl.BlockSpec((tm, d), lambda i: (i, 0)),
        compiler_params=_params("parallel"),
        name="rmsnorm",
    )(x, g.reshape(1, d))


def _rmsnorm_router_kernel(x_ref, g_ref, rw_ref, o_ref, cw_ref):
    x = x_ref[...]
    ms = jnp.mean(x * x, axis=-1, keepdims=True)
    h = x * lax.rsqrt(ms + RMS_EPS) * g_ref[...]
    o_ref[...] = h.astype(o_ref.dtype)
    logits = _dot3(h, rw_ref[...])
    col = lax.broadcasted_iota(jnp.int32, logits.shape, 1).astype(F32)
    neg = jnp.float32(-jnp.inf)
    logits = jnp.where(col < N_EXPERTS, logits, neg)
    m1 = jnp.max(logits, axis=-1, keepdims=True)
    i1 = jnp.min(jnp.where(logits == m1, col, float(LANES)), axis=-1, keepdims=True)
    rest = jnp.where(col == i1, neg, logits)
    m2 = jnp.max(rest, axis=-1, keepdims=True)
    i2 = jnp.min(jnp.where(rest == m2, col, float(LANES)), axis=-1, keepdims=True)
    e2 = jnp.exp(m2 - m1)
    g1 = 1.0 / (1.0 + e2)
    g2 = e2 / (1.0 + e2)
    cw_ref[...] = (jnp.where(col == ROUTE_E1, i1, 0.0) + jnp.where(col == ROUTE_E2, i2, 0.0)
                   + jnp.where(col == ROUTE_G1, g1, 0.0) + jnp.where(col == ROUTE_G2, g2, 0.0))


ROUTE_E1, ROUTE_E2, ROUTE_G1, ROUTE_G2 = 0, 1, 2, 3


def rmsnorm_router(x, g, router_w, tm=256):
    m, d = x.shape
    rw = jnp.zeros((d, LANES), F32).at[:, :N_EXPERTS].set(router_w)
    return pl.pallas_call(
        _rmsnorm_router_kernel,
        out_shape=(jax.ShapeDtypeStruct((m, d), F32), jax.ShapeDtypeStruct((m, LANES), F32)),
        grid=(m // tm,),
        in_specs=[pl.BlockSpec((tm, d), lambda i: (i, 0)),
                  pl.BlockSpec((1, d), lambda i: (0, 0)),
                  pl.BlockSpec((d, LANES), lambda i: (0, 0))],
        out_specs=(pl.BlockSpec((tm, d), lambda i: (i, 0)),
                   pl.BlockSpec((tm, LANES), lambda i: (i, 0))),
        compiler_params=_params("parallel"),
        name="rmsnorm_router",
    )(x, g.reshape(1, d), rw)


def _mm_kernel(*refs, n_a, n_w, dots, n_extra, epilogue):
    a_refs = refs[:n_a]
    w_refs = refs[n_a:n_a + n_w]
    e_refs = refs[n_a + n_w:n_a + n_w + n_extra]
    o_ref = refs[n_a + n_w + n_extra]
    wb_refs = refs[n_a + n_w + n_extra + 1:]

    @pl.when(pl.program_id(1) == 0)
    def _():
        for w, wb in zip(w_refs, wb_refs):
            wb[...] = w[0].astype(BF16)

    a_vals = [a[...].astype(BF16) for a in a_refs]
    accs = [_dot(a_vals[ai], wb_refs[wi][...]) for ai, wi in dots]
    o_ref[...] = epilogue(accs, [e[...] for e in e_refs]).astype(o_ref.dtype)


def matmul(a_list, w_list, n, dots, epilogue, extras, out_dtype, *, tm, tn, name):
    m = a_list[0].shape[0]
    assert m % tm == 0 and n % tn == 0
    in_specs = [pl.BlockSpec((tm, a.shape[1]), lambda j, i: (i, 0)) for a in a_list]
    w_arrays, scratch = [], []
    for w, s, first in w_list:
        assert first % tn == 0
        w_arrays.append(w)
        in_specs.append(pl.BlockSpec((1, w.shape[1], tn),
                                     lambda j, i, s=s, off=first // tn: (s, 0, off + j)))
        scratch.append(pltpu.VMEM((w.shape[1], tn), BF16))
    extra_arrays = []
    for arr, first in extras:
        assert first % tn == 0
        extra_arrays.append(arr)
        in_specs.append(pl.BlockSpec((tm, tn), lambda j, i, off=first // tn: (i, off + j)))
    kern = functools.partial(_mm_kernel, n_a=len(a_list), n_w=len(w_list), dots=tuple(dots),
                             n_extra=len(extras), epilogue=epilogue)
    return pl.pallas_call(
        kern,
        out_shape=jax.ShapeDtypeStruct((m, n), out_dtype),
        grid=(n // tn, m // tm),
        in_specs=in_specs,
        out_specs=pl.BlockSpec((tm, tn), lambda j, i: (i, j)),
        scratch_shapes=scratch,
        compiler_params=_params("arbitrary", "arbitrary"),
        name=name,
    )(*a_list, *w_arrays, *extra_arrays)


def _ep_plain(accs, ex):
    return accs[0]


def _ep_sigmoid(accs, ex):
    return _sigmoid(accs[0])


def _ep_residual(accs, ex):
    return ex[0] + accs[0]


def _ep_merge(accs, ex):
    return ex[0].astype(F32) * accs[0] + ex[1].astype(F32) * accs[1]


def _ep_swiglu(accs, ex):
    return _silu(accs[0]) * accs[1]


def _gate_kernel(h_ref, wlr_ref, w2_ref, b_ref, o_ref):
    lr = _dot(h_ref[...], wlr_ref[...].astype(BF16))
    z = _dot3(lr, w2_ref[...]) + b_ref[...]
    o_ref[...] = _log_sigmoid(z) * (1.0 / GLA_GATE_TAU)


def gla_gates(h, w_lr, gate_w2, gate_b, tm=512):
    m, d = h.shape
    r = GLA_LOWRANK
    n = 2 * GLA_DK_TOTAL
    wlr = jnp.zeros((d, LANES), F32).at[:, :2 * r].set(w_lr)
    w2 = jnp.zeros((LANES, n), F32)
    w2 = w2.at[:r, :GLA_DK_TOTAL].set(gate_w2[0]).at[r:2 * r, GLA_DK_TOTAL:].set(gate_w2[1])
    b = gate_b.reshape(1, n)
    return pl.pallas_call(
        _gate_kernel,
        out_shape=jax.ShapeDtypeStruct((m, n), F32),
        grid=(m // tm,),
        in_specs=[pl.BlockSpec((tm, d), lambda i: (i, 0)),
                  pl.BlockSpec((d, LANES), lambda i: (0, 0)),
                  pl.BlockSpec((LANES, n), lambda i: (0, 0)),
                  pl.BlockSpec((1, n), lambda i: (0, 0))],
        out_specs=pl.BlockSpec((tm, n), lambda i: (i, 0)),
        compiler_params=_params("parallel"),
        name="gla_gates",
    )(h, wlr, w2, b)


GLA_SUB = 4


def _gla_chunk(q, k, v, la, st_ref, h, reverse):
    c = GLA_CHUNK
    row = lax.broadcasted_iota(jnp.int32, (c, c), 0)
    col = lax.broadcasted_iota(jnp.int32, (c, c), 1)
    keep = (col > row) if reverse else (col <= row)
    b = la
    rows = lax.broadcasted_iota(jnp.int32, la.shape, 0)
    step = 1
    while step < c:
        if reverse:
            b = b + jnp.where(rows < c - step, pltpu.roll(b, c - step, 0), 0.0)
        else:
            b = b + jnp.where(rows >= step, pltpu.roll(b, step, 0), 0.0)
        step *= 2
    tot = b[0:1] if reverse else b[c - 1:c]
    q_in = (q * ((GLA_DK ** -0.5) * jnp.exp(b))).astype(BF16)
    k_dec = k * jnp.exp(-b)
    k_in = k_dec.astype(BF16)
    k_out = (k_dec * jnp.exp(tot)).astype(BF16)
    vb = v.astype(BF16)
    s = lax.dot_general(q_in, k_in, (((1,), (1,)), ((), ())), preferred_element_type=F32)
    s = jnp.where(keep, s, 0.0).astype(BF16)
    st = st_ref[h]
    o = _dot(s, vb) + lax.dot_general(q_in, st.astype(BF16), (((1,), (1,)), ((), ())),
                                      preferred_element_type=F32)
    upd = lax.dot_general(vb, k_out, (((0,), (0,)), ((), ())), preferred_element_type=F32)
    st_ref[h] = st * jnp.exp(tot) + upd
    return o


def _gla_kernel(qf_ref, kf_ref, vf_ref, lf_ref, qb_ref, kb_ref, vb_ref, lb_ref,
                of_ref, ob_ref, sf_ref, sb_ref):
    @pl.when(pl.program_id(0) == 0)
    def _():
        sf_ref[...] = jnp.zeros_like(sf_ref)
        sb_ref[...] = jnp.zeros_like(sb_ref)

    def body(s, carry):
        rf = pl.multiple_of(s * GLA_CHUNK, GLA_CHUNK)
        rb = pl.multiple_of((GLA_SUB - 1 - s) * GLA_CHUNK, GLA_CHUNK)
        for h in range(GLA_HEADS):
            kk = slice(h * GLA_DK, (h + 1) * GLA_DK)
            vv = slice(h * GLA_DV, (h + 1) * GLA_DV)
            of_ref[pl.ds(rf, GLA_CHUNK), vv] = _gla_chunk(
                qf_ref[pl.ds(rf, GLA_CHUNK), kk], kf_ref[pl.ds(rf, GLA_CHUNK), kk],
                vf_ref[pl.ds(rf, GLA_CHUNK), vv], lf_ref[pl.ds(rf, GLA_CHUNK), kk],
                sf_ref, h, False)
            ob_ref[pl.ds(rb, GLA_CHUNK), vv] = _gla_chunk(
                qb_ref[pl.ds(rb, GLA_CHUNK), kk], kb_ref[pl.ds(rb, GLA_CHUNK), kk],
                vb_ref[pl.ds(rb, GLA_CHUNK), vv], lb_ref[pl.ds(rb, GLA_CHUNK), kk],
                sb_ref, h, True)
        return carry

    lax.fori_loop(0, GLA_SUB, body, 0, unroll=True)


def gla_scan(qkvr, first, la):
    L = qkvr.shape[0]
    rows = GLA_CHUNK * GLA_SUB
    n = L // rows
    dk, dv = GLA_DK_TOTAL, GLA_DV_TOTAL
    assert first % dv == 0
    qb, vb = first // dk, (first + 2 * dk) // dv
    specs = []
    for rev in (False, True):
        blk = (lambda i: n - 1 - i) if rev else (lambda i: i)
        specs += [pl.BlockSpec((rows, dk), lambda i, blk=blk: (blk(i), qb)),
                  pl.BlockSpec((rows, dk), lambda i, blk=blk: (blk(i), qb + 1)),
                  pl.BlockSpec((rows, dv), lambda i, blk=blk: (blk(i), vb)),
                  pl.BlockSpec((rows, dk), lambda i, blk=blk, c=int(rev): (blk(i), c))]
    return pl.pallas_call(
        _gla_kernel,
        out_shape=(jax.ShapeDtypeStruct((L, dv), F32), jax.ShapeDtypeStruct((L, dv), F32)),
        grid=(n,),
        in_specs=specs,
        out_specs=(pl.BlockSpec((rows, dv), lambda i: (i, 0)),
                   pl.BlockSpec((rows, dv), lambda i: (n - 1 - i, 0))),
        scratch_shapes=[pltpu.VMEM((GLA_HEADS, GLA_DV, GLA_DK), F32),
                        pltpu.VMEM((GLA_HEADS, GLA_DV, GLA_DK), F32)],
        compiler_params=_params("arbitrary"),
        name="gla_scan",
    )(qkvr, qkvr, qkvr, la, qkvr, qkvr, qkvr, la)


def _gla_post_kernel(of_ref, ob_ref, r_ref, g_ref, o_ref):
    g = g_ref[...]
    for h in range(GLA_HEADS):
        vv = slice(h * GLA_DV, (h + 1) * GLA_DV)
        o = of_ref[:, vv] + ob_ref[:, vv]
        ms = jnp.mean(o * o, axis=-1, keepdims=True)
        y = o * lax.rsqrt(ms + RMS_EPS) * g
        o_ref[:, vv] = (y * _silu(r_ref[:, vv])).astype(o_ref.dtype)


def gla_post(o_f, o_b, qkvr, first, norm_g, tm=512):
    L, dv = o_f.shape
    assert first % dv == 0
    return pl.pallas_call(
        _gla_post_kernel,
        out_shape=jax.ShapeDtypeStruct((L, dv), BF16),
        grid=(L // tm,),
        in_specs=[pl.BlockSpec((tm, dv), lambda i: (i, 0)),
                  pl.BlockSpec((tm, dv), lambda i: (i, 0)),
                  pl.BlockSpec((tm, dv), lambda i: (i, first // dv)),
                  pl.BlockSpec((1, GLA_DV), lambda i: (0, 0))],
        out_specs=pl.BlockSpec((tm, dv), lambda i: (i, 0)),
        compiler_params=_params("parallel"),
        name="gla_post",
    )(o_f, o_b, qkvr, norm_g.reshape(1, GLA_DV))


CT = 256
CT_G = 512
CT_COL = 128
MID_SLABS = 8
SUB = 8


def _fft_dims(L):
    n = 2 * L
    na = 1 << ((n.bit_length() - 1) // 2)
    nb = n // na
    assert na * nb == n and nb % SUB == 0 and (na // 2) % 8 == 0
    kp = -(-(na // 2 + 1) // 8) * 8
    return n, na, nb, kp


def _cis(num, den):
    ang = (2.0 * math.pi / den) * (num % den).astype(F32)
    return jnp.cos(ang), jnp.sin(ang)


def _expand_kernel(t_ref, o_ref):
    t = t_ref[0].astype(BF16)
    cc, cols = t.shape[1], o_ref.shape[2]
    shift, low = SUB.bit_length() - 1, SUB - 1
    src = lax.broadcasted_iota(jnp.int32, (cc, cols), 0)
    dst = lax.broadcasted_iota(jnp.int32, (cc, cols), 1)
    spread = jnp.where(lax.shift_right_logical(dst, shift) == src, 1.0, 0.0).astype(BF16)
    x = _dot(t, spread)
    row = lax.broadcasted_iota(jnp.int32, x.shape, 0)
    col = lax.broadcasted_iota(jnp.int32, x.shape, 1)
    o_ref[0] = jnp.where((col & low) == (row & low), x, 0.0).astype(o_ref.dtype)


def _expand_block_diag(t):
    g, r, c = t.shape
    return pl.pallas_call(
        _expand_kernel,
        out_shape=jax.ShapeDtypeStruct((g, r, c * SUB), BF16),
        grid=(g,),
        in_specs=[pl.BlockSpec((1, r, c), lambda i: (i, 0, 0))],
        out_specs=pl.BlockSpec((1, r, c * SUB), lambda i: (i, 0, 0)),
        compiler_params=_params("parallel"),
        name="dft_table_expand",
    )(t)


def _fft_tables(L):
    n, na, nb, kp = _fft_dims(L)
    ha, ng = na // 2, nb // SUB
    ar = lambda m: jnp.arange(m, dtype=jnp.int32)
    g_, ka_, bl_, a_ = ar(ng)[:, None, None, None], ar(kp)[None, :, None, None], \
        ar(SUB)[None, None, :, None], ar(ha)[None, None, None, :]
    c, s = _cis(ka_ * (nb * a_ + SUB * g_ + bl_), n)
    cs = jnp.stack([c, -s], axis=2)
    m8 = _expand_block_diag(cs.reshape(ng, kp * 2 * SUB, ha))
    c, s = _cis(ar(nb)[:, None] * ar(nb)[None, :], nb)
    f2 = jnp.stack([jnp.stack([c, s], axis=1), jnp.stack([-s, c], axis=1)], axis=0)
    f2 = f2.reshape(2, nb, 2, ng, SUB).transpose(0, 1, 3, 2, 4).reshape(2 * nb, 2 * nb)
    ca, sa = _cis(ar(kp)[:, None] * ar(nb)[None, :], n)
    cb, sb = _cis(ar(nb)[:, None] * ar(nb)[None, :], nb)
    ca, sa = (t.reshape(kp, ng, 1, SUB, 1, 1) / n for t in (ca, sa))
    cb, sb = (t.reshape(1, ng, 1, SUB, 1, nb) for t in (cb, sb))
    c, s = ca * cb - sa * sb, sa * cb + ca * sb
    gi = jnp.concatenate([jnp.concatenate([c, -s], axis=4),
                          jnp.concatenate([s, c], axis=4)], axis=2)
    gi = gi.reshape(kp, 2 * nb, 2 * nb)
    kc = ar(kp)[None, :]
    wgt = jnp.where((kc == 0) | (kc == ha), 1.0, jnp.where(kc < ha, 2.0, 0.0))
    c, s = _cis(ar(ha)[:, None] * kc, na)
    pm = jnp.stack([wgt * c, -wgt * s], axis=2).reshape(ha, 1, kp * 2)
    p8 = _expand_block_diag(jnp.broadcast_to(pm, (ha, SUB, kp * 2)).reshape(1, ha * SUB, kp * 2))[0]
    return m8, f2.astype(BF16), gi.astype(BF16), p8


def _short_conv_kernel(u_ref, w_ref, b_ref, o_ref):
    u = u_ref[...]
    L = u.shape[0]
    row = lax.broadcasted_iota(jnp.int32, u.shape, 0)
    prev = jnp.where(row == 0, 0.0, pltpu.roll(u, 1, 0))
    nxt = jnp.where(row == L - 1, 0.0, pltpu.roll(u, L - 1, 0))
    w = w_ref[...]
    o_ref[...] = prev * w[0:1] + u * w[1:2] + nxt * w[2:3] + b_ref[...]


def short_conv(u, w, b):
    L, c = u.shape[0], w.shape[1]
    w8 = jnp.zeros((8, c), F32).at[:SHORT_CONV].set(w)
    return pl.pallas_call(
        _short_conv_kernel,
        out_shape=jax.ShapeDtypeStruct((L, c), F32),
        grid=(c // CT_COL,),
        in_specs=[pl.BlockSpec((L, CT_COL), lambda j: (0, j)),
                  pl.BlockSpec((8, CT_COL), lambda j: (0, j)),
                  pl.BlockSpec((1, CT_COL), lambda j: (0, j))],
        out_specs=pl.BlockSpec((L, CT_COL), lambda j: (0, j)),
        compiler_params=_params("parallel"),
        name="short_conv",
    )(u, w8, b.reshape(1, c))


def _fft_s1_kernel(u_ref, m8_ref, a_ref):
    ha, sub, ct = u_ref.shape
    x = u_ref[...].reshape(ha * sub, ct).astype(BF16)
    r = _dot(m8_ref[0], x)
    a_ref[:, 0, :, :] = r.astype(a_ref.dtype).reshape(a_ref.shape[0], 2 * sub, ct)


def fft_stage1(u, ncols, m8, L):
    n, na, nb, kp = _fft_dims(L)
    ha, ng = na // 2, nb // SUB
    u3 = u.reshape(ha, nb, u.shape[1])
    return pl.pallas_call(
        _fft_s1_kernel,
        out_shape=jax.ShapeDtypeStruct((kp, ng, 2 * SUB, ncols), BF16),
        grid=(ng, ncols // CT_G),
        in_specs=[pl.BlockSpec((ha, SUB, CT_G), lambda g, j: (0, g, j)),
                  pl.BlockSpec((1, kp * 2 * SUB, ha * SUB), lambda g, j: (g, 0, 0))],
        out_specs=pl.BlockSpec((kp, 1, 2 * SUB, CT_G), lambda g, j: (0, g, 0, j)),
        compiler_params=_params("parallel", "parallel"),
        name="fft_stage1",
    )(u3, m8)


def _filter_s1_kernel(z_ref, w1_ref, b1_ref, f1_ref, w2_ref, b2_ref, f2_ref, w3_ref, b3_ref,
                      f3_ref, w4_ref, dl_ref, m8_ref, a_ref, nrm_ref, hh_ref, hl_ref, *, nb, L, cb):
    g, j = pl.program_id(0), pl.program_id(1)
    ha, sub, _ = z_ref.shape
    rows = ha * sub

    @pl.when(j == 0)
    def _():
        z = z_ref[...].reshape(rows, z_ref.shape[2])
        h = jnp.sin(f1_ref[...] * (_dot3(z, w1_ref[...]) + b1_ref[...]))
        h = jnp.sin(f2_ref[...] * (_dot3(h, w2_ref[...]) + b2_ref[...]))
        h = jnp.sin(f3_ref[...] * (_dot3(h, w3_ref[...]) + b3_ref[...]))
        hh_ref[...], hl_ref[...] = _split(h)

    hh, hl = hh_ref[...], hl_ref[...]
    wh, wl = _split(w4_ref[...])
    ct = wh.shape[1]
    k = (_dot(hh, wh) + (_dot(hh, wl) + _dot(hl, wh))).reshape(ha, sub, ct)
    a3 = lax.broadcasted_iota(jnp.int32, (ha, sub, ct), 0)
    b3 = lax.broadcasted_iota(jnp.int32, (ha, sub, ct), 1)
    tpos = a3 * nb + (b3 + g * sub)
    k = k * jnp.exp(-(tpos.astype(F32) * (1.0 / (L - 1))) * dl_ref[...])
    k = jnp.where(jnp.logical_and((j // cb) % 2 == 1, tpos == 0), 0.0, k).reshape(rows, ct)
    nrm_ref[0] = jnp.sum(jnp.abs(k), axis=0, keepdims=True)
    q = _dot(m8_ref[0], k.astype(BF16))
    a_ref[:, 0, :, :] = q.astype(a_ref.dtype).reshape(a_ref.shape[0], 2 * sub, ct)


def filter_stage1(L, w1, b1, f1, w2, b2, f2, w3, b3, f3, w4, m8):
    n, na, nb, kp = _fft_dims(L)
    ha, ng = na // 2, nb // SUB
    c = HYENA_WIDTH
    hdim = FILTER_HIDDEN
    t = jnp.linspace(0.0, 1.0, L, dtype=F32)[:, None]
    omega = 2.0 * math.pi * jnp.arange(L, dtype=F32)[:, None] / L
    bands = jnp.linspace(1e-4, POS_BANDS - 1, POS_BANDS, dtype=F32)[None, :]
    z = jnp.concatenate([t, jnp.cos(bands * omega), -jnp.sin(bands * omega)], axis=-1)
    z = jnp.zeros((L, LANES), F32).at[:, :POS_EMB_DIM].set(z).reshape(ha, nb, LANES)
    w1p = jnp.zeros((LANES, hdim), F32).at[:POS_EMB_DIM].set(w1)
    deltas = jnp.abs(jnp.linspace(MIN_DECAY, MAX_DECAY, c, dtype=F32)).reshape(1, c)
    nc = w4.shape[1]
    cb = c // CT_G
    row = lambda v: v.reshape(1, hdim)
    full = lambda shape: pl.BlockSpec(shape, lambda g, j: (0,) * len(shape))
    kern = functools.partial(_filter_s1_kernel, nb=nb, L=L, cb=cb)
    return pl.pallas_call(
        kern,
        out_shape=(jax.ShapeDtypeStruct((kp, ng, 2 * SUB, nc), BF16),
                   jax.ShapeDtypeStruct((ng, 1, nc), F32)),
        grid=(ng, nc // CT_G),
        in_specs=[pl.BlockSpec((ha, SUB, LANES), lambda g, j: (0, g, 0)),
                  full((LANES, hdim)), full((1, hdim)), full((1, hdim)),
                  full((hdim, hdim)), full((1, hdim)), full((1, hdim)),
                  full((hdim, hdim)), full((1, hdim)), full((1, hdim)),
                  pl.BlockSpec((hdim, CT_G), lambda g, j: (0, j)),
                  pl.BlockSpec((1, CT_G), lambda g, j: (0, j % cb)),
                  pl.BlockSpec((1, kp * 2 * SUB, ha * SUB), lambda g, j: (g, 0, 0))],
        out_specs=(pl.BlockSpec((kp, 1, 2 * SUB, CT_G), lambda g, j: (0, g, 0, j)),
                   pl.BlockSpec((1, 1, CT_G), lambda g, j: (g, 0, j))),
        scratch_shapes=[pltpu.VMEM((ha * SUB, hdim), BF16), pltpu.VMEM((ha * SUB, hdim), BF16)],
        compiler_params=_params("arbitrary", "arbitrary"),
        name="hyena_filter_stage1",
    )(z, w1p, row(b1), row(f1), w2, row(b2), row(f2), w3, row(b3), row(f3), w4, deltas, m8)


def _slab(ref, s):
    _, ng, rows, ct = ref.shape
    return ref[s].reshape(ng * rows, ct)


def _filter_spec_kernel(af_ref, ab_ref, f2_ref, nf_ref, nbk_ref, h_ref):
    f2 = f2_ref[...]
    nb = f2.shape[0] // 2
    inv = 1.0 / (jnp.sum(nf_ref[...], axis=0) + jnp.sum(nbk_ref[...], axis=0))
    for s in range(MID_SLABS):
        xf = _dot(f2, _slab(af_ref, s))
        xb = _dot(f2, _slab(ab_ref, s))
        h_ref[0, s] = ((xf[:nb] + xb[:nb]) * inv).astype(h_ref.dtype)
        h_ref[1, s] = ((xf[nb:] - xb[nb:]) * inv).astype(h_ref.dtype)


def filter_spectrum(a, nrm, f2, L):
    n, na, nb, kp = _fft_dims(L)
    ng = nb // SUB
    c = HYENA_WIDTH
    cb = c // CT
    fcol = lambda j: (j // cb) * 2 * cb + j % cb
    return pl.pallas_call(
        _filter_spec_kernel,
        out_shape=jax.ShapeDtypeStruct((2, kp, nb, HYENA_ORDER * c), BF16),
        grid=(HYENA_ORDER * cb, kp // MID_SLABS),
        in_specs=[pl.BlockSpec((MID_SLABS, ng, 2 * SUB, CT), lambda j, k: (k, 0, 0, fcol(j))),
                  pl.BlockSpec((MID_SLABS, ng, 2 * SUB, CT), lambda j, k: (k, 0, 0, fcol(j) + cb)),
                  pl.BlockSpec((2 * nb, 2 * nb), lambda j, k: (0, 0)),
                  pl.BlockSpec((ng, 1, CT), lambda j, k: (0, 0, fcol(j))),
                  pl.BlockSpec((ng, 1, CT), lambda j, k: (0, 0, fcol(j) + cb))],
        out_specs=pl.BlockSpec((2, MID_SLABS, nb, CT), lambda j, k: (0, k, 0, j)),
        compiler_params=_params("parallel", "parallel"),
        name="hyena_filter_spectrum",
    )(a, a, f2, nrm, nrm)


def _fft_mid_kernel(a_ref, h_ref, f2_ref, g_ref, q_ref):
    f2 = f2_ref[...]
    nb = f2.shape[0] // 2
    _, ng, rows, ct = q_ref.shape
    for s in range(MID_SLABS):
        x = _dot(f2, _slab(a_ref, s))
        xr, xi = x[:nb], x[nb:]
        hr, hi = h_ref[0, s].astype(F32), h_ref[1, s].astype(F32)
        y = jnp.concatenate([xr * hr - xi * hi, xr * hi + xi * hr], axis=0).astype(BF16)
        q_ref[s] = _dot(g_ref[s], y).astype(q_ref.dtype).reshape(ng, rows, ct)


def fft_mid(a, hf, order, f2, g, L):
    n, na, nb, kp = _fft_dims(L)
    ng = nb // SUB
    c = a.shape[3]
    cb = c // CT
    return pl.pallas_call(
        _fft_mid_kernel,
        out_shape=jax.ShapeDtypeStruct(a.shape, BF16),
        grid=(kp // MID_SLABS, cb),
        in_specs=[pl.BlockSpec((MID_SLABS, ng, 2 * SUB, CT), lambda k, j: (k, 0, 0, j)),
                  pl.BlockSpec((2, MID_SLABS, nb, CT), lambda k, j: (0, k, 0, order * cb + j)),
                  pl.BlockSpec((2 * nb, 2 * nb), lambda k, j: (0, 0)),
                  pl.BlockSpec((MID_SLABS, 2 * nb, 2 * nb), lambda k, j: (k, 0, 0))],
        out_specs=pl.BlockSpec((MID_SLABS, ng, 2 * SUB, CT), lambda k, j: (k, 0, 0, j)),
        compiler_params=_params("parallel", "parallel"),
        name="fft_mid",
    )(a, hf, f2, g)


def _fft_i2_kernel(q_ref, p8_ref, u_ref, x_ref, s_ref, o_ref):
    kp, _, rows, ct = q_ref.shape
    q = q_ref[:, 0, :, :].reshape(kp * rows, ct)
    y = _dot(p8_ref[...], q).reshape(o_ref.shape)
    o_ref[...] = x_ref[...] * (y + u_ref[...] * s_ref[...])


def fft_stage_out(q, u, gate_src, gate_blk, skip, p8, L):
    n, na, nb, kp = _fft_dims(L)
    ha, ng = na // 2, nb // SUB
    c = q.shape[3]
    u3 = u.reshape(ha, nb, u.shape[1])
    g3 = gate_src.reshape(ha, nb, gate_src.shape[1])
    out = pl.pallas_call(
        _fft_i2_kernel,
        out_shape=jax.ShapeDtypeStruct((ha, nb, c), F32),
        grid=(ng, c // CT_G),
        in_specs=[pl.BlockSpec((kp, 1, 2 * SUB, CT_G), lambda g, j: (0, g, 0, j)),
                  pl.BlockSpec((ha * SUB, kp * 2 * SUB), lambda g, j: (0, 0)),
                  pl.BlockSpec((ha, SUB, CT_G), lambda g, j: (0, g, j)),
                  pl.BlockSpec((ha, SUB, CT_G), lambda g, j: (0, g, gate_blk + j)),
                  pl.BlockSpec((1, 1, CT_G), lambda g, j: (0, 0, j))],
        out_specs=pl.BlockSpec((ha, SUB, CT_G), lambda g, j: (0, g, j)),
        compiler_params=_params("parallel", "parallel"),
        name="fft_stage_out",
    )(q, p8, u3, g3, skip.reshape(1, 1, c))
    return out.reshape(L, c)


def hyena_mixer(u_hy, conv_w, conv_b, filt, skip, tables):
    L = u_hy.shape[0]
    c = HYENA_WIDTH
    m8, f2, g, p8 = tables
    uc = short_conv(u_hy, conv_w, conv_b)
    a_filt, nrm = filter_stage1(L, *filt, m8)
    hf = filter_spectrum(a_filt, nrm, f2, L)
    z = uc
    for order in range(HYENA_ORDER):
        a = fft_stage1(z, c, m8, L)
        q = fft_mid(a, hf, order, f2, g, L)
        z = fft_stage_out(q, z, uc, (order + 1) * (c // CT_G), skip[order], p8, L)
    return z


MOE_TM = 512
MOE_TN = 512
COMBINE_TM = 256


def _moe_plan(route, L):
    tm = MOE_TM
    nt = -(-(2 * L + N_EXPERTS * (tm - 1)) // tm)
    e_flat = jnp.concatenate([route[:, ROUTE_E1], route[:, ROUTE_E2]]).astype(jnp.int32)
    onehot = (e_flat[:, None] == jnp.arange(N_EXPERTS, dtype=jnp.int32)[None, :]).astype(jnp.int32)
    csum = jnp.cumsum(onehot, axis=0)
    rank = jnp.sum(onehot * (csum - 1), axis=1)
    counts = csum[-1]
    padded = ((counts + tm - 1) // tm) * tm
    ends = jnp.cumsum(padded)
    dest = (ends - padded)[e_flat] + rank
    tok = jnp.tile(jnp.arange(L, dtype=jnp.int32), 2)
    row_token = jnp.zeros((nt * tm,), jnp.int32).at[dest].set(tok, unique_indices=True)
    n_used = ends[-1] // tm
    tile_row = jnp.arange(nt, dtype=jnp.int32) * tm
    tile_expert = jnp.sum((tile_row[:, None] >= ends[None, :]).astype(jnp.int32), axis=1)
    last = jnp.take(tile_expert, jnp.maximum(n_used - 1, 0))
    tile_expert = jnp.where(jnp.arange(nt) < n_used, tile_expert, last)
    tile_first = jnp.concatenate([jnp.ones((1,), jnp.int32),
                                  (tile_expert[1:] != tile_expert[:-1]).astype(jnp.int32)])
    return dict(nt=nt, row_token=row_token, dest=dest, tile_expert=tile_expert,
                tile_first=tile_first, n_used=n_used.reshape(1).astype(jnp.int32))


def _row_copy(src_hbm, idx, buf, slot, r, sem):
    return pltpu.make_async_copy(src_hbm.at[pl.ds(idx, 1)], buf.at[slot, pl.ds(r, 1)],
                                 sem.at[slot])


def _gather_rows(idx_ref, src_hbm, buf, slot, sem, n_rows, start):
    if not start:
        pltpu.make_async_copy(src_hbm.at[pl.ds(0, n_rows)], buf.at[slot], sem.at[slot]).wait()
        return

    def body(p, carry):
        for lane in range(2):
            r = 2 * p + lane
            _row_copy(src_hbm, idx_ref[0, 0, r], buf, slot, r, sem).start(priority=lane)
        return carry
    lax.fori_loop(0, n_rows // 2, body, 0, unroll=4)


def _pipelined_gather(cur_ref, nxt_ref, src_hbm, buf, sem, n_rows):
    i, n = pl.program_id(0), pl.num_programs(0)
    slot = lax.rem(i, 2)

    @pl.when(i == 0)
    def _():
        _gather_rows(cur_ref, src_hbm, buf, 0, sem, n_rows, True)

    @pl.when(i + 1 < n)
    def _():
        _gather_rows(nxt_ref, src_hbm, buf, 1 - slot, sem, n_rows, True)

    _gather_rows(cur_ref, src_hbm, buf, slot, sem, n_rows, False)
    return slot


def _dispatch_kernel(cur_ref, nxt_ref, h_hbm, o_ref, buf, sem):
    slot = _pipelined_gather(cur_ref, nxt_ref, h_hbm, buf, sem, MOE_TM)
    o_ref[...] = buf[slot].astype(o_ref.dtype)


def moe_dispatch(h, row_token, nt):
    L, d = h.shape
    idx = row_token.reshape(nt, 1, MOE_TM)
    smem = lambda f: pl.BlockSpec((1, 1, MOE_TM), f, memory_space=pltpu.SMEM)
    return pl.pallas_call(
        _dispatch_kernel,
        out_shape=jax.ShapeDtypeStruct((nt * MOE_TM, d), BF16),
        grid=(nt,),
        in_specs=[smem(lambda i: (i, 0, 0)),
                  smem(lambda i: (jnp.minimum(i + 1, nt - 1), 0, 0)),
                  pl.BlockSpec(memory_space=pl.ANY)],
        out_specs=pl.BlockSpec((MOE_TM, d), lambda i: (i, 0)),
        scratch_shapes=[pltpu.VMEM((2, MOE_TM, d), F32), pltpu.SemaphoreType.DMA((2,))],
        compiler_params=_params("arbitrary"),
        name="moe_dispatch",
    )(idx, idx, h)


def _moe_up_kernel(te_ref, tf_ref, nu_ref, a_ref, wg_ref, wu_ref, o_ref, wgb, wub):
    i = pl.program_id(1)

    @pl.when(tf_ref[i] == 1)
    def _():
        wgb[...] = wg_ref[0].astype(BF16)
        wub[...] = wu_ref[0].astype(BF16)

    @pl.when(i < nu_ref[0])
    def _():
        a = a_ref[...]
        o_ref[...] = (_silu(_dot(a, wgb[...])) * _dot(a, wub[...])).astype(o_ref.dtype)

    @pl.when(i >= nu_ref[0])
    def _():
        o_ref[...] = jnp.zeros_like(o_ref)


def moe_up(hs, w_gate, w_up, plan):
    r, d = hs.shape
    f = w_gate.shape[2]
    nt = plan["nt"]
    wspec = pl.BlockSpec((1, d, MOE_TN), lambda j, i, te, tf, nu: (te[i], 0, j))
    return pl.pallas_call(
        _moe_up_kernel,
        out_shape=jax.ShapeDtypeStruct((r, f), BF16),
        grid_spec=pltpu.PrefetchScalarGridSpec(
            num_scalar_prefetch=3,
            grid=(f // MOE_TN, nt),
            in_specs=[pl.BlockSpec((MOE_TM, d), lambda j, i, te, tf, nu: (i, 0)), wspec, wspec],
            out_specs=pl.BlockSpec((MOE_TM, MOE_TN), lambda j, i, te, tf, nu: (i, j)),
            scratch_shapes=[pltpu.VMEM((d, MOE_TN), BF16), pltpu.VMEM((d, MOE_TN), BF16)]),
        compiler_params=_params("arbitrary", "arbitrary"),
        name="moe_up",
    )(plan["tile_expert"], plan["tile_first"], plan["n_used"], hs, w_gate, w_up)


def _moe_down_kernel(te_ref, tf_ref, nu_ref, t_ref, wd_ref, o_ref, wdb):
    i = pl.program_id(1)

    @pl.when(tf_ref[i] == 1)
    def _():
        wdb[...] = wd_ref[0].astype(BF16)

    @pl.when(i < nu_ref[0])
    def _():
        o_ref[...] = _dot(t_ref[...], wdb[...])

    @pl.when(i >= nu_ref[0])
    def _():
        o_ref[...] = jnp.zeros_like(o_ref)


def moe_down(t, w_down, plan):
    r, f = t.shape
    d = w_down.shape[2]
    nt = plan["nt"]
    return pl.pallas_call(
        _moe_down_kernel,
        out_shape=jax.ShapeDtypeStruct((r, d), F32),
        grid_spec=pltpu.PrefetchScalarGridSpec(
            num_scalar_prefetch=3,
            grid=(d // MOE_TN, nt),
            in_specs=[pl.BlockSpec((MOE_TM, f), lambda j, i, te, tf, nu: (i, 0)),
                      pl.BlockSpec((1, f, MOE_TN), lambda j, i, te, tf, nu: (te[i], 0, j))],
            out_specs=pl.BlockSpec((MOE_TM, MOE_TN), lambda j, i, te, tf, nu: (i, j)),
            scratch_shapes=[pltpu.VMEM((f, MOE_TN), BF16)]),
        compiler_params=_params("arbitrary", "arbitrary"),
        name="moe_down",
    )(plan["tile_expert"], plan["tile_first"], plan["n_used"], t, w_down)


def _combine_kernel(cur_ref, nxt_ref, y_hbm, x_ref, route_ref, g_ref, o_ref, buf, sem, *, final):
    tm = x_ref.shape[0]
    slot = _pipelined_gather(cur_ref, nxt_ref, y_hbm, buf, sem, 2 * tm)
    route = route_ref[...]
    g1 = route[:, ROUTE_G1:ROUTE_G1 + 1]
    g2 = route[:, ROUTE_G2:ROUTE_G2 + 1]
    x = x_ref[...] + g1 * buf[slot, :tm, :] + g2 * buf[slot, tm:, :]
    if final:
        ms = jnp.mean(x * x, axis=-1, keepdims=True)
        x = x * lax.rsqrt(ms + RMS_EPS) * g_ref[...]
    o_ref[...] = x


def moe_combine(x, y, dest, route, final_gain):
    L, d = x.shape
    tm = COMBINE_TM
    nt = L // tm
    idx = jnp.concatenate([dest[:L].reshape(nt, 1, tm), dest[L:].reshape(nt, 1, tm)], axis=2)
    smem = lambda f: pl.BlockSpec((1, 1, 2 * tm), f, memory_space=pltpu.SMEM)
    final = final_gain is not None
    gain = (final_gain if final else jnp.ones((d,), F32)).reshape(1, d)
    kern = functools.partial(_combine_kernel, final=final)
    return pl.pallas_call(
        kern,
        out_shape=jax.ShapeDtypeStruct((L, d), F32),
        grid=(nt,),
        in_specs=[smem(lambda i: (i, 0, 0)),
                  smem(lambda i: (jnp.minimum(i + 1, nt - 1), 0, 0)),
                  pl.BlockSpec(memory_space=pl.ANY),
                  pl.BlockSpec((tm, d), lambda i: (i, 0)),
                  pl.BlockSpec((tm, LANES), lambda i: (i, 0)),
                  pl.BlockSpec((1, d), lambda i: (0, 0))],
        out_specs=pl.BlockSpec((tm, d), lambda i: (i, 0)),
        scratch_shapes=[pltpu.VMEM((2, 2 * tm, d), F32), pltpu.SemaphoreType.DMA((2,))],
        compiler_params=_params("arbitrary"),
        name="moe_combine",
    )(idx, idx, y, x, route, gain)


def moe_ffn(x, norm_g, router_w, w_gate, w_up, w_down, final_gain):
    L = x.shape[0]
    h, route = rmsnorm_router(x, norm_g, router_w)
    plan = _moe_plan(route, L)
    hs = moe_dispatch(h, plan["row_token"], plan["nt"])
    t = moe_up(hs, w_gate, w_up, plan)
    y = moe_down(t, w_down, plan)
    return moe_combine(x, y, plan["dest"], route, final_gain)


def _layer_mixers(x, l, tables, norm_mix, w_in, conv_w, conv_b, filt, hyena_skip, w_branch_a,
                  gla_gate_w2, gla_gate_b, gla_norm, w_branch_b, w_out):
    hc = (HYENA_ORDER + 1) * HYENA_WIDTH
    o_lr = hc + 2 * GLA_DK_TOTAL + 2 * GLA_DV_TOTAL
    o_g = o_lr + 2 * GLA_LOWRANK
    h = rmsnorm(x, norm_mix[l], BF16)
    proj = matmul([h], [(w_in, l, 0)], o_lr, [(0, 0)], _ep_plain, [], F32,
                  tm=1024, tn=1024, name="in_proj")
    w_gates = w_in[l, :, o_g:][None]
    gates = matmul([h], [(w_gates, 0, 0)], 2 * D_MODEL, [(0, 0)], _ep_sigmoid, [], BF16,
                   tm=1024, tn=1024, name="in_proj_gates")
    la = gla_gates(h, w_in[l, :, o_lr:o_g], gla_gate_w2[l], gla_gate_b[l])

    z_a = hyena_mixer(proj, conv_w[l], conv_b[l], filt, hyena_skip[l], tables)
    o_f, o_b = gla_scan(proj, hc, la)
    z_b = gla_post(o_f, o_b, proj, hc + 2 * GLA_DK_TOTAL + GLA_DV_TOTAL, gla_norm[l])

    mixed = matmul([z_a, z_b], [(w_branch_a, l, 0), (w_branch_b, l, 0)], D_MODEL,
                   [(0, 0), (1, 1)], _ep_merge, [(gates, 0), (gates, D_MODEL)], BF16,
                   tm=1024, tn=1024, name="branch_merge")
    return matmul([mixed], [(w_out, l, 0)], D_MODEL, [(0, 0)], _ep_residual, [(x, 0)], F32,
                  tm=1024, tn=512, name="out_proj")


def _swiglu_ffn(x, h, wg, wu, wd, i):
    f = wg.shape[2]
    t = matmul([h], [(wg, i, 0), (wu, i, 0)], f, [(0, 0), (0, 1)], _ep_swiglu, [], BF16,
               tm=1024, tn=512, name="ffn_up")
    return matmul([t], [(wd, i, 0)], wd.shape[2], [(0, 0)], _ep_residual, [(x, 0)], F32,
                  tm=512, tn=512, name="ffn_down")


def kernel(x, norm_mix, w_in, conv_w, conv_b, filt_w1, filt_b1, filt_freq1, filt_w2, filt_b2, filt_freq2, filt_w3, filt_b3, filt_freq3, filt_w4, hyena_skip, w_branch_a, gla_gate_w2, gla_gate_b, gla_norm, w_branch_b, w_out, norm_ffn, dense_w_gate, dense_w_up, dense_w_down, router_w, moe_w_gate, moe_w_up, moe_w_down, norm_final):
    b, L, d = x.shape
    assert b == 1
    x = x.reshape(L, d)
    tables = _fft_tables(L)
    for l in range(DEPTH):
        filt = (filt_w1[l], filt_b1[l], filt_freq1[l], filt_w2[l], filt_b2[l], filt_freq2[l],
                filt_w3[l], filt_b3[l], filt_freq3[l], filt_w4[l])
        x = _layer_mixers(x, l, tables, norm_mix, w_in, conv_w, conv_b, filt, hyena_skip,
                          w_branch_a, gla_gate_w2, gla_gate_b, gla_norm, w_branch_b, w_out)
        i = l // 2
        last = l == DEPTH - 1
        if l % 2 == 0:
            h = rmsnorm(x, norm_ffn[l], BF16)
            x = _swiglu_ffn(x, h, dense_w_gate, dense_w_up, dense_w_down, i)
            if last:
                x = rmsnorm(x, norm_final, F32)
        else:
            x = moe_ffn(x, norm_ffn[l], router_w[i], moe_w_gate[i], moe_w_up[i], moe_w_down[i],
                        norm_final if last else None)
    return x.reshape(b, L, d)
```

```python
import functools
import math

import jax
import jax.numpy as jnp
from jax import lax
from jax.experimental import pallas as pl
from jax.experimental.pallas import tpu as pltpu

F32 = jnp.float32
BF16 = jnp.bfloat16

D_MODEL = 2048
DEPTH = 2
HYENA_WIDTH = D_MODEL // 2
HYENA_ORDER = 2
SHORT_CONV = 3
POS_EMB_DIM = 33
POS_BANDS = (POS_EMB_DIM - 1) // 2
FILTER_HIDDEN = 64
MIN_DECAY = math.log(1e-2) / 1.5
MAX_DECAY = math.log(1e-2) / 0.3
GLA_HEADS = 4
GLA_DK = 128
GLA_DV = 256
GLA_DK_TOTAL = GLA_HEADS * GLA_DK
GLA_DV_TOTAL = GLA_HEADS * GLA_DV
GLA_LOWRANK = 16
GLA_GATE_TAU = 16.0
GLA_CHUNK = 64
N_EXPERTS = 8
RMS_EPS = 1e-6

LANES = 128
VMEM_LIMIT_BYTES = 56 * 1024 * 1024


def _params(*sem):
    return pltpu.CompilerParams(dimension_semantics=sem, vmem_limit_bytes=VMEM_LIMIT_BYTES)


def _dot(a, b):
    return jnp.dot(a, b, preferred_element_type=F32)


def _split(a):
    hi = a.astype(BF16)
    lo = (a - hi.astype(F32)).astype(BF16)
    return hi, lo


def _dot3(a, b):
    ah, al = _split(a)
    bh, bl = _split(b)
    return _dot(ah, bh) + (_dot(ah, bl) + _dot(al, bh))


def _sigmoid(x):
    return 1.0 / (1.0 + jnp.exp(-x))


def _silu(x):
    return x * _sigmoid(x)


def _log_sigmoid(x):
    return jnp.minimum(x, 0.0) - jnp.log(1.0 + jnp.exp(-jnp.abs(x)))


def _rmsnorm_kernel(x_ref, g_ref, o_ref):
    x = x_ref[...]
    ms = jnp.mean(x * x, axis=-1, keepdims=True)
    o_ref[...] = (x * lax.rsqrt(ms + RMS_EPS) * g_ref[...]).astype(o_ref.dtype)


def rmsnorm(x, g, out_dtype, tm=256):
    m, d = x.shape
    return pl.pallas_call(
        _rmsnorm_kernel,
        out_shape=jax.ShapeDtypeStruct((m, d), out_dtype),
        grid=(m // tm,),
        in_specs=[pl.BlockSpec((tm, d), lambda i: (i, 0)),
                  pl.BlockSpec((1, d), lambda i: (0, 0))],
        out_specs=pl.BlockSpec((tm, d), lambda i: (i, 0)),
        compiler_params=_params("parallel"),
        name="rmsnorm",
    )(x, g.reshape(1, d))


def _rmsnorm_router_kernel(x_ref, g_ref, rw_ref, o_ref, cw_ref):
    x = x_ref[...]
    ms = jnp.mean(x * x, axis=-1, keepdims=True)
    h = x * lax.rsqrt(ms + RMS_EPS) * g_ref[...]
    o_ref[...] = h.astype(o_ref.dtype)
    logits = _dot3(h, rw_ref[...])
    col = lax.broadcasted_iota(jnp.int32, logits.shape, 1).astype(F32)
    neg = jnp.float32(-jnp.inf)
    logits = jnp.where(col < N_EXPERTS, logits, neg)
    m1 = jnp.max(logits, axis=-1, keepdims=True)
    i1 = jnp.min(jnp.where(logits == m1, col, float(LANES)), axis=-1, keepdims=True)
    rest = jnp.where(col == i1, neg, logits)
    m2 = jnp.max(rest, axis=-1, keepdims=True)
    i2 = jnp.min(jnp.where(rest == m2, col, float(LANES)), axis=-1, keepdims=True)
    e2 = jnp.exp(m2 - m1)
    g1 = 1.0 / (1.0 + e2)
    g2 = e2 / (1.0 + e2)
    cw_ref[...] = (jnp.where(col == ROUTE_E1, i1, 0.0) + jnp.where(col == ROUTE_E2, i2, 0.0)
                   + jnp.where(col == ROUTE_G1, g1, 0.0) + jnp.where(col == ROUTE_G2, g2, 0.0))


ROUTE_E1, ROUTE_E2, ROUTE_G1, ROUTE_G2 = 0, 1, 2, 3


def rmsnorm_router(x, g, router_w, tm=256):
    m, d = x.shape
    rw = jnp.zeros((d, LANES), F32).at[:, :N_EXPERTS].set(router_w)
    return pl.pallas_call(
        _rmsnorm_router_kernel,
        out_shape=(jax.ShapeDtypeStruct((m, d), F32), jax.ShapeDtypeStruct((m, LANES), F32)),
        grid=(m // tm,),
        in_specs=[pl.BlockSpec((tm, d), lambda i: (i, 0)),
                  pl.BlockSpec((1, d), lambda i: (0, 0)),
                  pl.BlockSpec((d, LANES), lambda i: (0, 0))],
        out_specs=(pl.BlockSpec((tm, d), lambda i: (i, 0)),
                   pl.BlockSpec((tm, LANES), lambda i: (i, 0))),
        compiler_params=_params("parallel"),
        name="rmsnorm_router",
    )(x, g.reshape(1, d), rw)


def _mm_kernel(*refs, n_a, n_w, dots, n_extra, epilogue):
    a_refs = refs[:n_a]
    w_refs = refs[n_a:n_a + n_w]
    e_refs = refs[n_a + n_w:n_a + n_w + n_extra]
    o_ref = refs[n_a + n_w + n_extra]
    wb_refs = refs[n_a + n_w + n_extra + 1:]

    @pl.when(pl.program_id(1) == 0)
    def _():
        for w, wb in zip(w_refs, wb_refs):
            wb[...] = w[0].astype(BF16)

    a_vals = [a[...].astype(BF16) for a in a_refs]
    accs = [_dot(a_vals[ai], wb_refs[wi][...]) for ai, wi in dots]
    o_ref[...] = epilogue(accs, [e[...] for e in e_refs]).astype(o_ref.dtype)


def matmul(a_list, w_list, n, dots, epilogue, extras, out_dtype, *, tm, tn, name):
    m = a_list[0].shape[0]
    assert m % tm == 0 and n % tn == 0
    in_specs = [pl.BlockSpec((tm, a.shape[1]), lambda j, i: (i, 0)) for a in a_list]
    w_arrays, scratch = [], []
    for w, s, first in w_list:
        assert first % tn == 0
        w_arrays.append(w)
        in_specs.append(pl.BlockSpec((1, w.shape[1], tn),
                                     lambda j, i, s=s, off=first // tn: (s, 0, off + j)))
        scratch.append(pltpu.VMEM((w.shape[1], tn), BF16))
    extra_arrays = []
    for arr, first in extras:
        assert first % tn == 0
        extra_arrays.append(arr)
        in_specs.append(pl.BlockSpec((tm, tn), lambda j, i, off=first // tn: (i, off + j)))
    kern = functools.partial(_mm_kernel, n_a=len(a_list), n_w=len(w_list), dots=tuple(dots),
                             n_extra=len(extras), epilogue=epilogue)
    return pl.pallas_call(
        kern,
        out_shape=jax.ShapeDtypeStruct((m, n), out_dtype),
        grid=(n // tn, m // tm),
        in_specs=in_specs,
        out_specs=pl.BlockSpec((tm, tn), lambda j, i: (i, j)),
        scratch_shapes=scratch,
        compiler_params=_params("arbitrary", "arbitrary"),
        name=name,
    )(*a_list, *w_arrays, *extra_arrays)


def _ep_plain(accs, ex):
    return accs[0]


def _ep_sigmoid(accs, ex):
    return _sigmoid(accs[0])


def _ep_residual(accs, ex):
    return ex[0] + accs[0]


def _ep_merge(accs, ex):
    return ex[0].astype(F32) * accs[0] + ex[1].astype(F32) * accs[1]


def _ep_swiglu(accs, ex):
    return _silu(accs[0]) * accs[1]


def _gate_kernel(h_ref, wlr_ref, w2_ref, b_ref, o_ref):
    lr = _dot(h_ref[...], wlr_ref[...].astype(BF16))
    z = _dot3(lr, w2_ref[...]) + b_ref[...]
    o_ref[...] = _log_sigmoid(z) * (1.0 / GLA_GATE_TAU)


def gla_gates(h, w_lr, gate_w2, gate_b, tm=512):
    m, d = h.shape
    r = GLA_LOWRANK
    n = 2 * GLA_DK_TOTAL
    wlr = jnp.zeros((d, LANES), F32).at[:, :2 * r].set(w_lr)
    w2 = jnp.zeros((LANES, n), F32)
    w2 = w2.at[:r, :GLA_DK_TOTAL].set(gate_w2[0]).at[r:2 * r, GLA_DK_TOTAL:].set(gate_w2[1])
    b = gate_b.reshape(1, n)
    return pl.pallas_call(
        _gate_kernel,
        out_shape=jax.ShapeDtypeStruct((m, n), F32),
        grid=(m // tm,),
        in_specs=[pl.BlockSpec((tm, d), lambda i: (i, 0)),
                  pl.BlockSpec((d, LANES), lambda i: (0, 0)),
                  pl.BlockSpec((LANES, n), lambda i: (0, 0)),
                  pl.BlockSpec((1, n), lambda i: (0, 0))],
        out_specs=pl.BlockSpec((tm, n), lambda i: (i, 0)),
        compiler_params=_params("parallel"),
        name="gla_gates",
    )(h, wlr, w2, b)


GLA_SUB = 4


def _gla_chunk(q, k, v, la, st_ref, h, reverse):
    c = GLA_CHUNK
    row = lax.broadcasted_iota(jnp.int32, (c, c), 0)
    col = lax.broadcasted_iota(jnp.int32, (c, c), 1)
    keep = (col > row) if reverse else (col <= row)
    b = la
    rows = lax.broadcasted_iota(jnp.int32, la.shape, 0)
    step = 1
    while step < c:
        if reverse:
            b = b + jnp.where(rows < c - step, pltpu.roll(b, c - step, 0), 0.0)
        else:
            b = b + jnp.where(rows >= step, pltpu.roll(b, step, 0), 0.0)
        step *= 2
    tot = b[0:1] if reverse else b[c - 1:c]
    q_in = (q * ((GLA_DK ** -0.5) * jnp.exp(b))).astype(BF16)
    k_dec = k * jnp.exp(-b)
    k_in = k_dec.astype(BF16)
    k_out = (k_dec * jnp.exp(tot)).astype(BF16)
    vb = v.astype(BF16)
    s = lax.dot_general(q_in, k_in, (((1,), (1,)), ((), ())), preferred_element_type=F32)
    s = jnp.where(keep, s, 0.0).astype(BF16)
    st = st_ref[h]
    o = _dot(s, vb) + lax.dot_general(q_in, st.astype(BF16), (((1,), (1,)), ((), ())),
                                      preferred_element_type=F32)
    upd = lax.dot_general(vb, k_out, (((0,), (0,)), ((), ())), preferred_element_type=F32)
    st_ref[h] = st * jnp.exp(tot) + upd
    return o


def _gla_kernel(qf_ref, kf_ref, vf_ref, lf_ref, qb_ref, kb_ref, vb_ref, lb_ref,
                of_ref, ob_ref, sf_ref, sb_ref):
    @pl.when(pl.program_id(0) == 0)
    def _():
        sf_ref[...] = jnp.zeros_like(sf_ref)
        sb_ref[...] = jnp.zeros_like(sb_ref)

    def body(s, carry):
        rf = pl.multiple_of(s * GLA_CHUNK, GLA_CHUNK)
        rb = pl.multiple_of((GLA_SUB - 1 - s) * GLA_CHUNK, GLA_CHUNK)
        for h in range(GLA_HEADS):
            kk = slice(h * GLA_DK, (h + 1) * GLA_DK)
            vv = slice(h * GLA_DV, (h + 1) * GLA_DV)
            of_ref[pl.ds(rf, GLA_CHUNK), vv] = _gla_chunk(
                qf_ref[pl.ds(rf, GLA_CHUNK), kk], kf_ref[pl.ds(rf, GLA_CHUNK), kk],
                vf_ref[pl.ds(rf, GLA_CHUNK), vv], lf_ref[pl.ds(rf, GLA_CHUNK), kk],
                sf_ref, h, False)
            ob_ref[pl.ds(rb, GLA_CHUNK), vv] = _gla_chunk(
                qb_ref[pl.ds(rb, GLA_CHUNK), kk], kb_ref[pl.ds(rb, GLA_CHUNK), kk],
                vb_ref[pl.ds(rb, GLA_CHUNK), vv], lb_ref[pl.ds(rb, GLA_CHUNK), kk],
                sb_ref, h, True)
        return carry

    lax.fori_loop(0, GLA_SUB, body, 0, unroll=True)


def gla_scan(qkvr, first, la):
    L = qkvr.shape[0]
    rows = GLA_CHUNK * GLA_SUB
    n = L // rows
    dk, dv = GLA_DK_TOTAL, GLA_DV_TOTAL
    assert first % dv == 0
    qb, vb = first // dk, (first + 2 * dk) // dv
    specs = []
    for rev in (False, True):
        blk = (lambda i: n - 1 - i) if rev else (lambda i: i)
        specs += [pl.BlockSpec((rows, dk), lambda i, blk=blk: (blk(i), qb)),
                  pl.BlockSpec((rows, dk), lambda i, blk=blk: (blk(i), qb + 1)),
                  pl.BlockSpec((rows, dv), lambda i, blk=blk: (blk(i), vb)),
                  pl.BlockSpec((rows, dk), lambda i, blk=blk, c=int(rev): (blk(i), c))]
    return pl.pallas_call(
        _gla_kernel,
        out_shape=(jax.ShapeDtypeStruct((L, dv), F32), jax.ShapeDtypeStruct((L, dv), F32)),
        grid=(n,),
        in_specs=specs,
        out_specs=(pl.BlockSpec((rows, dv), lambda i: (i, 0)),
                   pl.BlockSpec((rows, dv), lambda i: (n - 1 - i, 0))),
        scratch_shapes=[pltpu.VMEM((GLA_HEADS, GLA_DV, GLA_DK), F32),
                        pltpu.VMEM((GLA_HEADS, GLA_DV, GLA_DK), F32)],
        compiler_params=_params("arbitrary"),
        name="gla_scan",
    )(qkvr, qkvr, qkvr, la, qkvr, qkvr, qkvr, la)


def _gla_post_kernel(of_ref, ob_ref, r_ref, g_ref, o_ref):
    g = g_ref[...]
    for h in range(GLA_HEADS):
        vv = slice(h * GLA_DV, (h + 1) * GLA_DV)
        o = of_ref[:, vv] + ob_ref[:, vv]
        ms = jnp.mean(o * o, axis=-1, keepdims=True)
        y = o * lax.rsqrt(ms + RMS_EPS) * g
        o_ref[:, vv] = (y * _silu(r_ref[:, vv])).astype(o_ref.dtype)


def gla_post(o_f, o_b, qkvr, first, norm_g, tm=512):
    L, dv = o_f.shape
    assert first % dv == 0
    return pl.pallas_call(
        _gla_post_kernel,
        out_shape=jax.ShapeDtypeStruct((L, dv), BF16),
        grid=(L // tm,),
        in_specs=[pl.BlockSpec((tm, dv), lambda i: (i, 0)),
                  pl.BlockSpec((tm, dv), lambda i: (i, 0)),
                  pl.BlockSpec((tm, dv), lambda i: (i, first // dv)),
                  pl.BlockSpec((1, GLA_DV), lambda i: (0, 0))],
        out_specs=pl.BlockSpec((tm, dv), lambda i: (i, 0)),
        compiler_params=_params("parallel"),
        name="gla_post",
    )(o_f, o_b, qkvr, norm_g.reshape(1, GLA_DV))


CT = 256
CT_G = 512
CT_COL = 128
MID_SLABS = 8
SUB = 8


def _fft_dims(L):
    n = 2 * L
    na = 1 << ((n.bit_length() - 1) // 2)
    nb = n // na
    assert na * nb == n and nb % SUB == 0 and (na // 2) % 8 == 0
    kp = -(-(na // 2 + 1) // 8) * 8
    return n, na, nb, kp


def _cis(num, den):
    ang = (2.0 * math.pi / den) * (num % den).astype(F32)
    return jnp.cos(ang), jnp.sin(ang)


def _expand_kernel(t_ref, o_ref):
    t = t_ref[0].astype(BF16)
    cc, cols = t.shape[1], o_ref.shape[2]
    shift, low = SUB.bit_length() - 1, SUB - 1
    src = lax.broadcasted_iota(jnp.int32, (cc, cols), 0)
    dst = lax.broadcasted_iota(jnp.int32, (cc, cols), 1)
    spread = jnp.where(lax.shift_right_logical(dst, shift) == src, 1.0, 0.0).astype(BF16)
    x = _dot(t, spread)
    row = lax.broadcasted_iota(jnp.int32, x.shape, 0)
    col = lax.broadcasted_iota(jnp.int32, x.shape, 1)
    o_ref[0] = jnp.where((col & low) == (row & low), x, 0.0).astype(o_ref.dtype)


def _expand_block_diag(t):
    g, r, c = t.shape
    return pl.pallas_call(
        _expand_kernel,
        out_shape=jax.ShapeDtypeStruct((g, r, c * SUB), BF16),
        grid=(g,),
        in_specs=[pl.BlockSpec((1, r, c), lambda i: (i, 0, 0))],
        out_specs=pl.BlockSpec((1, r, c * SUB), lambda i: (i, 0, 0)),
        compiler_params=_params("parallel"),
        name="dft_table_expand",
    )(t)


def _fft_tables(L):
    n, na, nb, kp = _fft_dims(L)
    ha, ng = na // 2, nb // SUB
    ar = lambda m: jnp.arange(m, dtype=jnp.int32)
    g_, ka_, bl_, a_ = ar(ng)[:, None, None, None], ar(kp)[None, :, None, None], \
        ar(SUB)[None, None, :, None], ar(ha)[None, None, None, :]
    c, s = _cis(ka_ * (nb * a_ + SUB * g_ + bl_), n)
    cs = jnp.stack([c, -s], axis=2)
    m8 = _expand_block_diag(cs.reshape(ng, kp * 2 * SUB, ha))
    c, s = _cis(ar(nb)[:, None] * ar(nb)[None, :], nb)
    f2 = jnp.stack([jnp.stack([c, s], axis=1), jnp.stack([-s, c], axis=1)], axis=0)
    f2 = f2.reshape(2, nb, 2, ng, SUB).transpose(0, 1, 3, 2, 4).reshape(2 * nb, 2 * nb)
    ca, sa = _cis(ar(kp)[:, None] * ar(nb)[None, :], n)
    cb, sb = _cis(ar(nb)[:, None] * ar(nb)[None, :], nb)
    ca, sa = (t.reshape(kp, ng, 1, SUB, 1, 1) / n for t in (ca, sa))
    cb, sb = (t.reshape(1, ng, 1, SUB, 1, nb) for t in (cb, sb))
    c, s = ca * cb - sa * sb, sa * cb + ca * sb
    gi = jnp.concatenate([jnp.concatenate([c, -s], axis=4),
                          jnp.concatenate([s, c], axis=4)], axis=2)
    gi = gi.reshape(kp, 2 * nb, 2 * nb)
    kc = ar(kp)[None, :]
    wgt = jnp.where((kc == 0) | (kc == ha), 1.0, jnp.where(kc < ha, 2.0, 0.0))
    c, s = _cis(ar(ha)[:, None] * kc, na)
    pm = jnp.stack([wgt * c, -wgt * s], axis=2).reshape(ha, 1, kp * 2)
    p8 = _expand_block_diag(jnp.broadcast_to(pm, (ha, SUB, kp * 2)).reshape(1, ha * SUB, kp * 2))[0]
    return m8, f2.astype(BF16), gi.astype(BF16), p8


def _short_conv_kernel(u_ref, w_ref, b_ref, o_ref):
    u = u_ref[...]
    L = u.shape[0]
    row = lax.broadcasted_iota(jnp.int32, u.shape, 0)
    prev = jnp.where(row == 0, 0.0, pltpu.roll(u, 1, 0))
    nxt = jnp.where(row == L - 1, 0.0, pltpu.roll(u, L - 1, 0))
    w = w_ref[...]
    o_ref[...] = prev * w[0:1] + u * w[1:2] + nxt * w[2:3] + b_ref[...]


def short_conv(u, w, b):
    L, c = u.shape[0], w.shape[1]
    w8 = jnp.zeros((8, c), F32).at[:SHORT_CONV].set(w)
    return pl.pallas_call(
        _short_conv_kernel,
        out_shape=jax.ShapeDtypeStruct((L, c), F32),
        grid=(c // CT_COL,),
        in_specs=[pl.BlockSpec((L, CT_COL), lambda j: (0, j)),
                  pl.BlockSpec((8, CT_COL), lambda j: (0, j)),
                  pl.BlockSpec((1, CT_COL), lambda j: (0, j))],
        out_specs=pl.BlockSpec((L, CT_COL), lambda j: (0, j)),
        compiler_params=_params("parallel"),
        name="short_conv",
    )(u, w8, b.reshape(1, c))


def _fft_s1_kernel(u_ref, m8_ref, a_ref):
    ha, sub, ct = u_ref.shape
    x = u_ref[...].reshape(ha * sub, ct).astype(BF16)
    r = _dot(m8_ref[0], x)
    a_ref[:, 0, :, :] = r.astype(a_ref.dtype).reshape(a_ref.shape[0], 2 * sub, ct)


def fft_stage1(u, ncols, m8, L):
    n, na, nb, kp = _fft_dims(L)
    ha, ng = na // 2, nb // SUB
    u3 = u.reshape(ha, nb, u.shape[1])
    return pl.pallas_call(
        _fft_s1_kernel,
        out_shape=jax.ShapeDtypeStruct((kp, ng, 2 * SUB, ncols), BF16),
        grid=(ng, ncols // CT_G),
        in_specs=[pl.BlockSpec((ha, SUB, CT_G), lambda g, j: (0, g, j)),
                  pl.BlockSpec((1, kp * 2 * SUB, ha * SUB), lambda g, j: (g, 0, 0))],
        out_specs=pl.BlockSpec((kp, 1, 2 * SUB, CT_G), lambda g, j: (0, g, 0, j)),
        compiler_params=_params("parallel", "parallel"),
        name="fft_stage1",
    )(u3, m8)


def _filter_s1_kernel(z_ref, w1_ref, b1_ref, f1_ref, w2_ref, b2_ref, f2_ref, w3_ref, b3_ref,
                      f3_ref, w4_ref, dl_ref, m8_ref, a_ref, nrm_ref, hh_ref, hl_ref, *, nb, L, cb):
    g, j = pl.program_id(0), pl.program_id(1)
    ha, sub, _ = z_ref.shape
    rows = ha * sub

    @pl.when(j == 0)
    def _():
        z = z_ref[...].reshape(rows, z_ref.shape[2])
        h = jnp.sin(f1_ref[...] * (_dot3(z, w1_ref[...]) + b1_ref[...]))
        h = jnp.sin(f2_ref[...] * (_dot3(h, w2_ref[...]) + b2_ref[...]))
        h = jnp.sin(f3_ref[...] * (_dot3(h, w3_ref[...]) + b3_ref[...]))
        hh_ref[...], hl_ref[...] = _split(h)

    hh, hl = hh_ref[...], hl_ref[...]
    wh, wl = _split(w4_ref[...])
    ct = wh.shape[1]
    k = (_dot(hh, wh) + (_dot(hh, wl) + _dot(hl, wh))).reshape(ha, sub, ct)
    a3 = lax.broadcasted_iota(jnp.int32, (ha, sub, ct), 0)
    b3 = lax.broadcasted_iota(jnp.int32, (ha, sub, ct), 1)
    tpos = a3 * nb + (b3 + g * sub)
    k = k * jnp.exp(-(tpos.astype(F32) * (1.0 / (L - 1))) * dl_ref[...])
    k = jnp.where(jnp.logical_and((j // cb) % 2 == 1, tpos == 0), 0.0, k).reshape(rows, ct)
    nrm_ref[0] = jnp.sum(jnp.abs(k), axis=0, keepdims=True)
    q = _dot(m8_ref[0], k.astype(BF16))
    a_ref[:, 0, :, :] = q.astype(a_ref.dtype).reshape(a_ref.shape[0], 2 * sub, ct)


def filter_stage1(L, w1, b1, f1, w2, b2, f2, w3, b3, f3, w4, m8):
    n, na, nb, kp = _fft_dims(L)
    ha, ng = na // 2, nb // SUB
    c = HYENA_WIDTH
    hdim = FILTER_HIDDEN
    t = jnp.linspace(0.0, 1.0, L, dtype=F32)[:, None]
    omega = 2.0 * math.pi * jnp.arange(L, dtype=F32)[:, None] / L
    bands = jnp.linspace(1e-4, POS_BANDS - 1, POS_BANDS, dtype=F32)[None, :]
    z = jnp.concatenate([t, jnp.cos(bands * omega), -jnp.sin(bands * omega)], axis=-1)
    z = jnp.zeros((L, LANES), F32).at[:, :POS_EMB_DIM].set(z).reshape(ha, nb, LANES)
    w1p = jnp.zeros((LANES, hdim), F32).at[:POS_EMB_DIM].set(w1)
    deltas = jnp.abs(jnp.linspace(MIN_DECAY, MAX_DECAY, c, dtype=F32)).reshape(1, c)
    nc = w4.shape[1]
    cb = c // CT_G
    row = lambda v: v.reshape(1, hdim)
    full = lambda shape: pl.BlockSpec(shape, lambda g, j: (0,) * len(shape))
    kern = functools.partial(_filter_s1_kernel, nb=nb, L=L, cb=cb)
    return pl.pallas_call(
        kern,
        out_shape=(jax.ShapeDtypeStruct((kp, ng, 2 * SUB, nc), BF16),
                   jax.ShapeDtypeStruct((ng, 1, nc), F32)),
        grid=(ng, nc // CT_G),
        in_specs=[pl.BlockSpec((ha, SUB, LANES), lambda g, j: (0, g, 0)),
                  full((LANES, hdim)), full((1, hdim)), full((1, hdim)),
                  full((hdim, hdim)), full((1, hdim)), full((1, hdim)),
                  full((hdim, hdim)), full((1, hdim)), full((1, hdim)),
                  pl.BlockSpec((hdim, CT_G), lambda g, j: (0, j)),
                  pl.BlockSpec((1, CT_G), lambda g, j: (0, j % cb)),
                  pl.BlockSpec((1, kp * 2 * SUB, ha * SUB), lambda g, j: (g, 0, 0))],
        out_specs=(pl.BlockSpec((kp, 1, 2 * SUB, CT_G), lambda g, j: (0, g, 0, j)),
                   pl.BlockSpec((1, 1, CT_G), lambda g, j: (g, 0, j))),
        scratch_shapes=[pltpu.VMEM((ha * SUB, hdim), BF16), pltpu.VMEM((ha * SUB, hdim), BF16)],
        compiler_params=_params("arbitrary", "arbitrary"),
        name="hyena_filter_stage1",
    )(z, w1p, row(b1), row(f1), w2, row(b2), row(f2), w3, row(b3), row(f3), w4, deltas, m8)


def _slab(ref, s):
    _, ng, rows, ct = ref.shape
    return ref[s].reshape(ng * rows, ct)


def _filter_spec_kernel(af_ref, ab_ref, f2_ref, nf_ref, nbk_ref, h_ref):
    f2 = f2_ref[...]
    nb = f2.shape[0] // 2
    inv = 1.0 / (jnp.sum(nf_ref[...], axis=0) + jnp.sum(nbk_ref[...], axis=0))
    for s in range(MID_SLABS):
        xf = _dot(f2, _slab(af_ref, s))
        xb = _dot(f2, _slab(ab_ref, s))
        h_ref[0, s] = ((xf[:nb] + xb[:nb]) * inv).astype(h_ref.dtype)
        h_ref[1, s] = ((xf[nb:] - xb[nb:]) * inv).astype(h_ref.dtype)


def filter_spectrum(a, nrm, f2, L):
    n, na, nb, kp = _fft_dims(L)
    ng = nb // SUB
    c = HYENA_WIDTH
    cb = c // CT
    fcol = lambda j: (j // cb) * 2 * cb + j % cb
    return pl.pallas_call(
        _filter_spec_kernel,
        out_shape=jax.ShapeDtypeStruct((2, kp, nb, HYENA_ORDER * c), BF16),
        grid=(HYENA_ORDER * cb, kp // MID_SLABS),
        in_specs=[pl.BlockSpec((MID_SLABS, ng, 2 * SUB, CT), lambda j, k: (k, 0, 0, fcol(j))),
                  pl.BlockSpec((MID_SLABS, ng, 2 * SUB, CT), lambda j, k: (k, 0, 0, fcol(j) + cb)),
                  pl.BlockSpec((2 * nb, 2 * nb), lambda j, k: (0, 0)),
                  pl.BlockSpec((ng, 1, CT), lambda j, k: (0, 0, fcol(j))),
                  pl.BlockSpec((ng, 1, CT), lambda j, k: (0, 0, fcol(j) + cb))],
        out_specs=pl.BlockSpec((2, MID_SLABS, nb, CT), lambda j, k: (0, k, 0, j)),
        compiler_params=_params("parallel", "parallel"),
        name="hyena_filter_spectrum",
    )(a, a, f2, nrm, nrm)


def _fft_mid_kernel(a_ref, h_ref, f2_ref, g_ref, q_ref):
    f2 = f2_ref[...]
    nb = f2.shape[0] // 2
    _, ng, rows, ct = q_ref.shape
    for s in range(MID_SLABS):
        x = _dot(f2, _slab(a_ref, s))
        xr, xi = x[:nb], x[nb:]
        hr, hi = h_ref[0, s].astype(F32), h_ref[1, s].astype(F32)
        y = jnp.concatenate([xr * hr - xi * hi, xr * hi + xi * hr], axis=0).astype(BF16)
        q_ref[s] = _dot(g_ref[s], y).astype(q_ref.dtype).reshape(ng, rows, ct)


def fft_mid(a, hf, order, f2, g, L):
    n, na, nb, kp = _fft_dims(L)
    ng = nb // SUB
    c = a.shape[3]
    cb = c // CT
    return pl.pallas_call(
        _fft_mid_kernel,
        out_shape=jax.ShapeDtypeStruct(a.shape, BF16),
        grid=(kp // MID_SLABS, cb),
        in_specs=[pl.BlockSpec((MID_SLABS, ng, 2 * SUB, CT), lambda k, j: (k, 0, 0, j)),
                  pl.BlockSpec((2, MID_SLABS, nb, CT), lambda k, j: (0, k, 0, order * cb + j)),
                  pl.BlockSpec((2 * nb, 2 * nb), lambda k, j: (0, 0)),
                  pl.BlockSpec((MID_SLABS, 2 * nb, 2 * nb), lambda k, j: (k, 0, 0))],
        out_specs=pl.BlockSpec((MID_SLABS, ng, 2 * SUB, CT), lambda k, j: (k, 0, 0, j)),
        compiler_params=_params("parallel", "parallel"),
        name="fft_mid",
    )(a, hf, f2, g)


def _fft_i2_kernel(q_ref, p8_ref, u_ref, x_ref, s_ref, o_ref):
    kp, _, rows, ct = q_ref.shape
    q = q_ref[:, 0, :, :].reshape(kp * rows, ct)
    y = _dot(p8_ref[...], q).reshape(o_ref.shape)
    o_ref[...] = x_ref[...] * (y + u_ref[...] * s_ref[...])


def fft_stage_out(q, u, gate_src, gate_blk, skip, p8, L):
    n, na, nb, kp = _fft_dims(L)
    ha, ng = na // 2, nb // SUB
    c = q.shape[3]
    u3 = u.reshape(ha, nb, u.shape[1])
    g3 = gate_src.reshape(ha, nb, gate_src.shape[1])
    out = pl.pallas_call(
        _fft_i2_kernel,
        out_shape=jax.ShapeDtypeStruct((ha, nb, c), F32),
        grid=(ng, c // CT_G),
        in_specs=[pl.BlockSpec((kp, 1, 2 * SUB, CT_G), lambda g, j: (0, g, 0, j)),
                  pl.BlockSpec((ha * SUB, kp * 2 * SUB), lambda g, j: (0, 0)),
                  pl.BlockSpec((ha, SUB, CT_G), lambda g, j: (0, g, j)),
                  pl.BlockSpec((ha, SUB, CT_G), lambda g, j: (0, g, gate_blk + j)),
                  pl.BlockSpec((1, 1, CT_G), lambda g, j: (0, 0, j))],
        out_specs=pl.BlockSpec((ha, SUB, CT_G), lambda g, j: (0, g, j)),
        compiler_params=_params("parallel", "parallel"),
        name="fft_stage_out",
    )(q, p8, u3, g3, skip.reshape(1, 1, c))
    return out.reshape(L, c)


def hyena_mixer(u_hy, conv_w, conv_b, filt, skip, tables):
    L = u_hy.shape[0]
    c = HYENA_WIDTH
    m8, f2, g, p8 = tables
    uc = short_conv(u_hy, conv_w, conv_b)
    a_filt, nrm = filter_stage1(L, *filt, m8)
    hf = filter_spectrum(a_filt, nrm, f2, L)
    z = uc
    for order in range(HYENA_ORDER):
        a = fft_stage1(z, c, m8, L)
        q = fft_mid(a, hf, order, f2, g, L)
        z = fft_stage_out(q, z, uc, (order + 1) * (c // CT_G), skip[order], p8, L)
    return z


MOE_TM = 512
MOE_TN = 512
COMBINE_TM = 256


def _moe_plan(route, L):
    tm = MOE_TM
    nt = -(-(2 * L + N_EXPERTS * (tm - 1)) // tm)
    e_flat = jnp.concatenate([route[:, ROUTE_E1], route[:, ROUTE_E2]]).astype(jnp.int32)
    onehot = (e_flat[:, None] == jnp.arange(N_EXPERTS, dtype=jnp.int32)[None, :]).astype(jnp.int32)
    csum = jnp.cumsum(onehot, axis=0)
    rank = jnp.sum(onehot * (csum - 1), axis=1)
    counts = csum[-1]
    padded = ((counts + tm - 1) // tm) * tm
    ends = jnp.cumsum(padded)
    dest = (ends - padded)[e_flat] + rank
    tok = jnp.tile(jnp.arange(L, dtype=jnp.int32), 2)
    row_token = jnp.zeros((nt * tm,), jnp.int32).at[dest].set(tok, unique_indices=True)
    n_used = ends[-1] // tm
    tile_row = jnp.arange(nt, dtype=jnp.int32) * tm
    tile_expert = jnp.sum((tile_row[:, None] >= ends[None, :]).astype(jnp.int32), axis=1)
    last = jnp.take(tile_expert, jnp.maximum(n_used - 1, 0))
    tile_expert = jnp.where(jnp.arange(nt) < n_used, tile_expert, last)
    tile_first = jnp.concatenate([jnp.ones((1,), jnp.int32),
                                  (tile_expert[1:] != tile_expert[:-1]).astype(jnp.int32)])
    tile_group = jnp.cumsum(tile_first) - 1
    group_expert = jnp.zeros((N_EXPERTS,), jnp.int32).at[tile_group].set(tile_expert)
    n_groups = jnp.take(tile_group, jnp.maximum(n_used - 1, 0)) + 1
    i32 = lambda v: v.reshape(1).astype(jnp.int32)
    return dict(nt=nt, row_token=row_token, dest=dest,
                scalars=(tile_expert, tile_first, i32(n_used), tile_group.astype(jnp.int32),
                         i32(n_groups), group_expert))


def _row_copy(src_hbm, idx, buf, slot, r, sem):
    return pltpu.make_async_copy(src_hbm.at[pl.ds(idx, 1)], buf.at[slot, pl.ds(r, 1)],
                                 sem.at[slot])


def _gather_rows(idx_ref, src_hbm, buf, slot, sem, n_rows, start):
    if not start:
        pltpu.make_async_copy(src_hbm.at[pl.ds(0, n_rows)], buf.at[slot], sem.at[slot]).wait()
        return

    def body(r, carry):
        _row_copy(src_hbm, idx_ref[0, 0, r], buf, slot, r, sem).start()
        return carry
    lax.fori_loop(0, n_rows, body, 0, unroll=8)


def _pipelined_gather(cur_ref, nxt_ref, src_hbm, buf, sem, n_rows):
    i, n = pl.program_id(0), pl.num_programs(0)
    slot = lax.rem(i, 2)

    @pl.when(i == 0)
    def _():
        _gather_rows(cur_ref, src_hbm, buf, 0, sem, n_rows, True)

    @pl.when(i + 1 < n)
    def _():
        _gather_rows(nxt_ref, src_hbm, buf, 1 - slot, sem, n_rows, True)

    _gather_rows(cur_ref, src_hbm, buf, slot, sem, n_rows, False)
    return slot


def _dispatch_kernel(cur_ref, nxt_ref, h_hbm, o_ref, buf, sem):
    slot = _pipelined_gather(cur_ref, nxt_ref, h_hbm, buf, sem, MOE_TM)
    o_ref[...] = buf[slot].astype(o_ref.dtype)


def moe_dispatch(h, row_token, nt):
    L, d = h.shape
    idx = row_token.reshape(nt, 1, MOE_TM)
    smem = lambda f: pl.BlockSpec((1, 1, MOE_TM), f, memory_space=pltpu.SMEM)
    return pl.pallas_call(
        _dispatch_kernel,
        out_shape=jax.ShapeDtypeStruct((nt * MOE_TM, d), BF16),
        grid=(nt,),
        in_specs=[smem(lambda i: (i, 0, 0)),
                  smem(lambda i: (jnp.minimum(i + 1, nt - 1), 0, 0)),
                  pl.BlockSpec(memory_space=pl.ANY)],
        out_specs=pl.BlockSpec((MOE_TM, d), lambda i: (i, 0)),
        scratch_shapes=[pltpu.VMEM((2, MOE_TM, d), F32), pltpu.SemaphoreType.DMA((2,))],
        compiler_params=_params("arbitrary"),
        name="moe_dispatch",
    )(idx, idx, h)


def _weight_copy(w_hbm, e, jcol, wbuf, k, slot, sem):
    tn = wbuf.shape[3]
    return pltpu.make_async_copy(w_hbm.at[e, :, pl.ds(pl.multiple_of(jcol * tn, tn), tn)],
                                 wbuf.at[k, slot], sem.at[k, slot])


def _refresh_expert_weights(scalars, w_hbms, wbuf, wbs, sem):
    te_ref, tf_ref, _, tg_ref, ng_ref, ge_ref = scalars
    j, i = pl.program_id(0), pl.program_id(1)
    nj = pl.num_programs(0)

    @pl.when(tf_ref[i] == 1)
    def _():
        g, ng = tg_ref[i], ng_ref[0]
        blk = j * ng + g
        slot = lax.rem(blk, 2)

        @pl.when(blk == 0)
        def _():
            for k, w in enumerate(w_hbms):
                _weight_copy(w, te_ref[i], j, wbuf, k, 0, sem).start()

        for k, w in enumerate(w_hbms):
            _weight_copy(w, 0, 0, wbuf, k, slot, sem).wait()
        for k, wb in enumerate(wbs):
            wb[...] = wbuf[k, slot].astype(BF16)

        wrap = g + 1 == ng
        nxt_e = ge_ref[jnp.where(wrap, 0, g + 1)]
        nxt_j = j + wrap.astype(jnp.int32)

        @pl.when(nxt_j < nj)
        def _():
            for k, w in enumerate(w_hbms):
                _weight_copy(w, nxt_e, nxt_j, wbuf, k, 1 - slot, sem).start()


def _moe_up_kernel(te_ref, tf_ref, nu_ref, tg_ref, ng_ref, ge_ref, a_ref, wg_hbm, wu_hbm, o_ref,
                   wbuf, wgb, wub, sem):
    i = pl.program_id(1)
    _refresh_expert_weights((te_ref, tf_ref, nu_ref, tg_ref, ng_ref, ge_ref),
                            (wg_hbm, wu_hbm), wbuf, (wgb, wub), sem)

    @pl.when(i < nu_ref[0])
    def _():
        a = a_ref[...]
        o_ref[...] = (_silu(_dot(a, wgb[...])) * _dot(a, wub[...])).astype(o_ref.dtype)

    @pl.when(i >= nu_ref[0])
    def _():
        o_ref[...] = jnp.zeros_like(o_ref)


def moe_up(hs, w_gate, w_up, plan):
    r, d = hs.shape
    f = w_gate.shape[2]
    nt = plan["nt"]
    anyspec = pl.BlockSpec(memory_space=pl.ANY)
    return pl.pallas_call(
        _moe_up_kernel,
        out_shape=jax.ShapeDtypeStruct((r, f), BF16),
        grid_spec=pltpu.PrefetchScalarGridSpec(
            num_scalar_prefetch=6,
            grid=(f // MOE_TN, nt),
            in_specs=[pl.BlockSpec((MOE_TM, d), lambda j, i, *_: (i, 0)), anyspec, anyspec],
            out_specs=pl.BlockSpec((MOE_TM, MOE_TN), lambda j, i, *_: (i, j)),
            scratch_shapes=[pltpu.VMEM((2, 2, d, MOE_TN), F32),
                            pltpu.VMEM((d, MOE_TN), BF16), pltpu.VMEM((d, MOE_TN), BF16),
                            pltpu.SemaphoreType.DMA((2, 2))]),
        compiler_params=_params("arbitrary", "arbitrary"),
        name="moe_up",
    )(*plan["scalars"], hs, w_gate, w_up)


def _moe_down_kernel(te_ref, tf_ref, nu_ref, tg_ref, ng_ref, ge_ref, t_ref, wd_hbm, o_ref,
                     wbuf, wdb, sem):
    i = pl.program_id(1)
    _refresh_expert_weights((te_ref, tf_ref, nu_ref, tg_ref, ng_ref, ge_ref),
                            (wd_hbm,), wbuf, (wdb,), sem)

    @pl.when(i < nu_ref[0])
    def _():
        o_ref[...] = _dot(t_ref[...], wdb[...])

    @pl.when(i >= nu_ref[0])
    def _():
        o_ref[...] = jnp.zeros_like(o_ref)


def moe_down(t, w_down, plan):
    r, f = t.shape
    d = w_down.shape[2]
    nt = plan["nt"]
    return pl.pallas_call(
        _moe_down_kernel,
        out_shape=jax.ShapeDtypeStruct((r, d), F32),
        grid_spec=pltpu.PrefetchScalarGridSpec(
            num_scalar_prefetch=6,
            grid=(d // MOE_TN, nt),
            in_specs=[pl.BlockSpec((MOE_TM, f), lambda j, i, *_: (i, 0)),
                      pl.BlockSpec(memory_space=pl.ANY)],
            out_specs=pl.BlockSpec((MOE_TM, MOE_TN), lambda j, i, *_: (i, j)),
            scratch_shapes=[pltpu.VMEM((1, 2, f, MOE_TN), F32), pltpu.VMEM((f, MOE_TN), BF16),
                            pltpu.SemaphoreType.DMA((1, 2))]),
        compiler_params=_params("arbitrary", "arbitrary"),
        name="moe_down",
    )(*plan["scalars"], t, w_down)


def _combine_kernel(cur_ref, nxt_ref, y_hbm, x_ref, route_ref, g_ref, o_ref, buf, sem, *, final):
    tm = x_ref.shape[0]
    slot = _pipelined_gather(cur_ref, nxt_ref, y_hbm, buf, sem, 2 * tm)
    route = route_ref[...]
    g1 = route[:, ROUTE_G1:ROUTE_G1 + 1]
    g2 = route[:, ROUTE_G2:ROUTE_G2 + 1]
    x = x_ref[...] + g1 * buf[slot, :tm, :] + g2 * buf[slot, tm:, :]
    if final:
        ms = jnp.mean(x * x, axis=-1, keepdims=True)
        x = x * lax.rsqrt(ms + RMS_EPS) * g_ref[...]
    o_ref[...] = x


def moe_combine(x, y, dest, route, final_gain):
    L, d = x.shape
    tm = COMBINE_TM
    nt = L // tm
    idx = jnp.concatenate([dest[:L].reshape(nt, 1, tm), dest[L:].reshape(nt, 1, tm)], axis=2)
    smem = lambda f: pl.BlockSpec((1, 1, 2 * tm), f, memory_space=pltpu.SMEM)
    final = final_gain is not None
    gain = (final_gain if final else jnp.ones((d,), F32)).reshape(1, d)
    kern = functools.partial(_combine_kernel, final=final)
    return pl.pallas_call(
        kern,
        out_shape=jax.ShapeDtypeStruct((L, d), F32),
        grid=(nt,),
        in_specs=[smem(lambda i: (i, 0, 0)),
                  smem(lambda i: (jnp.minimum(i + 1, nt - 1), 0, 0)),
                  pl.BlockSpec(memory_space=pl.ANY),
                  pl.BlockSpec((tm, d), lambda i: (i, 0)),
                  pl.BlockSpec((tm, LANES), lambda i: (i, 0)),
                  pl.BlockSpec((1, d), lambda i: (0, 0))],
        out_specs=pl.BlockSpec((tm, d), lambda i: (i, 0)),
        scratch_shapes=[pltpu.VMEM((2, 2 * tm, d), F32), pltpu.SemaphoreType.DMA((2,))],
        compiler_params=_params("arbitrary"),
        name="moe_combine",
    )(idx, idx, y, x, route, gain)


def moe_ffn(x, norm_g, router_w, w_gate, w_up, w_down, final_gain):
    L = x.shape[0]
    h, route = rmsnorm_router(x, norm_g, router_w)
    plan = _moe_plan(route, L)
    hs = moe_dispatch(h, plan["row_token"], plan["nt"])
    t = moe_up(hs, w_gate, w_up, plan)
    y = moe_down(t, w_down, plan)
    return moe_combine(x, y, plan["dest"], route, final_gain)


def _layer_mixers(x, l, tables, norm_mix, w_in, conv_w, conv_b, filt, hyena_skip, w_branch_a,
                  gla_gate_w2, gla_gate_b, gla_norm, w_branch_b, w_out):
    hc = (HYENA_ORDER + 1) * HYENA_WIDTH
    o_lr = hc + 2 * GLA_DK_TOTAL + 2 * GLA_DV_TOTAL
    o_g = o_lr + 2 * GLA_LOWRANK
    h = rmsnorm(x, norm_mix[l], BF16)
    proj = matmul([h], [(w_in[l, :, :o_lr][None], 0, 0)], o_lr, [(0, 0)], _ep_plain, [], F32,
                  tm=1024, tn=1024, name="in_proj")
    w_gates = w_in[l, :, o_g:][None]
    gates = matmul([h], [(w_gates, 0, 0)], 2 * D_MODEL, [(0, 0)], _ep_sigmoid, [], BF16,
                   tm=1024, tn=1024, name="in_proj_gates")
    la = gla_gates(h, w_in[l, :, o_lr:o_g], gla_gate_w2[l], gla_gate_b[l])

    z_a = hyena_mixer(proj, conv_w[l], conv_b[l], filt, hyena_skip[l], tables)
    o_f, o_b = gla_scan(proj, hc, la)
    z_b = gla_post(o_f, o_b, proj, hc + 2 * GLA_DK_TOTAL + GLA_DV_TOTAL, gla_norm[l])

    mixed = matmul([z_a, z_b], [(w_branch_a, l, 0), (w_branch_b, l, 0)], D_MODEL,
                   [(0, 0), (1, 1)], _ep_merge, [(gates, 0), (gates, D_MODEL)], BF16,
                   tm=1024, tn=1024, name="branch_merge")
    return matmul([mixed], [(w_out, l, 0)], D_MODEL, [(0, 0)], _ep_residual, [(x, 0)], F32,
                  tm=1024, tn=512, name="out_proj")


def _swiglu_ffn(x, h, wg, wu, wd, i):
    f = wg.shape[2]
    t = matmul([h], [(wg, i, 0), (wu, i, 0)], f, [(0, 0), (0, 1)], _ep_swiglu, [], BF16,
               tm=1024, tn=512, name="ffn_up")
    return matmul([t], [(wd, i, 0)], wd.shape[2], [(0, 0)], _ep_residual, [(x, 0)], F32,
                  tm=512, tn=512, name="ffn_down")


def kernel(x, norm_mix, w_in, conv_w, conv_b, filt_w1, filt_b1, filt_freq1, filt_w2, filt_b2, filt_freq2, filt_w3, filt_b3, filt_freq3, filt_w4, hyena_skip, w_branch_a, gla_gate_w2, gla_gate_b, gla_norm, w_branch_b, w_out, norm_ffn, dense_w_gate, dense_w_up, dense_w_down, router_w, moe_w_gate, moe_w_up, moe_w_down, norm_final):
    b, L, d = x.shape
    assert b == 1
    x = x.reshape(L, d)
    tables = _fft_tables(L)
    for l in range(DEPTH):
        filt = (filt_w1[l], filt_b1[l], filt_freq1[l], filt_w2[l], filt_b2[l], filt_freq2[l],
                filt_w3[l], filt_b3[l], filt_freq3[l], filt_w4[l])
        x = _layer_mixers(x, l, tables, norm_mix, w_in, conv_w, conv_b, filt, hyena_skip,
                          w_branch_a, gla_gate_w2, gla_gate_b, gla_norm, w_branch_b, w_out)
        i = l // 2
        last = l == DEPTH - 1
        if l % 2 == 0:
            h = rmsnorm(x, norm_ffn[l], BF16)
            x = _swiglu_ffn(x, h, dense_w_gate, dense_w_up, dense_w_down, i)
            if last:
                x = rmsnorm(x, norm_final, F32)
        else:
            x = moe_ffn(x, norm_ffn[l], router_w[i], moe_w_gate[i], moe_w_up[i], moe_w_down[i],
                        norm_final if last else None)
    return x.reshape(b, L, d)
```

```python
import functools
import math

import jax
import jax.numpy as jnp
from jax import lax
from jax.experimental import pallas as pl
from jax.experimental.pallas import tpu as pltpu

F32 = jnp.float32
BF16 = jnp.bfloat16

D_MODEL = 2048
DEPTH = 2
HYENA_WIDTH = D_MODEL // 2
HYENA_ORDER = 2
SHORT_CONV = 3
POS_EMB_DIM = 33
POS_BANDS = (POS_EMB_DIM - 1) // 2
FILTER_HIDDEN = 64
MIN_DECAY = math.log(1e-2) / 1.5
MAX_DECAY = math.log(1e-2) / 0.3
GLA_HEADS = 4
GLA_DK = 128
GLA_DV = 256
GLA_DK_TOTAL = GLA_HEADS * GLA_DK
GLA_DV_TOTAL = GLA_HEADS * GLA_DV
GLA_LOWRANK = 16
GLA_GATE_TAU = 16.0
GLA_CHUNK = 64
N_EXPERTS = 8
RMS_EPS = 1e-6

LANES = 128
VMEM_LIMIT_BYTES = 56 * 1024 * 1024


def _params(*sem):
    return pltpu.CompilerParams(dimension_semantics=sem, vmem_limit_bytes=VMEM_LIMIT_BYTES)


def _dot(a, b):
    return jnp.dot(a, b, preferred_element_type=F32)


def _split(a):
    hi = a.astype(BF16)
    lo = (a - hi.astype(F32)).astype(BF16)
    return hi, lo


def _dot3(a, b):
    ah, al = _split(a)
    bh, bl = _split(b)
    return _dot(ah, bh) + (_dot(ah, bl) + _dot(al, bh))


def _sigmoid(x):
    return 1.0 / (1.0 + jnp.exp(-x))


def _silu(x):
    return x * _sigmoid(x)


def _log_sigmoid(x):
    return jnp.minimum(x, 0.0) - jnp.log(1.0 + jnp.exp(-jnp.abs(x)))


def _rmsnorm_kernel(x_ref, g_ref, o_ref):
    x = x_ref[...]
    ms = jnp.mean(x * x, axis=-1, keepdims=True)
    o_ref[...] = (x * lax.rsqrt(ms + RMS_EPS) * g_ref[...]).astype(o_ref.dtype)


def rmsnorm(x, g, out_dtype, tm=256):
    m, d = x.shape
    return pl.pallas_call(
        _rmsnorm_kernel,
        out_shape=jax.ShapeDtypeStruct((m, d), out_dtype),
        grid=(m // tm,),
        in_specs=[pl.BlockSpec((tm, d), lambda i: (i, 0)),
                  pl.BlockSpec((1, d), lambda i: (0, 0))],
        out_specs=pl.BlockSpec((tm, d), lambda i: (i, 0)),
        compiler_params=_params("parallel"),
        name="rmsnorm",
    )(x, g.reshape(1, d))


def _rmsnorm_router_kernel(x_ref, g_ref, rw_ref, o_ref, cw_ref):
    x = x_ref[...]
    ms = jnp.mean(x * x, axis=-1, keepdims=True)
    h = x * lax.rsqrt(ms + RMS_EPS) * g_ref[...]
    o_ref[...] = h.astype(o_ref.dtype)
    logits = _dot3(h, rw_ref[...])
    col = lax.broadcasted_iota(jnp.int32, logits.shape, 1).astype(F32)
    neg = jnp.float32(-jnp.inf)
    logits = jnp.where(col < N_EXPERTS, logits, neg)
    m1 = jnp.max(logits, axis=-1, keepdims=True)
    i1 = jnp.min(jnp.where(logits == m1, col, float(LANES)), axis=-1, keepdims=True)
    rest = jnp.where(col == i1, neg, logits)
    m2 = jnp.max(rest, axis=-1, keepdims=True)
    i2 = jnp.min(jnp.where(rest == m2, col, float(LANES)), axis=-1, keepdims=True)
    e2 = jnp.exp(m2 - m1)
    g1 = 1.0 / (1.0 + e2)
    g2 = e2 / (1.0 + e2)
    cw_ref[...] = (jnp.where(col == ROUTE_E1, i1, 0.0) + jnp.where(col == ROUTE_E2, i2, 0.0)
                   + jnp.where(col == ROUTE_G1, g1, 0.0) + jnp.where(col == ROUTE_G2, g2, 0.0))


ROUTE_E1, ROUTE_E2, ROUTE_G1, ROUTE_G2 = 0, 1, 2, 3


def rmsnorm_router(x, g, router_w, tm=256):
    m, d = x.shape
    rw = jnp.zeros((d, LANES), F32).at[:, :N_EXPERTS].set(router_w)
    return pl.pallas_call(
        _rmsnorm_router_kernel,
        out_shape=(jax.ShapeDtypeStruct((m, d), F32), jax.ShapeDtypeStruct((m, LANES), F32)),
        grid=(m // tm,),
        in_specs=[pl.BlockSpec((tm, d), lambda i: (i, 0)),
                  pl.BlockSpec((1, d), lambda i: (0, 0)),
                  pl.BlockSpec((d, LANES), lambda i: (0, 0))],
        out_specs=(pl.BlockSpec((tm, d), lambda i: (i, 0)),
                   pl.BlockSpec((tm, LANES), lambda i: (i, 0))),
        compiler_params=_params("parallel"),
        name="rmsnorm_router",
    )(x, g.reshape(1, d), rw)


def _mm_kernel(*refs, n_a, n_w, dots, n_extra, epilogue, w_transposed):
    a_refs = refs[:n_a]
    w_refs = refs[n_a:n_a + n_w]
    e_refs = refs[n_a + n_w:n_a + n_w + n_extra]
    o_ref = refs[n_a + n_w + n_extra]
    wb_refs = refs[n_a + n_w + n_extra + 1:]

    @pl.when(pl.program_id(1) == 0)
    def _():
        for w, wb in zip(w_refs, wb_refs):
            wb[...] = w[0].astype(BF16)

    a_vals = [a[...].astype(BF16) for a in a_refs]
    nt_dims = (((1,), (1,)), ((), ()))
    accs = [lax.dot_general(a_vals[ai], wb_refs[wi][...], nt_dims, preferred_element_type=F32)
            if w_transposed else _dot(a_vals[ai], wb_refs[wi][...]) for ai, wi in dots]
    o_ref[...] = epilogue(accs, [e[...] for e in e_refs]).astype(o_ref.dtype)


def matmul(a_list, w_list, n, dots, epilogue, extras, out_dtype, *, tm, tn, name,
           w_transposed=False):
    m = a_list[0].shape[0]
    assert m % tm == 0 and n % tn == 0
    in_specs = [pl.BlockSpec((tm, a.shape[1]), lambda j, i: (i, 0)) for a in a_list]
    w_arrays, scratch = [], []
    for w, s, first in w_list:
        assert first % tn == 0
        w_arrays.append(w)
        if w_transposed:
            in_specs.append(pl.BlockSpec((1, tn, w.shape[2]),
                                         lambda j, i, s=s, off=first // tn: (s, off + j, 0)))
            scratch.append(pltpu.VMEM((tn, w.shape[2]), BF16))
            continue
        in_specs.append(pl.BlockSpec((1, w.shape[1], tn),
                                     lambda j, i, s=s, off=first // tn: (s, 0, off + j)))
        scratch.append(pltpu.VMEM((w.shape[1], tn), BF16))
    extra_arrays = []
    for arr, first in extras:
        assert first % tn == 0
        extra_arrays.append(arr)
        in_specs.append(pl.BlockSpec((tm, tn), lambda j, i, off=first // tn: (i, off + j)))
    kern = functools.partial(_mm_kernel, n_a=len(a_list), n_w=len(w_list), dots=tuple(dots),
                             n_extra=len(extras), epilogue=epilogue, w_transposed=w_transposed)
    return pl.pallas_call(
        kern,
        out_shape=jax.ShapeDtypeStruct((m, n), out_dtype),
        grid=(n // tn, m // tm),
        in_specs=in_specs,
        out_specs=pl.BlockSpec((tm, tn), lambda j, i: (i, j)),
        scratch_shapes=scratch,
        compiler_params=_params("arbitrary", "arbitrary"),
        name=name,
    )(*a_list, *w_arrays, *extra_arrays)


def _ep_plain(accs, ex):
    return accs[0]


def _ep_sigmoid(accs, ex):
    return _sigmoid(accs[0])


def _ep_residual(accs, ex):
    return ex[0] + accs[0]


def _ep_merge(accs, ex):
    return ex[0].astype(F32) * accs[0] + ex[1].astype(F32) * accs[1]


def _ep_swiglu(accs, ex):
    return _silu(accs[0]) * accs[1]


def _gate_kernel(h_ref, wlr_ref, w2_ref, b_ref, o_ref):
    lr = lax.dot_general(h_ref[...], wlr_ref[...].astype(BF16), (((1,), (1,)), ((), ())),
                         preferred_element_type=F32)
    z = _dot3(lr, w2_ref[...]) + b_ref[...]
    o_ref[...] = _log_sigmoid(z) * (1.0 / GLA_GATE_TAU)


def gla_gates(h, w_lr_t, gate_w2, gate_b, tm=512):
    m, d = h.shape
    r = GLA_LOWRANK
    n = 2 * GLA_DK_TOTAL
    wlr = jnp.zeros((LANES, d), F32).at[:2 * r].set(w_lr_t)
    w2 = jnp.zeros((LANES, n), F32)
    w2 = w2.at[:r, :GLA_DK_TOTAL].set(gate_w2[0]).at[r:2 * r, GLA_DK_TOTAL:].set(gate_w2[1])
    b = gate_b.reshape(1, n)
    return pl.pallas_call(
        _gate_kernel,
        out_shape=jax.ShapeDtypeStruct((m, n), F32),
        grid=(m // tm,),
        in_specs=[pl.BlockSpec((tm, d), lambda i: (i, 0)),
                  pl.BlockSpec((LANES, d), lambda i: (0, 0)),
                  pl.BlockSpec((LANES, n), lambda i: (0, 0)),
                  pl.BlockSpec((1, n), lambda i: (0, 0))],
        out_specs=pl.BlockSpec((tm, n), lambda i: (i, 0)),
        compiler_params=_params("parallel"),
        name="gla_gates",
    )(h, wlr, w2, b)


GLA_SUB = 4


def _gla_chunk(q, k, v, la, st_ref, h, reverse):
    c = GLA_CHUNK
    row = lax.broadcasted_iota(jnp.int32, (c, c), 0)
    col = lax.broadcasted_iota(jnp.int32, (c, c), 1)
    keep = (col > row) if reverse else (col <= row)
    b = la
    rows = lax.broadcasted_iota(jnp.int32, la.shape, 0)
    step = 1
    while step < c:
        if reverse:
            b = b + jnp.where(rows < c - step, pltpu.roll(b, c - step, 0), 0.0)
        else:
            b = b + jnp.where(rows >= step, pltpu.roll(b, step, 0), 0.0)
        step *= 2
    tot = b[0:1] if reverse else b[c - 1:c]
    q_in = (q * ((GLA_DK ** -0.5) * jnp.exp(b))).astype(BF16)
    k_dec = k * jnp.exp(-b)
    k_in = k_dec.astype(BF16)
    k_out = (k_dec * jnp.exp(tot)).astype(BF16)
    vb = v.astype(BF16)
    s = lax.dot_general(q_in, k_in, (((1,), (1,)), ((), ())), preferred_element_type=F32)
    s = jnp.where(keep, s, 0.0).astype(BF16)
    st = st_ref[h]
    o = _dot(s, vb) + lax.dot_general(q_in, st.astype(BF16), (((1,), (1,)), ((), ())),
                                      preferred_element_type=F32)
    upd = lax.dot_general(vb, k_out, (((0,), (0,)), ((), ())), preferred_element_type=F32)
    st_ref[h] = st * jnp.exp(tot) + upd
    return o


def _gla_kernel(qf_ref, kf_ref, vf_ref, lf_ref, qb_ref, kb_ref, vb_ref, lb_ref,
                of_ref, ob_ref, sf_ref, sb_ref):
    @pl.when(pl.program_id(0) == 0)
    def _():
        sf_ref[...] = jnp.zeros_like(sf_ref)
        sb_ref[...] = jnp.zeros_like(sb_ref)

    def body(s, carry):
        rf = pl.multiple_of(s * GLA_CHUNK, GLA_CHUNK)
        rb = pl.multiple_of((GLA_SUB - 1 - s) * GLA_CHUNK, GLA_CHUNK)
        for h in range(GLA_HEADS):
            kk = slice(h * GLA_DK, (h + 1) * GLA_DK)
            vv = slice(h * GLA_DV, (h + 1) * GLA_DV)
            of_ref[pl.ds(rf, GLA_CHUNK), vv] = _gla_chunk(
                qf_ref[pl.ds(rf, GLA_CHUNK), kk], kf_ref[pl.ds(rf, GLA_CHUNK), kk],
                vf_ref[pl.ds(rf, GLA_CHUNK), vv], lf_ref[pl.ds(rf, GLA_CHUNK), kk],
                sf_ref, h, False)
            ob_ref[pl.ds(rb, GLA_CHUNK), vv] = _gla_chunk(
                qb_ref[pl.ds(rb, GLA_CHUNK), kk], kb_ref[pl.ds(rb, GLA_CHUNK), kk],
                vb_ref[pl.ds(rb, GLA_CHUNK), vv], lb_ref[pl.ds(rb, GLA_CHUNK), kk],
                sb_ref, h, True)
        return carry

    lax.fori_loop(0, GLA_SUB, body, 0, unroll=True)


def gla_scan(qkvr, first, la):
    L = qkvr.shape[0]
    rows = GLA_CHUNK * GLA_SUB
    n = L // rows
    dk, dv = GLA_DK_TOTAL, GLA_DV_TOTAL
    assert first % dv == 0
    qb, vb = first // dk, (first + 2 * dk) // dv
    specs = []
    for rev in (False, True):
        blk = (lambda i: n - 1 - i) if rev else (lambda i: i)
        specs += [pl.BlockSpec((rows, dk), lambda i, blk=blk: (blk(i), qb)),
                  pl.BlockSpec((rows, dk), lambda i, blk=blk: (blk(i), qb + 1)),
                  pl.BlockSpec((rows, dv), lambda i, blk=blk: (blk(i), vb)),
                  pl.BlockSpec((rows, dk), lambda i, blk=blk, c=int(rev): (blk(i), c))]
    return pl.pallas_call(
        _gla_kernel,
        out_shape=(jax.ShapeDtypeStruct((L, dv), F32), jax.ShapeDtypeStruct((L, dv), F32)),
        grid=(n,),
        in_specs=specs,
        out_specs=(pl.BlockSpec((rows, dv), lambda i: (i, 0)),
                   pl.BlockSpec((rows, dv), lambda i: (n - 1 - i, 0))),
        scratch_shapes=[pltpu.VMEM((GLA_HEADS, GLA_DV, GLA_DK), F32),
                        pltpu.VMEM((GLA_HEADS, GLA_DV, GLA_DK), F32)],
        compiler_params=_params("arbitrary"),
        name="gla_scan",
    )(qkvr, qkvr, qkvr, la, qkvr, qkvr, qkvr, la)


def _gla_post_kernel(of_ref, ob_ref, r_ref, g_ref, o_ref):
    g = g_ref[...]
    for h in range(GLA_HEADS):
        vv = slice(h * GLA_DV, (h + 1) * GLA_DV)
        o = of_ref[:, vv] + ob_ref[:, vv]
        ms = jnp.mean(o * o, axis=-1, keepdims=True)
        y = o * lax.rsqrt(ms + RMS_EPS) * g
        o_ref[:, vv] = (y * _silu(r_ref[:, vv])).astype(o_ref.dtype)


def gla_post(o_f, o_b, qkvr, first, norm_g, tm=512):
    L, dv = o_f.shape
    assert first % dv == 0
    return pl.pallas_call(
        _gla_post_kernel,
        out_shape=jax.ShapeDtypeStruct((L, dv), BF16),
        grid=(L // tm,),
        in_specs=[pl.BlockSpec((tm, dv), lambda i: (i, 0)),
                  pl.BlockSpec((tm, dv), lambda i: (i, 0)),
                  pl.BlockSpec((tm, dv), lambda i: (i, first // dv)),
                  pl.BlockSpec((1, GLA_DV), lambda i: (0, 0))],
        out_specs=pl.BlockSpec((tm, dv), lambda i: (i, 0)),
        compiler_params=_params("parallel"),
        name="gla_post",
    )(o_f, o_b, qkvr, norm_g.reshape(1, GLA_DV))


CT = 256
CT_G = 512
CT_COL = 128
MAX_SLABS = 16
SUB = 8


def _fft_dims(L):
    n = 2 * L
    na = 1 << ((n.bit_length() - 1) // 2)
    nb = n // na
    assert na * nb == n and nb % SUB == 0 and (na // 2) % 8 == 0
    return n, na, nb, na // 2 + 1


def _slabs_per_step(kp):
    return max(s for s in range(1, MAX_SLABS + 1) if kp % s == 0)


def _cis(num, den):
    ang = (2.0 * math.pi / den) * (num % den).astype(F32)
    return jnp.cos(ang), jnp.sin(ang)


def _expand_kernel(t_ref, o_ref):
    t = t_ref[0].astype(BF16)
    cc, cols = t.shape[1], o_ref.shape[2]
    shift, low = SUB.bit_length() - 1, SUB - 1
    src = lax.broadcasted_iota(jnp.int32, (cc, cols), 0)
    dst = lax.broadcasted_iota(jnp.int32, (cc, cols), 1)
    spread = jnp.where(lax.shift_right_logical(dst, shift) == src, 1.0, 0.0).astype(BF16)
    x = _dot(t, spread)
    row = lax.broadcasted_iota(jnp.int32, x.shape, 0)
    col = lax.broadcasted_iota(jnp.int32, x.shape, 1)
    o_ref[0] = jnp.where((col & low) == (row & low), x, 0.0).astype(o_ref.dtype)


def _expand_block_diag(t):
    g, r, c_in = t.shape
    c = -(-c_in // 16) * 16
    if c != c_in:
        return _expand_block_diag(jnp.pad(t, ((0, 0), (0, 0), (0, c - c_in))))[:, :, :c_in * SUB]
    return pl.pallas_call(
        _expand_kernel,
        out_shape=jax.ShapeDtypeStruct((g, r, c * SUB), BF16),
        grid=(g,),
        in_specs=[pl.BlockSpec((1, r, c), lambda i: (i, 0, 0))],
        out_specs=pl.BlockSpec((1, r, c * SUB), lambda i: (i, 0, 0)),
        compiler_params=_params("parallel"),
        name="dft_table_expand",
    )(t)


def _fft_tables(L):
    n, na, nb, kp = _fft_dims(L)
    ha, ng = na // 2, nb // SUB
    ar = lambda m: jnp.arange(m, dtype=jnp.int32)
    g_, ka_, bl_, a_ = ar(ng)[:, None, None, None], ar(kp)[None, :, None, None], \
        ar(SUB)[None, None, :, None], ar(ha)[None, None, None, :]
    c, s = _cis(ka_ * (nb * a_ + SUB * g_ + bl_), n)
    cs = jnp.stack([c, -s], axis=2)
    m8 = _expand_block_diag(cs.reshape(ng, kp * 2 * SUB, ha))
    c, s = _cis(ar(nb)[:, None] * ar(nb)[None, :], nb)
    f2 = jnp.stack([jnp.stack([c, s], axis=1), jnp.stack([-s, c], axis=1)], axis=0)
    f2 = f2.reshape(2, nb, 2, ng, SUB).transpose(0, 1, 3, 2, 4).reshape(2 * nb, 2 * nb)
    ca, sa = _cis(ar(kp)[:, None] * ar(nb)[None, :], n)
    cb, sb = _cis(ar(nb)[:, None] * ar(nb)[None, :], nb)
    ca, sa = (t.reshape(kp, ng, 1, SUB, 1, 1) / n for t in (ca, sa))
    cb, sb = (t.reshape(1, ng, 1, SUB, 1, nb) for t in (cb, sb))
    c, s = ca * cb - sa * sb, sa * cb + ca * sb
    gi = jnp.concatenate([jnp.concatenate([c, -s], axis=4),
                          jnp.concatenate([s, c], axis=4)], axis=2)
    gi = gi.reshape(kp, 2 * nb, 2 * nb)
    kc = ar(kp)[None, :]
    wgt = jnp.where((kc == 0) | (kc == ha), 1.0, jnp.where(kc < ha, 2.0, 0.0))
    c, s = _cis(ar(ha)[:, None] * kc, na)
    pm = jnp.stack([wgt * c, -wgt * s], axis=2).reshape(ha, 1, kp * 2)
    p8 = _expand_block_diag(jnp.broadcast_to(pm, (ha, SUB, kp * 2)).reshape(1, ha * SUB, kp * 2))[0]
    return m8, f2.astype(BF16), gi.astype(BF16), p8


def _short_conv_kernel(u_ref, w_ref, b_ref, o_ref):
    u = u_ref[...]
    L = u.shape[0]
    row = lax.broadcasted_iota(jnp.int32, u.shape, 0)
    prev = jnp.where(row == 0, 0.0, pltpu.roll(u, 1, 0))
    nxt = jnp.where(row == L - 1, 0.0, pltpu.roll(u, L - 1, 0))
    w = w_ref[...]
    o_ref[...] = prev * w[0:1] + u * w[1:2] + nxt * w[2:3] + b_ref[...]


def short_conv(u, w, b):
    L, c = u.shape[0], w.shape[1]
    w8 = jnp.zeros((8, c), F32).at[:SHORT_CONV].set(w)
    return pl.pallas_call(
        _short_conv_kernel,
        out_shape=jax.ShapeDtypeStruct((L, c), F32),
        grid=(c // CT_COL,),
        in_specs=[pl.BlockSpec((L, CT_COL), lambda j: (0, j)),
                  pl.BlockSpec((8, CT_COL), lambda j: (0, j)),
                  pl.BlockSpec((1, CT_COL), lambda j: (0, j))],
        out_specs=pl.BlockSpec((L, CT_COL), lambda j: (0, j)),
        compiler_params=_params("parallel"),
        name="short_conv",
    )(u, w8, b.reshape(1, c))


def _fft_s1_kernel(u_ref, m8_ref, a_ref):
    ha, sub, ct = u_ref.shape
    x = u_ref[...].reshape(ha * sub, ct).astype(BF16)
    r = _dot(m8_ref[0], x)
    a_ref[:, 0, :, :] = r.astype(a_ref.dtype).reshape(a_ref.shape[0], 2 * sub, ct)


def fft_stage1(u, ncols, m8, L):
    n, na, nb, kp = _fft_dims(L)
    ha, ng = na // 2, nb // SUB
    u3 = u.reshape(ha, nb, u.shape[1])
    return pl.pallas_call(
        _fft_s1_kernel,
        out_shape=jax.ShapeDtypeStruct((kp, ng, 2 * SUB, ncols), BF16),
        grid=(ng, ncols // CT_G),
        in_specs=[pl.BlockSpec((ha, SUB, CT_G), lambda g, j: (0, g, j)),
                  pl.BlockSpec((1, kp * 2 * SUB, ha * SUB), lambda g, j: (g, 0, 0))],
        out_specs=pl.BlockSpec((kp, 1, 2 * SUB, CT_G), lambda g, j: (0, g, 0, j)),
        compiler_params=_params("parallel", "parallel"),
        name="fft_stage1",
    )(u3, m8)


def _filter_s1_kernel(z_ref, w1_ref, b1_ref, f1_ref, w2_ref, b2_ref, f2_ref, w3_ref, b3_ref,
                      f3_ref, w4_ref, dl_ref, m8_ref, a_ref, nrm_ref, hh_ref, hl_ref, *, nb, L, cb):
    g, j = pl.program_id(0), pl.program_id(1)
    ha, sub, _ = z_ref.shape
    rows = ha * sub

    @pl.when(j == 0)
    def _():
        z = z_ref[...].reshape(rows, z_ref.shape[2])
        h = jnp.sin(f1_ref[...] * (_dot3(z, w1_ref[...]) + b1_ref[...]))
        h = jnp.sin(f2_ref[...] * (_dot3(h, w2_ref[...]) + b2_ref[...]))
        h = jnp.sin(f3_ref[...] * (_dot3(h, w3_ref[...]) + b3_ref[...]))
        hh_ref[...], hl_ref[...] = _split(h)

    hh, hl = hh_ref[...], hl_ref[...]
    wh, wl = _split(w4_ref[...])
    ct = wh.shape[1]
    k = (_dot(hh, wh) + (_dot(hh, wl) + _dot(hl, wh))).reshape(ha, sub, ct)
    a3 = lax.broadcasted_iota(jnp.int32, (ha, sub, ct), 0)
    b3 = lax.broadcasted_iota(jnp.int32, (ha, sub, ct), 1)
    tpos = a3 * nb + (b3 + g * sub)
    k = k * jnp.exp(-(tpos.astype(F32) * (1.0 / (L - 1))) * dl_ref[...])
    k = jnp.where(jnp.logical_and((j // cb) % 2 == 1, tpos == 0), 0.0, k).reshape(rows, ct)
    nrm_ref[0] = jnp.sum(jnp.abs(k), axis=0, keepdims=True)
    q = _dot(m8_ref[0], k.astype(BF16))
    a_ref[:, 0, :, :] = q.astype(a_ref.dtype).reshape(a_ref.shape[0], 2 * sub, ct)


def filter_stage1(L, w1, b1, f1, w2, b2, f2, w3, b3, f3, w4, m8):
    n, na, nb, kp = _fft_dims(L)
    ha, ng = na // 2, nb // SUB
    c = HYENA_WIDTH
    hdim = FILTER_HIDDEN
    t = jnp.linspace(0.0, 1.0, L, dtype=F32)[:, None]
    omega = 2.0 * math.pi * jnp.arange(L, dtype=F32)[:, None] / L
    bands = jnp.linspace(1e-4, POS_BANDS - 1, POS_BANDS, dtype=F32)[None, :]
    z = jnp.concatenate([t, jnp.cos(bands * omega), -jnp.sin(bands * omega)], axis=-1)
    z = jnp.zeros((L, LANES), F32).at[:, :POS_EMB_DIM].set(z).reshape(ha, nb, LANES)
    w1p = jnp.zeros((LANES, hdim), F32).at[:POS_EMB_DIM].set(w1)
    deltas = jnp.abs(jnp.linspace(MIN_DECAY, MAX_DECAY, c, dtype=F32)).reshape(1, c)
    nc = w4.shape[1]
    cb = c // CT_G
    row = lambda v: v.reshape(1, hdim)
    full = lambda shape: pl.BlockSpec(shape, lambda g, j: (0,) * len(shape))
    kern = functools.partial(_filter_s1_kernel, nb=nb, L=L, cb=cb)
    return pl.pallas_call(
        kern,
        out_shape=(jax.ShapeDtypeStruct((kp, ng, 2 * SUB, nc), BF16),
                   jax.ShapeDtypeStruct((ng, 1, nc), F32)),
        grid=(ng, nc // CT_G),
        in_specs=[pl.BlockSpec((ha, SUB, LANES), lambda g, j: (0, g, 0)),
                  full((LANES, hdim)), full((1, hdim)), full((1, hdim)),
                  full((hdim, hdim)), full((1, hdim)), full((1, hdim)),
                  full((hdim, hdim)), full((1, hdim)), full((1, hdim)),
                  pl.BlockSpec((hdim, CT_G), lambda g, j: (0, j)),
                  pl.BlockSpec((1, CT_G), lambda g, j: (0, j % cb)),
                  pl.BlockSpec((1, kp * 2 * SUB, ha * SUB), lambda g, j: (g, 0, 0))],
        out_specs=(pl.BlockSpec((kp, 1, 2 * SUB, CT_G), lambda g, j: (0, g, 0, j)),
                   pl.BlockSpec((1, 1, CT_G), lambda g, j: (g, 0, j))),
        scratch_shapes=[pltpu.VMEM((ha * SUB, hdim), BF16), pltpu.VMEM((ha * SUB, hdim), BF16)],
        compiler_params=_params("arbitrary", "arbitrary"),
        name="hyena_filter_stage1",
    )(z, w1p, row(b1), row(f1), w2, row(b2), row(f2), w3, row(b3), row(f3), w4, deltas, m8)


def _slab(ref, s):
    _, ng, rows, ct = ref.shape
    return ref[s].reshape(ng * rows, ct)


def _filter_spec_kernel(af_ref, ab_ref, f2_ref, nf_ref, nbk_ref, h_ref):
    f2 = f2_ref[...]
    nb = f2.shape[0] // 2
    inv = 1.0 / (jnp.sum(nf_ref[...], axis=0) + jnp.sum(nbk_ref[...], axis=0))
    for s in range(af_ref.shape[0]):
        xf = _dot(f2, _slab(af_ref, s))
        xb = _dot(f2, _slab(ab_ref, s))
        h_ref[0, s] = ((xf[:nb] + xb[:nb]) * inv).astype(h_ref.dtype)
        h_ref[1, s] = ((xf[nb:] - xb[nb:]) * inv).astype(h_ref.dtype)


def filter_spectrum(a, nrm, f2, L):
    n, na, nb, kp = _fft_dims(L)
    ng = nb // SUB
    c = HYENA_WIDTH
    cb = c // CT
    ms = _slabs_per_step(kp)
    fcol = lambda j: (j // cb) * 2 * cb + j % cb
    return pl.pallas_call(
        _filter_spec_kernel,
        out_shape=jax.ShapeDtypeStruct((2, kp, nb, HYENA_ORDER * c), BF16),
        grid=(HYENA_ORDER * cb, kp // ms),
        in_specs=[pl.BlockSpec((ms, ng, 2 * SUB, CT), lambda j, k: (k, 0, 0, fcol(j))),
                  pl.BlockSpec((ms, ng, 2 * SUB, CT), lambda j, k: (k, 0, 0, fcol(j) + cb)),
                  pl.BlockSpec((2 * nb, 2 * nb), lambda j, k: (0, 0)),
                  pl.BlockSpec((ng, 1, CT), lambda j, k: (0, 0, fcol(j))),
                  pl.BlockSpec((ng, 1, CT), lambda j, k: (0, 0, fcol(j) + cb))],
        out_specs=pl.BlockSpec((2, ms, nb, CT), lambda j, k: (0, k, 0, j)),
        compiler_params=_params("parallel", "parallel"),
        name="hyena_filter_spectrum",
    )(a, a, f2, nrm, nrm)


def _fft_mid_kernel(a_ref, h_ref, f2_ref, g_ref, q_ref):
    f2 = f2_ref[...]
    nb = f2.shape[0] // 2
    slabs, ng, rows, ct = q_ref.shape
    for s in range(slabs):
        x = _dot(f2, _slab(a_ref, s))
        xr, xi = x[:nb], x[nb:]
        hr, hi = h_ref[0, s].astype(F32), h_ref[1, s].astype(F32)
        y = jnp.concatenate([xr * hr - xi * hi, xr * hi + xi * hr], axis=0).astype(BF16)
        q_ref[s] = _dot(g_ref[s], y).astype(q_ref.dtype).reshape(ng, rows, ct)


def fft_mid(a, hf, order, f2, g, L):
    n, na, nb, kp = _fft_dims(L)
    ng = nb // SUB
    c = a.shape[3]
    cb = c // CT
    ms = _slabs_per_step(kp)
    return pl.pallas_call(
        _fft_mid_kernel,
        out_shape=jax.ShapeDtypeStruct(a.shape, BF16),
        grid=(kp // ms, cb),
        in_specs=[pl.BlockSpec((ms, ng, 2 * SUB, CT), lambda k, j: (k, 0, 0, j)),
                  pl.BlockSpec((2, ms, nb, CT), lambda k, j: (0, k, 0, order * cb + j)),
                  pl.BlockSpec((2 * nb, 2 * nb), lambda k, j: (0, 0)),
                  pl.BlockSpec((ms, 2 * nb, 2 * nb), lambda k, j: (k, 0, 0))],
        out_specs=pl.BlockSpec((ms, ng, 2 * SUB, CT), lambda k, j: (k, 0, 0, j)),
        compiler_params=_params("parallel", "parallel"),
        name="fft_mid",
    )(a, hf, f2, g)


def _fft_i2_kernel(q_ref, p8_ref, u_ref, x_ref, s_ref, o_ref):
    kp, _, rows, ct = q_ref.shape
    q = q_ref[:, 0, :, :].reshape(kp * rows, ct)
    y = _dot(p8_ref[...], q).reshape(o_ref.shape)
    o_ref[...] = x_ref[...] * (y + u_ref[...] * s_ref[...])


def fft_stage_out(q, u, gate_src, gate_blk, skip, p8, L):
    n, na, nb, kp = _fft_dims(L)
    ha, ng = na // 2, nb // SUB
    c = q.shape[3]
    u3 = u.reshape(ha, nb, u.shape[1])
    g3 = gate_src.reshape(ha, nb, gate_src.shape[1])
    out = pl.pallas_call(
        _fft_i2_kernel,
        out_shape=jax.ShapeDtypeStruct((ha, nb, c), F32),
        grid=(ng, c // CT_G),
        in_specs=[pl.BlockSpec((kp, 1, 2 * SUB, CT_G), lambda g, j: (0, g, 0, j)),
                  pl.BlockSpec((ha * SUB, kp * 2 * SUB), lambda g, j: (0, 0)),
                  pl.BlockSpec((ha, SUB, CT_G), lambda g, j: (0, g, j)),
                  pl.BlockSpec((ha, SUB, CT_G), lambda g, j: (0, g, gate_blk + j)),
                  pl.BlockSpec((1, 1, CT_G), lambda g, j: (0, 0, j))],
        out_specs=pl.BlockSpec((ha, SUB, CT_G), lambda g, j: (0, g, j)),
        compiler_params=_params("parallel", "parallel"),
        name="fft_stage_out",
    )(q, p8, u3, g3, skip.reshape(1, 1, c))
    return out.reshape(L, c)


def hyena_mixer(u_hy, conv_w, conv_b, filt, skip, tables):
    L = u_hy.shape[0]
    c = HYENA_WIDTH
    m8, f2, g, p8 = tables
    uc = short_conv(u_hy, conv_w, conv_b)
    a_filt, nrm = filter_stage1(L, *filt, m8)
    hf = filter_spectrum(a_filt, nrm, f2, L)
    z = uc
    for order in range(HYENA_ORDER):
        a = fft_stage1(z, c, m8, L)
        q = fft_mid(a, hf, order, f2, g, L)
        z = fft_stage_out(q, z, uc, (order + 1) * (c // CT_G), skip[order], p8, L)
    return z


MOE_TM = 512
MOE_TN = 512
COMBINE_TM = 256


def _moe_plan(route, L):
    tm = MOE_TM
    nt = -(-(2 * L + N_EXPERTS * (tm - 1)) // tm)
    e_flat = jnp.concatenate([route[:, ROUTE_E1], route[:, ROUTE_E2]]).astype(jnp.int32)
    onehot = (e_flat[:, None] == jnp.arange(N_EXPERTS, dtype=jnp.int32)[None, :]).astype(jnp.int32)
    csum = jnp.cumsum(onehot, axis=0)
    rank = jnp.sum(onehot * (csum - 1), axis=1)
    counts = csum[-1]
    padded = ((counts + tm - 1) // tm) * tm
    ends = jnp.cumsum(padded)
    dest = (ends - padded)[e_flat] + rank
    tok = jnp.tile(jnp.arange(L, dtype=jnp.int32), 2)
    row_token = jnp.zeros((nt * tm,), jnp.int32).at[dest].set(tok, unique_indices=True)
    n_used = ends[-1] // tm
    tile_row = jnp.arange(nt, dtype=jnp.int32) * tm
    tile_expert = jnp.sum((tile_row[:, None] >= ends[None, :]).astype(jnp.int32), axis=1)
    last = jnp.take(tile_expert, jnp.maximum(n_used - 1, 0))
    tile_expert = jnp.where(jnp.arange(nt) < n_used, tile_expert, last)
    tile_first = jnp.concatenate([jnp.ones((1,), jnp.int32),
                                  (tile_expert[1:] != tile_expert[:-1]).astype(jnp.int32)])
    tile_group = jnp.cumsum(tile_first) - 1
    group_expert = jnp.zeros((N_EXPERTS,), jnp.int32).at[tile_group].set(tile_expert)
    n_groups = jnp.take(tile_group, jnp.maximum(n_used - 1, 0)) + 1
    i32 = lambda v: v.reshape(1).astype(jnp.int32)
    return dict(nt=nt, row_token=row_token, dest=dest,
                scalars=(tile_expert, tile_first, i32(n_used), tile_group.astype(jnp.int32),
                         i32(n_groups), group_expert))


def _row_copy(src_hbm, idx, buf, slot, r, sem):
    return pltpu.make_async_copy(src_hbm.at[pl.ds(idx, 1)], buf.at[slot, pl.ds(r, 1)],
                                 sem.at[slot])


def _gather_rows(idx_ref, src_hbm, buf, slot, sem, n_rows, start):
    if not start:
        pltpu.make_async_copy(src_hbm.at[pl.ds(0, n_rows)], buf.at[slot], sem.at[slot]).wait()
        return

    def body(r, carry):
        _row_copy(src_hbm, idx_ref[0, 0, r], buf, slot, r, sem).start()
        return carry
    lax.fori_loop(0, n_rows, body, 0, unroll=8)


def _pipelined_gather(cur_ref, nxt_ref, src_hbm, buf, sem, n_rows):
    i, n = pl.program_id(0), pl.num_programs(0)
    slot = lax.rem(i, 2)

    @pl.when(i == 0)
    def _():
        _gather_rows(cur_ref, src_hbm, buf, 0, sem, n_rows, True)

    @pl.when(i + 1 < n)
    def _():
        _gather_rows(nxt_ref, src_hbm, buf, 1 - slot, sem, n_rows, True)

    _gather_rows(cur_ref, src_hbm, buf, slot, sem, n_rows, False)
    return slot


def _dispatch_kernel(cur_ref, nxt_ref, h_hbm, o_ref, buf, sem):
    slot = _pipelined_gather(cur_ref, nxt_ref, h_hbm, buf, sem, MOE_TM)
    o_ref[...] = buf[slot].astype(o_ref.dtype)


def moe_dispatch(h, row_token, nt):
    L, d = h.shape
    idx = row_token.reshape(nt, 1, MOE_TM)
    smem = lambda f: pl.BlockSpec((1, 1, MOE_TM), f, memory_space=pltpu.SMEM)
    return pl.pallas_call(
        _dispatch_kernel,
        out_shape=jax.ShapeDtypeStruct((nt * MOE_TM, d), BF16),
        grid=(nt,),
        in_specs=[smem(lambda i: (i, 0, 0)),
                  smem(lambda i: (jnp.minimum(i + 1, nt - 1), 0, 0)),
                  pl.BlockSpec(memory_space=pl.ANY)],
        out_specs=pl.BlockSpec((MOE_TM, d), lambda i: (i, 0)),
        scratch_shapes=[pltpu.VMEM((2, MOE_TM, d), F32), pltpu.SemaphoreType.DMA((2,))],
        compiler_params=_params("arbitrary"),
        name="moe_dispatch",
    )(idx, idx, h)


def _weight_copy(w_hbm, e, jcol, wbuf, k, slot, sem):
    tn = wbuf.shape[3]
    return pltpu.make_async_copy(w_hbm.at[e, :, pl.ds(pl.multiple_of(jcol * tn, tn), tn)],
                                 wbuf.at[k, slot], sem.at[k, slot])


def _refresh_expert_weights(scalars, w_hbms, wbuf, wbs, sem):
    te_ref, tf_ref, _, tg_ref, ng_ref, ge_ref = scalars
    j, i = pl.program_id(0), pl.program_id(1)
    nj = pl.num_programs(0)

    @pl.when(tf_ref[i] == 1)
    def _():
        g, ng = tg_ref[i], ng_ref[0]
        blk = j * ng + g
        slot = lax.rem(blk, 2)

        @pl.when(blk == 0)
        def _():
            for k, w in enumerate(w_hbms):
                _weight_copy(w, te_ref[i], j, wbuf, k, 0, sem).start()

        for k, w in enumerate(w_hbms):
            _weight_copy(w, 0, 0, wbuf, k, slot, sem).wait()
        for k, wb in enumerate(wbs):
            wb[...] = wbuf[k, slot].astype(BF16)

        wrap = g + 1 == ng
        nxt_e = ge_ref[jnp.where(wrap, 0, g + 1)]
        nxt_j = j + wrap.astype(jnp.int32)

        @pl.when(nxt_j < nj)
        def _():
            for k, w in enumerate(w_hbms):
                _weight_copy(w, nxt_e, nxt_j, wbuf, k, 1 - slot, sem).start()


def _moe_up_kernel(te_ref, tf_ref, nu_ref, tg_ref, ng_ref, ge_ref, a_ref, wg_hbm, wu_hbm, o_ref,
                   wbuf, wgb, wub, sem):
    i = pl.program_id(1)
    _refresh_expert_weights((te_ref, tf_ref, nu_ref, tg_ref, ng_ref, ge_ref),
                            (wg_hbm, wu_hbm), wbuf, (wgb, wub), sem)

    @pl.when(i < nu_ref[0])
    def _():
        a = a_ref[...]
        o_ref[...] = (_silu(_dot(a, wgb[...])) * _dot(a, wub[...])).astype(o_ref.dtype)

    @pl.when(i >= nu_ref[0])
    def _():
        o_ref[...] = jnp.zeros_like(o_ref)


def moe_up(hs, w_gate, w_up, plan):
    r, d = hs.shape
    f = w_gate.shape[2]
    nt = plan["nt"]
    anyspec = pl.BlockSpec(memory_space=pl.ANY)
    return pl.pallas_call(
        _moe_up_kernel,
        out_shape=jax.ShapeDtypeStruct((r, f), BF16),
        grid_spec=pltpu.PrefetchScalarGridSpec(
            num_scalar_prefetch=6,
            grid=(f // MOE_TN, nt),
            in_specs=[pl.BlockSpec((MOE_TM, d), lambda j, i, *_: (i, 0)), anyspec, anyspec],
            out_specs=pl.BlockSpec((MOE_TM, MOE_TN), lambda j, i, *_: (i, j)),
            scratch_shapes=[pltpu.VMEM((2, 2, d, MOE_TN), F32),
                            pltpu.VMEM((d, MOE_TN), BF16), pltpu.VMEM((d, MOE_TN), BF16),
                            pltpu.SemaphoreType.DMA((2, 2))]),
        compiler_params=_params("arbitrary", "arbitrary"),
        name="moe_up",
    )(*plan["scalars"], hs, w_gate, w_up)


def _moe_down_kernel(te_ref, tf_ref, nu_ref, tg_ref, ng_ref, ge_ref, t_ref, wd_hbm, o_ref,
                     wbuf, wdb, sem):
    i = pl.program_id(1)
    _refresh_expert_weights((te_ref, tf_ref, nu_ref, tg_ref, ng_ref, ge_ref),
                            (wd_hbm,), wbuf, (wdb,), sem)

    @pl.when(i < nu_ref[0])
    def _():
        o_ref[...] = _dot(t_ref[...], wdb[...])

    @pl.when(i >= nu_ref[0])
    def _():
        o_ref[...] = jnp.zeros_like(o_ref)


def moe_down(t, w_down, plan):
    r, f = t.shape
    d = w_down.shape[2]
    nt = plan["nt"]
    return pl.pallas_call(
        _moe_down_kernel,
        out_shape=jax.ShapeDtypeStruct((r, d), F32),
        grid_spec=pltpu.PrefetchScalarGridSpec(
            num_scalar_prefetch=6,
            grid=(d // MOE_TN, nt),
            in_specs=[pl.BlockSpec((MOE_TM, f), lambda j, i, *_: (i, 0)),
                      pl.BlockSpec(memory_space=pl.ANY)],
            out_specs=pl.BlockSpec((MOE_TM, MOE_TN), lambda j, i, *_: (i, j)),
            scratch_shapes=[pltpu.VMEM((1, 2, f, MOE_TN), F32), pltpu.VMEM((f, MOE_TN), BF16),
                            pltpu.SemaphoreType.DMA((1, 2))]),
        compiler_params=_params("arbitrary", "arbitrary"),
        name="moe_down",
    )(*plan["scalars"], t, w_down)


def _combine_kernel(cur_ref, nxt_ref, y_hbm, x_ref, route_ref, g_ref, o_ref, buf, sem, *, final):
    tm = x_ref.shape[0]
    slot = _pipelined_gather(cur_ref, nxt_ref, y_hbm, buf, sem, 2 * tm)
    route = route_ref[...]
    g1 = route[:, ROUTE_G1:ROUTE_G1 + 1]
    g2 = route[:, ROUTE_G2:ROUTE_G2 + 1]
    x = x_ref[...] + g1 * buf[slot, :tm, :] + g2 * buf[slot, tm:, :]
    if final:
        ms = jnp.mean(x * x, axis=-1, keepdims=True)
        x = x * lax.rsqrt(ms + RMS_EPS) * g_ref[...]
    o_ref[...] = x


def moe_combine(x, y, dest, route, final_gain):
    L, d = x.shape
    tm = COMBINE_TM
    nt = L // tm
    idx = jnp.concatenate([dest[:L].reshape(nt, 1, tm), dest[L:].reshape(nt, 1, tm)], axis=2)
    smem = lambda f: pl.BlockSpec((1, 1, 2 * tm), f, memory_space=pltpu.SMEM)
    final = final_gain is not None
    gain = (final_gain if final else jnp.ones((d,), F32)).reshape(1, d)
    kern = functools.partial(_combine_kernel, final=final)
    return pl.pallas_call(
        kern,
        out_shape=jax.ShapeDtypeStruct((L, d), F32),
        grid=(nt,),
        in_specs=[smem(lambda i: (i, 0, 0)),
                  smem(lambda i: (jnp.minimum(i + 1, nt - 1), 0, 0)),
                  pl.BlockSpec(memory_space=pl.ANY),
                  pl.BlockSpec((tm, d), lambda i: (i, 0)),
                  pl.BlockSpec((tm, LANES), lambda i: (i, 0)),
                  pl.BlockSpec((1, d), lambda i: (0, 0))],
        out_specs=pl.BlockSpec((tm, d), lambda i: (i, 0)),
        scratch_shapes=[pltpu.VMEM((2, 2 * tm, d), F32), pltpu.SemaphoreType.DMA((2,))],
        compiler_params=_params("arbitrary"),
        name="moe_combine",
    )(idx, idx, y, x, route, gain)


def moe_ffn(x, norm_g, router_w, w_gate, w_up, w_down, final_gain):
    L = x.shape[0]
    h, route = rmsnorm_router(x, norm_g, router_w)
    plan = _moe_plan(route, L)
    hs = moe_dispatch(h, plan["row_token"], plan["nt"])
    t = moe_up(hs, w_gate, w_up, plan)
    y = moe_down(t, w_down, plan)
    return moe_combine(x, y, plan["dest"], route, final_gain)


def _layer_mixers(x, l, tables, norm_mix, w_in_t, conv_w, conv_b, filt, hyena_skip, w_branch_a,
                  gla_gate_w2, gla_gate_b, gla_norm, w_branch_b, w_out):
    hc = (HYENA_ORDER + 1) * HYENA_WIDTH
    o_lr = hc + 2 * GLA_DK_TOTAL + 2 * GLA_DV_TOTAL
    o_g = o_lr + 2 * GLA_LOWRANK
    h = rmsnorm(x, norm_mix[l], BF16)
    proj = matmul([h], [(w_in_t, l, 0)], o_lr, [(0, 0)], _ep_plain, [], F32,
                  tm=1024, tn=1024, name="in_proj", w_transposed=True)
    gates = matmul([h], [(w_in_t[l, o_g:][None], 0, 0)], 2 * D_MODEL, [(0, 0)], _ep_sigmoid, [],
                   BF16, tm=1024, tn=1024, name="in_proj_gates", w_transposed=True)
    la = gla_gates(h, w_in_t[l, o_lr:o_g], gla_gate_w2[l], gla_gate_b[l])

    z_a = hyena_mixer(proj, conv_w[l], conv_b[l], filt, hyena_skip[l], tables)
    o_f, o_b = gla_scan(proj, hc, la)
    z_b = gla_post(o_f, o_b, proj, hc + 2 * GLA_DK_TOTAL + GLA_DV_TOTAL, gla_norm[l])

    mixed = matmul([z_a, z_b], [(w_branch_a, l, 0), (w_branch_b, l, 0)], D_MODEL,
                   [(0, 0), (1, 1)], _ep_merge, [(gates, 0), (gates, D_MODEL)], BF16,
                   tm=1024, tn=1024, name="branch_merge")
    return matmul([mixed], [(w_out, l, 0)], D_MODEL, [(0, 0)], _ep_residual, [(x, 0)], F32,
                  tm=1024, tn=512, name="out_proj")


def _swiglu_ffn(x, h, wg, wu, wd, i):
    f = wg.shape[2]
    t = matmul([h], [(wg, i, 0), (wu, i, 0)], f, [(0, 0), (0, 1)], _ep_swiglu, [], BF16,
               tm=1024, tn=512, name="ffn_up")
    return matmul([t], [(wd, i, 0)], wd.shape[2], [(0, 0)], _ep_residual, [(x, 0)], F32,
                  tm=512, tn=512, name="ffn_down")


def kernel(x, norm_mix, w_in, conv_w, conv_b, filt_w1, filt_b1, filt_freq1, filt_w2, filt_b2, filt_freq2, filt_w3, filt_b3, filt_freq3, filt_w4, hyena_skip, w_branch_a, gla_gate_w2, gla_gate_b, gla_norm, w_branch_b, w_out, norm_ffn, dense_w_gate, dense_w_up, dense_w_down, router_w, moe_w_gate, moe_w_up, moe_w_down, norm_final):
    b, L, d = x.shape
    assert b == 1
    x = x.reshape(L, d)
    tables = _fft_tables(L)
    w_in_t = jnp.swapaxes(w_in, 1, 2)
    for l in range(DEPTH):
        filt = (filt_w1[l], filt_b1[l], filt_freq1[l], filt_w2[l], filt_b2[l], filt_freq2[l],
                filt_w3[l], filt_b3[l], filt_freq3[l], filt_w4[l])
        x = _layer_mixers(x, l, tables, norm_mix, w_in_t, conv_w, conv_b, filt, hyena_skip,
                          w_branch_a, gla_gate_w2, gla_gate_b, gla_norm, w_branch_b, w_out)
        i = l // 2
        last = l == DEPTH - 1
        if l % 2 == 0:
            h = rmsnorm(x, norm_ffn[l], BF16)
            x = _swiglu_ffn(x, h, dense_w_gate, dense_w_up, dense_w_down, i)
            if last:
                x = rmsnorm(x, norm_final, F32)
        else:
            x = moe_ffn(x, norm_ffn[l], router_w[i], moe_w_gate[i], moe_w_up[i], moe_w_down[i],
                        norm_final if last else None)
    return x.reshape(b, L, d)
```

```python
import functools
import math

import jax
import jax.numpy as jnp
from jax import lax
from jax.experimental import pallas as pl
from jax.experimental.pallas import tpu as pltpu

F32 = jnp.float32
BF16 = jnp.bfloat16

D_MODEL = 2048
DEPTH = 2
HYENA_WIDTH = D_MODEL // 2
HYENA_ORDER = 2
SHORT_CONV = 3
POS_EMB_DIM = 33
POS_BANDS = (POS_EMB_DIM - 1) // 2
FILTER_HIDDEN = 64
MIN_DECAY = math.log(1e-2) / 1.5
MAX_DECAY = math.log(1e-2) / 0.3
GLA_HEADS = 4
GLA_DK = 128
GLA_DV = 256
GLA_DK_TOTAL = GLA_HEADS * GLA_DK
GLA_DV_TOTAL = GLA_HEADS * GLA_DV
GLA_LOWRANK = 16
GLA_GATE_TAU = 16.0
GLA_CHUNK = 64
N_EXPERTS = 8
RMS_EPS = 1e-6

LANES = 128
VMEM_LIMIT_BYTES = 56 * 1024 * 1024


def _params(*sem):
    return pltpu.CompilerParams(dimension_semantics=sem, vmem_limit_bytes=VMEM_LIMIT_BYTES)


def _dot(a, b):
    return jnp.dot(a, b, preferred_element_type=F32)


def _split(a):
    hi = a.astype(BF16)
    lo = (a - hi.astype(F32)).astype(BF16)
    return hi, lo


def _dot3(a, b):
    ah, al = _split(a)
    bh, bl = _split(b)
    return _dot(ah, bh) + (_dot(ah, bl) + _dot(al, bh))


def _sigmoid(x):
    return 1.0 / (1.0 + jnp.exp(-x))


def _silu(x):
    return x * _sigmoid(x)


def _log_sigmoid(x):
    return jnp.minimum(x, 0.0) - jnp.log(1.0 + jnp.exp(-jnp.abs(x)))


def _rmsnorm_kernel(x_ref, g_ref, o_ref):
    x = x_ref[...]
    ms = jnp.mean(x * x, axis=-1, keepdims=True)
    o_ref[...] = (x * lax.rsqrt(ms + RMS_EPS) * g_ref[...]).astype(o_ref.dtype)


def rmsnorm(x, g, out_dtype, tm=512):
    m, d = x.shape
    return pl.pallas_call(
        _rmsnorm_kernel,
        out_shape=jax.ShapeDtypeStruct((m, d), out_dtype),
        grid=(m // tm,),
        in_specs=[pl.BlockSpec((tm, d), lambda i: (i, 0)),
                  pl.BlockSpec((1, d), lambda i: (0, 0))],
        out_specs=pl.BlockSpec((tm, d), lambda i: (i, 0)),
        compiler_params=_params("parallel"),
        name="rmsnorm",
    )(x, g.reshape(1, d))


def _rmsnorm_router_kernel(x_ref, g_ref, rw_ref, o_ref, cw_ref):
    x = x_ref[...]
    ms = jnp.mean(x * x, axis=-1, keepdims=True)
    h = x * lax.rsqrt(ms + RMS_EPS) * g_ref[...]
    o_ref[...] = h.astype(o_ref.dtype)
    logits = _dot3(h, rw_ref[...])
    col = lax.broadcasted_iota(jnp.int32, logits.shape, 1).astype(F32)
    neg = jnp.float32(-jnp.inf)
    logits = jnp.where(col < N_EXPERTS, logits, neg)
    m1 = jnp.max(logits, axis=-1, keepdims=True)
    i1 = jnp.min(jnp.where(logits == m1, col, float(LANES)), axis=-1, keepdims=True)
    rest = jnp.where(col == i1, neg, logits)
    m2 = jnp.max(rest, axis=-1, keepdims=True)
    i2 = jnp.min(jnp.where(rest == m2, col, float(LANES)), axis=-1, keepdims=True)
    e2 = jnp.exp(m2 - m1)
    g1 = 1.0 / (1.0 + e2)
    g2 = e2 / (1.0 + e2)
    cw_ref[...] = (jnp.where(col == ROUTE_E1, i1, 0.0) + jnp.where(col == ROUTE_E2, i2, 0.0)
                   + jnp.where(col == ROUTE_G1, g1, 0.0) + jnp.where(col == ROUTE_G2, g2, 0.0))


ROUTE_E1, ROUTE_E2, ROUTE_G1, ROUTE_G2 = 0, 1, 2, 3


def rmsnorm_router(x, g, router_w, tm=256):
    m, d = x.shape
    rw = jnp.zeros((d, LANES), F32).at[:, :N_EXPERTS].set(router_w)
    return pl.pallas_call(
        _rmsnorm_router_kernel,
        out_shape=(jax.ShapeDtypeStruct((m, d), F32), jax.ShapeDtypeStruct((m, LANES), F32)),
        grid=(m // tm,),
        in_specs=[pl.BlockSpec((tm, d), lambda i: (i, 0)),
                  pl.BlockSpec((1, d), lambda i: (0, 0)),
                  pl.BlockSpec((d, LANES), lambda i: (0, 0))],
        out_specs=(pl.BlockSpec((tm, d), lambda i: (i, 0)),
                   pl.BlockSpec((tm, LANES), lambda i: (i, 0))),
        compiler_params=_params("parallel"),
        name="rmsnorm_router",
    )(x, g.reshape(1, d), rw)


def _mm_kernel(*refs, n_a, n_w, dots, n_extra, epilogue, w_transposed):
    a_refs = refs[:n_a]
    w_refs = refs[n_a:n_a + n_w]
    e_refs = refs[n_a + n_w:n_a + n_w + n_extra]
    o_ref = refs[n_a + n_w + n_extra]
    wb_refs = refs[n_a + n_w + n_extra + 1:]

    @pl.when(pl.program_id(1) == 0)
    def _():
        for w, wb in zip(w_refs, wb_refs):
            wb[...] = w[0].astype(BF16)

    a_vals = [a[...].astype(BF16) for a in a_refs]
    nt_dims = (((1,), (1,)), ((), ()))
    accs = [lax.dot_general(a_vals[ai], wb_refs[wi][...], nt_dims, preferred_element_type=F32)
            if w_transposed else _dot(a_vals[ai], wb_refs[wi][...]) for ai, wi in dots]
    o_ref[...] = epilogue(accs, [e[...] for e in e_refs]).astype(o_ref.dtype)


def matmul(a_list, w_list, n, dots, epilogue, extras, out_dtype, *, tm, tn, name,
           w_transposed=False):
    m = a_list[0].shape[0]
    assert m % tm == 0 and n % tn == 0
    in_specs = [pl.BlockSpec((tm, a.shape[1]), lambda j, i: (i, 0)) for a in a_list]
    w_arrays, scratch = [], []
    for w, s, first in w_list:
        assert first % tn == 0
        w_arrays.append(w)
        if w_transposed:
            in_specs.append(pl.BlockSpec((1, tn, w.shape[2]),
                                         lambda j, i, s=s, off=first // tn: (s, off + j, 0)))
            scratch.append(pltpu.VMEM((tn, w.shape[2]), BF16))
            continue
        in_specs.append(pl.BlockSpec((1, w.shape[1], tn),
                                     lambda j, i, s=s, off=first // tn: (s, 0, off + j)))
        scratch.append(pltpu.VMEM((w.shape[1], tn), BF16))
    extra_arrays = []
    for arr, first in extras:
        assert first % tn == 0
        extra_arrays.append(arr)
        in_specs.append(pl.BlockSpec((tm, tn), lambda j, i, off=first // tn: (i, off + j)))
    kern = functools.partial(_mm_kernel, n_a=len(a_list), n_w=len(w_list), dots=tuple(dots),
                             n_extra=len(extras), epilogue=epilogue, w_transposed=w_transposed)
    return pl.pallas_call(
        kern,
        out_shape=jax.ShapeDtypeStruct((m, n), out_dtype),
        grid=(n // tn, m // tm),
        in_specs=in_specs,
        out_specs=pl.BlockSpec((tm, tn), lambda j, i: (i, j)),
        scratch_shapes=scratch,
        compiler_params=_params("arbitrary", "arbitrary"),
        name=name,
    )(*a_list, *w_arrays, *extra_arrays)


def _ep_plain(accs, ex):
    return accs[0]


def _ep_sigmoid(accs, ex):
    return _sigmoid(accs[0])


def _ep_residual(accs, ex):
    return ex[0] + accs[0]


def _ep_merge(accs, ex):
    return ex[0].astype(F32) * accs[0] + ex[1].astype(F32) * accs[1]


def _ep_swiglu(accs, ex):
    return _silu(accs[0]) * accs[1]


def _gate_kernel(h_ref, wlr_ref, w2_ref, b_ref, o_ref):
    lr = lax.dot_general(h_ref[...], wlr_ref[...].astype(BF16), (((1,), (1,)), ((), ())),
                         preferred_element_type=F32)
    z = _dot3(lr, w2_ref[...]) + b_ref[...]
    o_ref[...] = _log_sigmoid(z) * (1.0 / GLA_GATE_TAU)


def gla_gates(h, w_lr_t, gate_w2, gate_b, tm=512):
    m, d = h.shape
    r = GLA_LOWRANK
    n = 2 * GLA_DK_TOTAL
    wlr = jnp.zeros((LANES, d), F32).at[:2 * r].set(w_lr_t)
    w2 = jnp.zeros((LANES, n), F32)
    w2 = w2.at[:r, :GLA_DK_TOTAL].set(gate_w2[0]).at[r:2 * r, GLA_DK_TOTAL:].set(gate_w2[1])
    b = gate_b.reshape(1, n)
    return pl.pallas_call(
        _gate_kernel,
        out_shape=jax.ShapeDtypeStruct((m, n), F32),
        grid=(m // tm,),
        in_specs=[pl.BlockSpec((tm, d), lambda i: (i, 0)),
                  pl.BlockSpec((LANES, d), lambda i: (0, 0)),
                  pl.BlockSpec((LANES, n), lambda i: (0, 0)),
                  pl.BlockSpec((1, n), lambda i: (0, 0))],
        out_specs=pl.BlockSpec((tm, n), lambda i: (i, 0)),
        compiler_params=_params("parallel"),
        name="gla_gates",
    )(h, wlr, w2, b)


GLA_SUB = 4


def _gla_chunk(q, k, v, la, st_ref, h, reverse):
    c = GLA_CHUNK
    row = lax.broadcasted_iota(jnp.int32, (c, c), 0)
    col = lax.broadcasted_iota(jnp.int32, (c, c), 1)
    keep = (col > row) if reverse else (col <= row)
    b = la
    rows = lax.broadcasted_iota(jnp.int32, la.shape, 0)
    step = 1
    while step < c:
        if reverse:
            b = b + jnp.where(rows < c - step, pltpu.roll(b, c - step, 0), 0.0)
        else:
            b = b + jnp.where(rows >= step, pltpu.roll(b, step, 0), 0.0)
        step *= 2
    tot = b[0:1] if reverse else b[c - 1:c]
    q_in = (q * ((GLA_DK ** -0.5) * jnp.exp(b))).astype(BF16)
    k_dec = k * jnp.exp(-b)
    k_in = k_dec.astype(BF16)
    k_out = (k_dec * jnp.exp(tot)).astype(BF16)
    vb = v.astype(BF16)
    s = lax.dot_general(q_in, k_in, (((1,), (1,)), ((), ())), preferred_element_type=F32)
    s = jnp.where(keep, s, 0.0).astype(BF16)
    st = st_ref[h]
    o = _dot(s, vb) + lax.dot_general(q_in, st.astype(BF16), (((1,), (1,)), ((), ())),
                                      preferred_element_type=F32)
    upd = lax.dot_general(vb, k_out, (((0,), (0,)), ((), ())), preferred_element_type=F32)
    st_ref[h] = st * jnp.exp(tot) + upd
    return o


def _gla_kernel(qf_ref, kf_ref, vf_ref, lf_ref, qb_ref, kb_ref, vb_ref, lb_ref,
                of_ref, ob_ref, sf_ref, sb_ref):
    @pl.when(pl.program_id(0) == 0)
    def _():
        sf_ref[...] = jnp.zeros_like(sf_ref)
        sb_ref[...] = jnp.zeros_like(sb_ref)

    def body(s, carry):
        rf = pl.multiple_of(s * GLA_CHUNK, GLA_CHUNK)
        rb = pl.multiple_of((GLA_SUB - 1 - s) * GLA_CHUNK, GLA_CHUNK)
        for h in range(GLA_HEADS):
            kk = slice(h * GLA_DK, (h + 1) * GLA_DK)
            vv = slice(h * GLA_DV, (h + 1) * GLA_DV)
            of_ref[pl.ds(rf, GLA_CHUNK), vv] = _gla_chunk(
                qf_ref[pl.ds(rf, GLA_CHUNK), kk], kf_ref[pl.ds(rf, GLA_CHUNK), kk],
                vf_ref[pl.ds(rf, GLA_CHUNK), vv], lf_ref[pl.ds(rf, GLA_CHUNK), kk],
                sf_ref, h, False)
            ob_ref[pl.ds(rb, GLA_CHUNK), vv] = _gla_chunk(
                qb_ref[pl.ds(rb, GLA_CHUNK), kk], kb_ref[pl.ds(rb, GLA_CHUNK), kk],
                vb_ref[pl.ds(rb, GLA_CHUNK), vv], lb_ref[pl.ds(rb, GLA_CHUNK), kk],
                sb_ref, h, True)
        return carry

    lax.fori_loop(0, GLA_SUB, body, 0, unroll=True)


def gla_scan(qkvr, first, la):
    L = qkvr.shape[0]
    rows = GLA_CHUNK * GLA_SUB
    n = L // rows
    dk, dv = GLA_DK_TOTAL, GLA_DV_TOTAL
    assert first % dv == 0
    qb, vb = first // dk, (first + 2 * dk) // dv
    specs = []
    for rev in (False, True):
        blk = (lambda i: n - 1 - i) if rev else (lambda i: i)
        specs += [pl.BlockSpec((rows, dk), lambda i, blk=blk: (blk(i), qb)),
                  pl.BlockSpec((rows, dk), lambda i, blk=blk: (blk(i), qb + 1)),
                  pl.BlockSpec((rows, dv), lambda i, blk=blk: (blk(i), vb)),
                  pl.BlockSpec((rows, dk), lambda i, blk=blk, c=int(rev): (blk(i), c))]
    return pl.pallas_call(
        _gla_kernel,
        out_shape=(jax.ShapeDtypeStruct((L, dv), F32), jax.ShapeDtypeStruct((L, dv), F32)),
        grid=(n,),
        in_specs=specs,
        out_specs=(pl.BlockSpec((rows, dv), lambda i: (i, 0)),
                   pl.BlockSpec((rows, dv), lambda i: (n - 1 - i, 0))),
        scratch_shapes=[pltpu.VMEM((GLA_HEADS, GLA_DV, GLA_DK), F32),
                        pltpu.VMEM((GLA_HEADS, GLA_DV, GLA_DK), F32)],
        compiler_params=_params("arbitrary"),
        name="gla_scan",
    )(qkvr, qkvr, qkvr, la, qkvr, qkvr, qkvr, la)


def _gla_post_kernel(of_ref, ob_ref, r_ref, g_ref, o_ref):
    g = g_ref[...]
    for h in range(GLA_HEADS):
        vv = slice(h * GLA_DV, (h + 1) * GLA_DV)
        o = of_ref[:, vv] + ob_ref[:, vv]
        ms = jnp.mean(o * o, axis=-1, keepdims=True)
        y = o * lax.rsqrt(ms + RMS_EPS) * g
        o_ref[:, vv] = (y * _silu(r_ref[:, vv])).astype(o_ref.dtype)


def gla_post(o_f, o_b, qkvr, first, norm_g, tm=512):
    L, dv = o_f.shape
    assert first % dv == 0
    return pl.pallas_call(
        _gla_post_kernel,
        out_shape=jax.ShapeDtypeStruct((L, dv), BF16),
        grid=(L // tm,),
        in_specs=[pl.BlockSpec((tm, dv), lambda i: (i, 0)),
                  pl.BlockSpec((tm, dv), lambda i: (i, 0)),
                  pl.BlockSpec((tm, dv), lambda i: (i, first // dv)),
                  pl.BlockSpec((1, GLA_DV), lambda i: (0, 0))],
        out_specs=pl.BlockSpec((tm, dv), lambda i: (i, 0)),
        compiler_params=_params("parallel"),
        name="gla_post",
    )(o_f, o_b, qkvr, norm_g.reshape(1, GLA_DV))


CT = 256
CT_G = 1024
CT_COL = 128
MAX_SLABS = 16
SUB = 8


def _fft_dims(L):
    n = 2 * L
    na = 1 << ((n.bit_length() - 1) // 2)
    nb = n // na
    assert na * nb == n and nb % SUB == 0 and (na // 2) % 8 == 0
    return n, na, nb, na // 2 + 1


def _slabs_per_step(kp):
    return max(s for s in range(1, MAX_SLABS + 1) if kp % s == 0)


def _cis(num, den):
    ang = (2.0 * math.pi / den) * (num % den).astype(F32)
    return jnp.cos(ang), jnp.sin(ang)


def _expand_kernel(t_ref, o_ref):
    t = t_ref[0].astype(BF16)
    cc, cols = t.shape[1], o_ref.shape[2]
    shift, low = SUB.bit_length() - 1, SUB - 1
    src = lax.broadcasted_iota(jnp.int32, (cc, cols), 0)
    dst = lax.broadcasted_iota(jnp.int32, (cc, cols), 1)
    spread = jnp.where(lax.shift_right_logical(dst, shift) == src, 1.0, 0.0).astype(BF16)
    x = _dot(t, spread)
    row = lax.broadcasted_iota(jnp.int32, x.shape, 0)
    col = lax.broadcasted_iota(jnp.int32, x.shape, 1)
    o_ref[0] = jnp.where((col & low) == (row & low), x, 0.0).astype(o_ref.dtype)


def _expand_block_diag(t):
    g, r, c_in = t.shape
    c = -(-c_in // 16) * 16
    if c != c_in:
        return _expand_block_diag(jnp.pad(t, ((0, 0), (0, 0), (0, c - c_in))))[:, :, :c_in * SUB]
    return pl.pallas_call(
        _expand_kernel,
        out_shape=jax.ShapeDtypeStruct((g, r, c * SUB), BF16),
        grid=(g,),
        in_specs=[pl.BlockSpec((1, r, c), lambda i: (i, 0, 0))],
        out_specs=pl.BlockSpec((1, r, c * SUB), lambda i: (i, 0, 0)),
        compiler_params=_params("parallel"),
        name="dft_table_expand",
    )(t)


def _fft_tables(L):
    n, na, nb, kp = _fft_dims(L)
    ha, ng = na // 2, nb // SUB
    ar = lambda m: jnp.arange(m, dtype=jnp.int32)
    g_, ka_, bl_, a_ = ar(ng)[:, None, None, None], ar(kp)[None, :, None, None], \
        ar(SUB)[None, None, :, None], ar(ha)[None, None, None, :]
    c, s = _cis(ka_ * (nb * a_ + SUB * g_ + bl_), n)
    cs = jnp.stack([c, -s], axis=2)
    m8 = _expand_block_diag(cs.reshape(ng, kp * 2 * SUB, ha))
    c, s = _cis(ar(nb)[:, None] * ar(nb)[None, :], nb)
    f2 = jnp.stack([jnp.stack([c, s], axis=1), jnp.stack([-s, c], axis=1)], axis=0)
    f2 = f2.reshape(2, nb, 2, ng, SUB).transpose(0, 1, 3, 2, 4).reshape(2 * nb, 2 * nb)
    ca, sa = _cis(ar(kp)[:, None] * ar(nb)[None, :], n)
    cb, sb = _cis(ar(nb)[:, None] * ar(nb)[None, :], nb)
    ca, sa = (t.reshape(kp, ng, 1, SUB, 1, 1) / n for t in (ca, sa))
    cb, sb = (t.reshape(1, ng, 1, SUB, 1, nb) for t in (cb, sb))
    c, s = ca * cb - sa * sb, sa * cb + ca * sb
    gi = jnp.concatenate([jnp.concatenate([c, -s], axis=4),
                          jnp.concatenate([s, c], axis=4)], axis=2)
    gi = gi.reshape(kp, 2 * nb, 2 * nb)
    kc = ar(kp)[None, :]
    wgt = jnp.where((kc == 0) | (kc == ha), 1.0, jnp.where(kc < ha, 2.0, 0.0))
    c, s = _cis(ar(ha)[:, None] * kc, na)
    pm = jnp.stack([wgt * c, -wgt * s], axis=2).reshape(ha, 1, kp * 2)
    p8 = _expand_block_diag(jnp.broadcast_to(pm, (ha, SUB, kp * 2)).reshape(1, ha * SUB, kp * 2))[0]
    return m8, f2.astype(BF16), gi.astype(BF16), p8


def _short_conv_kernel(u_ref, w_ref, b_ref, o_ref):
    u = u_ref[...]
    L = u.shape[0]
    row = lax.broadcasted_iota(jnp.int32, u.shape, 0)
    prev = jnp.where(row == 0, 0.0, pltpu.roll(u, 1, 0))
    nxt = jnp.where(row == L - 1, 0.0, pltpu.roll(u, L - 1, 0))
    w = w_ref[...]
    o_ref[...] = prev * w[0:1] + u * w[1:2] + nxt * w[2:3] + b_ref[...]


def short_conv(u, w, b):
    L, c = u.shape[0], w.shape[1]
    w8 = jnp.zeros((8, c), F32).at[:SHORT_CONV].set(w)
    return pl.pallas_call(
        _short_conv_kernel,
        out_shape=jax.ShapeDtypeStruct((L, c), F32),
        grid=(c // CT_COL,),
        in_specs=[pl.BlockSpec((L, CT_COL), lambda j: (0, j)),
                  pl.BlockSpec((8, CT_COL), lambda j: (0, j)),
                  pl.BlockSpec((1, CT_COL), lambda j: (0, j))],
        out_specs=pl.BlockSpec((L, CT_COL), lambda j: (0, j)),
        compiler_params=_params("parallel"),
        name="short_conv",
    )(u, w8, b.reshape(1, c))


def _fft_s1_kernel(u_ref, m8_ref, a_ref):
    ha, sub, ct = u_ref.shape
    x = u_ref[...].reshape(ha * sub, ct).astype(BF16)
    r = _dot(m8_ref[0], x)
    a_ref[:, 0, :, :] = r.astype(a_ref.dtype).reshape(a_ref.shape[0], 2 * sub, ct)


def fft_stage1(u, ncols, m8, L):
    n, na, nb, kp = _fft_dims(L)
    ha, ng = na // 2, nb // SUB
    u3 = u.reshape(ha, nb, u.shape[1])
    return pl.pallas_call(
        _fft_s1_kernel,
        out_shape=jax.ShapeDtypeStruct((kp, ng, 2 * SUB, ncols), BF16),
        grid=(ng, ncols // CT_G),
        in_specs=[pl.BlockSpec((ha, SUB, CT_G), lambda g, j: (0, g, j)),
                  pl.BlockSpec((1, kp * 2 * SUB, ha * SUB), lambda g, j: (g, 0, 0))],
        out_specs=pl.BlockSpec((kp, 1, 2 * SUB, CT_G), lambda g, j: (0, g, 0, j)),
        compiler_params=_params("parallel", "parallel"),
        name="fft_stage1",
    )(u3, m8)


def _filter_s1_kernel(z_ref, w1_ref, b1_ref, f1_ref, w2_ref, b2_ref, f2_ref, w3_ref, b3_ref,
                      f3_ref, w4_ref, dl_ref, m8_ref, a_ref, nrm_ref, hh_ref, *, nb, L, cb):
    g, j = pl.program_id(0), pl.program_id(1)
    ha, sub, _ = z_ref.shape
    rows = ha * sub

    @pl.when(j == 0)
    def _():
        z = z_ref[...].reshape(rows, z_ref.shape[2])
        h = jnp.sin(f1_ref[...] * (_dot3(z, w1_ref[...]) + b1_ref[...]))
        h = jnp.sin(f2_ref[...] * (_dot3(h, w2_ref[...]) + b2_ref[...]))
        h = jnp.sin(f3_ref[...] * (_dot3(h, w3_ref[...]) + b3_ref[...]))
        hh_ref[...] = h.astype(BF16)

    wh = w4_ref[...].astype(BF16)
    ct = wh.shape[1]
    k = _dot(hh_ref[...], wh).reshape(ha, sub, ct)
    a3 = lax.broadcasted_iota(jnp.int32, (ha, sub, ct), 0)
    b3 = lax.broadcasted_iota(jnp.int32, (ha, sub, ct), 1)
    tpos = a3 * nb + (b3 + g * sub)
    k = k * jnp.exp(-(tpos.astype(F32) * (1.0 / (L - 1))) * dl_ref[...])
    k = jnp.where(jnp.logical_and((j // cb) % 2 == 1, tpos == 0), 0.0, k).reshape(rows, ct)
    nrm_ref[0] = jnp.sum(jnp.abs(k), axis=0, keepdims=True)
    q = _dot(m8_ref[0], k.astype(BF16))
    a_ref[:, 0, :, :] = q.astype(a_ref.dtype).reshape(a_ref.shape[0], 2 * sub, ct)


def filter_stage1(L, w1, b1, f1, w2, b2, f2, w3, b3, f3, w4, m8):
    n, na, nb, kp = _fft_dims(L)
    ha, ng = na // 2, nb // SUB
    c = HYENA_WIDTH
    hdim = FILTER_HIDDEN
    t = jnp.linspace(0.0, 1.0, L, dtype=F32)[:, None]
    omega = 2.0 * math.pi * jnp.arange(L, dtype=F32)[:, None] / L
    bands = jnp.linspace(1e-4, POS_BANDS - 1, POS_BANDS, dtype=F32)[None, :]
    z = jnp.concatenate([t, jnp.cos(bands * omega), -jnp.sin(bands * omega)], axis=-1)
    z = jnp.zeros((L, LANES), F32).at[:, :POS_EMB_DIM].set(z).reshape(ha, nb, LANES)
    w1p = jnp.zeros((LANES, hdim), F32).at[:POS_EMB_DIM].set(w1)
    deltas = jnp.abs(jnp.linspace(MIN_DECAY, MAX_DECAY, c, dtype=F32)).reshape(1, c)
    nc = w4.shape[1]
    cb = c // CT_G
    row = lambda v: v.reshape(1, hdim)
    full = lambda shape: pl.BlockSpec(shape, lambda g, j: (0,) * len(shape))
    kern = functools.partial(_filter_s1_kernel, nb=nb, L=L, cb=cb)
    return pl.pallas_call(
        kern,
        out_shape=(jax.ShapeDtypeStruct((kp, ng, 2 * SUB, nc), BF16),
                   jax.ShapeDtypeStruct((ng, 1, nc), F32)),
        grid=(ng, nc // CT_G),
        in_specs=[pl.BlockSpec((ha, SUB, LANES), lambda g, j: (0, g, 0)),
                  full((LANES, hdim)), full((1, hdim)), full((1, hdim)),
                  full((hdim, hdim)), full((1, hdim)), full((1, hdim)),
                  full((hdim, hdim)), full((1, hdim)), full((1, hdim)),
                  pl.BlockSpec((hdim, CT_G), lambda g, j: (0, j)),
                  pl.BlockSpec((1, CT_G), lambda g, j: (0, j % cb)),
                  pl.BlockSpec((1, kp * 2 * SUB, ha * SUB), lambda g, j: (g, 0, 0))],
        out_specs=(pl.BlockSpec((kp, 1, 2 * SUB, CT_G), lambda g, j: (0, g, 0, j)),
                   pl.BlockSpec((1, 1, CT_G), lambda g, j: (g, 0, j))),
        scratch_shapes=[pltpu.VMEM((ha * SUB, hdim), BF16)],
        compiler_params=_params("arbitrary", "arbitrary"),
        name="hyena_filter_stage1",
    )(z, w1p, row(b1), row(f1), w2, row(b2), row(f2), w3, row(b3), row(f3), w4, deltas, m8)


def _slab(ref, s):
    _, ng, rows, ct = ref.shape
    return ref[s].reshape(ng * rows, ct)


def _filter_spec_kernel(af_ref, ab_ref, f2_ref, nf_ref, nbk_ref, h_ref):
    f2 = f2_ref[...]
    nb = f2.shape[0] // 2
    inv = 1.0 / (jnp.sum(nf_ref[...], axis=0) + jnp.sum(nbk_ref[...], axis=0))
    for s in range(af_ref.shape[0]):
        xf = _dot(f2, _slab(af_ref, s))
        xb = _dot(f2, _slab(ab_ref, s))
        h_ref[0, s] = ((xf[:nb] + xb[:nb]) * inv).astype(h_ref.dtype)
        h_ref[1, s] = ((xf[nb:] - xb[nb:]) * inv).astype(h_ref.dtype)


def filter_spectrum(a, nrm, f2, L):
    n, na, nb, kp = _fft_dims(L)
    ng = nb // SUB
    c = HYENA_WIDTH
    cb = c // CT
    ms = _slabs_per_step(kp)
    fcol = lambda j: (j // cb) * 2 * cb + j % cb
    return pl.pallas_call(
        _filter_spec_kernel,
        out_shape=jax.ShapeDtypeStruct((2, kp, nb, HYENA_ORDER * c), BF16),
        grid=(HYENA_ORDER * cb, kp // ms),
        in_specs=[pl.BlockSpec((ms, ng, 2 * SUB, CT), lambda j, k: (k, 0, 0, fcol(j))),
                  pl.BlockSpec((ms, ng, 2 * SUB, CT), lambda j, k: (k, 0, 0, fcol(j) + cb)),
                  pl.BlockSpec((2 * nb, 2 * nb), lambda j, k: (0, 0)),
                  pl.BlockSpec((ng, 1, CT), lambda j, k: (0, 0, fcol(j))),
                  pl.BlockSpec((ng, 1, CT), lambda j, k: (0, 0, fcol(j) + cb))],
        out_specs=pl.BlockSpec((2, ms, nb, CT), lambda j, k: (0, k, 0, j)),
        compiler_params=_params("parallel", "parallel"),
        name="hyena_filter_spectrum",
    )(a, a, f2, nrm, nrm)


def _fft_mid_kernel(a_ref, h_ref, f2_ref, g_ref, q_ref):
    f2 = f2_ref[...]
    nb = f2.shape[0] // 2
    slabs, ng, rows, ct = q_ref.shape
    for s in range(slabs):
        x = _dot(f2, _slab(a_ref, s))
        xr, xi = x[:nb], x[nb:]
        hr, hi = h_ref[0, s].astype(F32), h_ref[1, s].astype(F32)
        y = jnp.concatenate([xr * hr - xi * hi, xr * hi + xi * hr], axis=0).astype(BF16)
        q_ref[s] = _dot(g_ref[s], y).astype(q_ref.dtype).reshape(ng, rows, ct)


def fft_mid(a, hf, order, f2, g, L):
    n, na, nb, kp = _fft_dims(L)
    ng = nb // SUB
    c = a.shape[3]
    cb = c // CT
    ms = _slabs_per_step(kp)
    return pl.pallas_call(
        _fft_mid_kernel,
        out_shape=jax.ShapeDtypeStruct(a.shape, BF16),
        grid=(kp // ms, cb),
        in_specs=[pl.BlockSpec((ms, ng, 2 * SUB, CT), lambda k, j: (k, 0, 0, j)),
                  pl.BlockSpec((2, ms, nb, CT), lambda k, j: (0, k, 0, order * cb + j)),
                  pl.BlockSpec((2 * nb, 2 * nb), lambda k, j: (0, 0)),
                  pl.BlockSpec((ms, 2 * nb, 2 * nb), lambda k, j: (k, 0, 0))],
        out_specs=pl.BlockSpec((ms, ng, 2 * SUB, CT), lambda k, j: (k, 0, 0, j)),
        compiler_params=_params("parallel", "parallel"),
        name="fft_mid",
    )(a, hf, f2, g)


def _fft_i2_kernel(q_ref, p8_ref, u_ref, x_ref, s_ref, o_ref):
    kp, _, rows, ct = q_ref.shape
    q = q_ref[:, 0, :, :].reshape(kp * rows, ct)
    y = _dot(p8_ref[...], q).reshape(o_ref.shape)
    o_ref[...] = x_ref[...] * (y + u_ref[...] * s_ref[...])


def fft_stage_out(q, u, gate_src, gate_blk, skip, p8, L):
    n, na, nb, kp = _fft_dims(L)
    ha, ng = na // 2, nb // SUB
    c = q.shape[3]
    u3 = u.reshape(ha, nb, u.shape[1])
    g3 = gate_src.reshape(ha, nb, gate_src.shape[1])
    out = pl.pallas_call(
        _fft_i2_kernel,
        out_shape=jax.ShapeDtypeStruct((ha, nb, c), F32),
        grid=(ng, c // CT_G),
        in_specs=[pl.BlockSpec((kp, 1, 2 * SUB, CT_G), lambda g, j: (0, g, 0, j)),
                  pl.BlockSpec((ha * SUB, kp * 2 * SUB), lambda g, j: (0, 0)),
                  pl.BlockSpec((ha, SUB, CT_G), lambda g, j: (0, g, j)),
                  pl.BlockSpec((ha, SUB, CT_G), lambda g, j: (0, g, gate_blk + j)),
                  pl.BlockSpec((1, 1, CT_G), lambda g, j: (0, 0, j))],
        out_specs=pl.BlockSpec((ha, SUB, CT_G), lambda g, j: (0, g, j)),
        compiler_params=_params("parallel", "parallel"),
        name="fft_stage_out",
    )(q, p8, u3, g3, skip.reshape(1, 1, c))
    return out.reshape(L, c)


def hyena_mixer(u_hy, conv_w, conv_b, filt, skip, tables):
    L = u_hy.shape[0]
    c = HYENA_WIDTH
    m8, f2, g, p8 = tables
    uc = short_conv(u_hy, conv_w, conv_b)
    a_filt, nrm = filter_stage1(L, *filt, m8)
    hf = filter_spectrum(a_filt, nrm, f2, L)
    z = uc
    for order in range(HYENA_ORDER):
        a = fft_stage1(z, c, m8, L)
        q = fft_mid(a, hf, order, f2, g, L)
        z = fft_stage_out(q, z, uc, (order + 1) * (c // CT_G), skip[order], p8, L)
    return z


MOE_TM = 512
MOE_TN = 512
COMBINE_TM = 256


def _moe_plan(route, L):
    tm = MOE_TM
    nt = -(-(2 * L + N_EXPERTS * (tm - 1)) // tm)
    e_flat = jnp.concatenate([route[:, ROUTE_E1], route[:, ROUTE_E2]]).astype(jnp.int32)
    onehot = (e_flat[:, None] == jnp.arange(N_EXPERTS, dtype=jnp.int32)[None, :]).astype(jnp.int32)
    csum = jnp.cumsum(onehot, axis=0)
    rank = jnp.sum(onehot * (csum - 1), axis=1)
    counts = csum[-1]
    padded = ((counts + tm - 1) // tm) * tm
    ends = jnp.cumsum(padded)
    dest = (ends - padded)[e_flat] + rank
    tok = jnp.tile(jnp.arange(L, dtype=jnp.int32), 2)
    row_token = jnp.zeros((nt * tm,), jnp.int32).at[dest].set(tok, unique_indices=True)
    n_used = ends[-1] // tm
    tile_row = jnp.arange(nt, dtype=jnp.int32) * tm
    tile_expert = jnp.sum((tile_row[:, None] >= ends[None, :]).astype(jnp.int32), axis=1)
    last = jnp.take(tile_expert, jnp.maximum(n_used - 1, 0))
    tile_expert = jnp.where(jnp.arange(nt) < n_used, tile_expert, last)
    tile_first = jnp.concatenate([jnp.ones((1,), jnp.int32),
                                  (tile_expert[1:] != tile_expert[:-1]).astype(jnp.int32)])
    tile_group = jnp.cumsum(tile_first) - 1
    group_expert = jnp.zeros((N_EXPERTS,), jnp.int32).at[tile_group].set(tile_expert)
    n_groups = jnp.take(tile_group, jnp.maximum(n_used - 1, 0)) + 1
    i32 = lambda v: v.reshape(1).astype(jnp.int32)
    return dict(nt=nt, row_token=row_token, dest=dest,
                scalars=(tile_expert, tile_first, i32(n_used), tile_group.astype(jnp.int32),
                         i32(n_groups), group_expert))


def _row_copy(src_hbm, idx, buf, slot, r, sem):
    return pltpu.make_async_copy(src_hbm.at[pl.ds(idx, 1)], buf.at[slot, pl.ds(r, 1)],
                                 sem.at[slot])


def _gather_rows(idx_ref, src_hbm, buf, slot, sem, n_rows, start):
    if not start:
        pltpu.make_async_copy(src_hbm.at[pl.ds(0, n_rows)], buf.at[slot], sem.at[slot]).wait()
        return

    def body(r, carry):
        _row_copy(src_hbm, idx_ref[0, 0, r], buf, slot, r, sem).start(priority=1)
        return carry
    lax.fori_loop(0, n_rows, body, 0, unroll=8)


def _pipelined_gather(cur_ref, nxt_ref, src_hbm, buf, sem, n_rows):
    i, n = pl.program_id(0), pl.num_programs(0)
    slot = lax.rem(i, 2)

    @pl.when(i == 0)
    def _():
        _gather_rows(cur_ref, src_hbm, buf, 0, sem, n_rows, True)

    @pl.when(i + 1 < n)
    def _():
        _gather_rows(nxt_ref, src_hbm, buf, 1 - slot, sem, n_rows, True)

    _gather_rows(cur_ref, src_hbm, buf, slot, sem, n_rows, False)
    return slot


def _dispatch_kernel(cur_ref, nxt_ref, h_hbm, o_ref, buf, sem):
    slot = _pipelined_gather(cur_ref, nxt_ref, h_hbm, buf, sem, MOE_TM)
    o_ref[...] = buf[slot].astype(o_ref.dtype)


def moe_dispatch(h, row_token, nt):
    L, d = h.shape
    idx = row_token.reshape(nt, 1, MOE_TM)
    smem = lambda f: pl.BlockSpec((1, 1, MOE_TM), f, memory_space=pltpu.SMEM)
    return pl.pallas_call(
        _dispatch_kernel,
        out_shape=jax.ShapeDtypeStruct((nt * MOE_TM, d), BF16),
        grid=(nt,),
        in_specs=[smem(lambda i: (i, 0, 0)),
                  smem(lambda i: (jnp.minimum(i + 1, nt - 1), 0, 0)),
                  pl.BlockSpec(memory_space=pl.ANY)],
        out_specs=pl.BlockSpec((MOE_TM, d), lambda i: (i, 0)),
        scratch_shapes=[pltpu.VMEM((2, MOE_TM, d), F32), pltpu.SemaphoreType.DMA((2,))],
        compiler_params=_params("arbitrary"),
        name="moe_dispatch",
    )(idx, idx, h)


def _weight_copy(w_hbm, e, jcol, wbuf, k, slot, sem):
    tn = wbuf.shape[3]
    return pltpu.make_async_copy(w_hbm.at[e, :, pl.ds(pl.multiple_of(jcol * tn, tn), tn)],
                                 wbuf.at[k, slot], sem.at[k, slot])


def _refresh_expert_weights(scalars, w_hbms, wbuf, wbs, sem):
    te_ref, tf_ref, _, tg_ref, ng_ref, ge_ref = scalars
    j, i = pl.program_id(0), pl.program_id(1)
    nj = pl.num_programs(0)

    @pl.when(tf_ref[i] == 1)
    def _():
        g, ng = tg_ref[i], ng_ref[0]
        blk = j * ng + g
        slot = lax.rem(blk, 2)

        @pl.when(blk == 0)
        def _():
            for k, w in enumerate(w_hbms):
                _weight_copy(w, te_ref[i], j, wbuf, k, 0, sem).start()

        for k, w in enumerate(w_hbms):
            _weight_copy(w, 0, 0, wbuf, k, slot, sem).wait()
        for k, wb in enumerate(wbs):
            wb[...] = wbuf[k, slot].astype(BF16)

        wrap = g + 1 == ng
        nxt_e = ge_ref[jnp.where(wrap, 0, g + 1)]
        nxt_j = j + wrap.astype(jnp.int32)

        @pl.when(nxt_j < nj)
        def _():
            for k, w in enumerate(w_hbms):
                _weight_copy(w, nxt_e, nxt_j, wbuf, k, 1 - slot, sem).start()


def _moe_up_kernel(te_ref, tf_ref, nu_ref, tg_ref, ng_ref, ge_ref, a_ref, wg_hbm, wu_hbm, o_ref,
                   wbuf, wgb, wub, sem):
    i = pl.program_id(1)
    _refresh_expert_weights((te_ref, tf_ref, nu_ref, tg_ref, ng_ref, ge_ref),
                            (wg_hbm, wu_hbm), wbuf, (wgb, wub), sem)

    @pl.when(i < nu_ref[0])
    def _():
        a = a_ref[...]
        o_ref[...] = (_silu(_dot(a, wgb[...])) * _dot(a, wub[...])).astype(o_ref.dtype)

    @pl.when(i >= nu_ref[0])
    def _():
        o_ref[...] = jnp.zeros_like(o_ref)


def moe_up(hs, w_gate, w_up, plan):
    r, d = hs.shape
    f = w_gate.shape[2]
    nt = plan["nt"]
    anyspec = pl.BlockSpec(memory_space=pl.ANY)
    return pl.pallas_call(
        _moe_up_kernel,
        out_shape=jax.ShapeDtypeStruct((r, f), BF16),
        grid_spec=pltpu.PrefetchScalarGridSpec(
            num_scalar_prefetch=6,
            grid=(f // MOE_TN, nt),
            in_specs=[pl.BlockSpec((MOE_TM, d), lambda j, i, *_: (i, 0)), anyspec, anyspec],
            out_specs=pl.BlockSpec((MOE_TM, MOE_TN), lambda j, i, *_: (i, j)),
            scratch_shapes=[pltpu.VMEM((2, 2, d, MOE_TN), F32),
                            pltpu.VMEM((d, MOE_TN), BF16), pltpu.VMEM((d, MOE_TN), BF16),
                            pltpu.SemaphoreType.DMA((2, 2))]),
        compiler_params=_params("arbitrary", "arbitrary"),
        name="moe_up",
    )(*plan["scalars"], hs, w_gate, w_up)


def _moe_down_kernel(te_ref, tf_ref, nu_ref, tg_ref, ng_ref, ge_ref, t_ref, wd_hbm, o_ref,
                     wbuf, wdb, sem):
    i = pl.program_id(1)
    _refresh_expert_weights((te_ref, tf_ref, nu_ref, tg_ref, ng_ref, ge_ref),
                            (wd_hbm,), wbuf, (wdb,), sem)

    @pl.when(i < nu_ref[0])
    def _():
        o_ref[...] = _dot(t_ref[...], wdb[...])

    @pl.when(i >= nu_ref[0])
    def _():
        o_ref[...] = jnp.zeros_like(o_ref)


def moe_down(t, w_down, plan):
    r, f = t.shape
    d = w_down.shape[2]
    nt = plan["nt"]
    return pl.pallas_call(
        _moe_down_kernel,
        out_shape=jax.ShapeDtypeStruct((r, d), F32),
        grid_spec=pltpu.PrefetchScalarGridSpec(
            num_scalar_prefetch=6,
            grid=(d // MOE_TN, nt),
            in_specs=[pl.BlockSpec((MOE_TM, f), lambda j, i, *_: (i, 0)),
                      pl.BlockSpec(memory_space=pl.ANY)],
            out_specs=pl.BlockSpec((MOE_TM, MOE_TN), lambda j, i, *_: (i, j)),
            scratch_shapes=[pltpu.VMEM((1, 2, f, MOE_TN), F32), pltpu.VMEM((f, MOE_TN), BF16),
                            pltpu.SemaphoreType.DMA((1, 2))]),
        compiler_params=_params("arbitrary", "arbitrary"),
        name="moe_down",
    )(*plan["scalars"], t, w_down)


def _combine_kernel(cur_ref, nxt_ref, y_hbm, x_ref, route_ref, g_ref, o_ref, buf, sem, *, final):
    tm = x_ref.shape[0]
    slot = _pipelined_gather(cur_ref, nxt_ref, y_hbm, buf, sem, 2 * tm)
    route = route_ref[...]
    g1 = route[:, ROUTE_G1:ROUTE_G1 + 1]
    g2 = route[:, ROUTE_G2:ROUTE_G2 + 1]
    x = x_ref[...] + g1 * buf[slot, :tm, :] + g2 * buf[slot, tm:, :]
    if final:
        ms = jnp.mean(x * x, axis=-1, keepdims=True)
        x = x * lax.rsqrt(ms + RMS_EPS) * g_ref[...]
    o_ref[...] = x


def moe_combine(x, y, dest, route, final_gain):
    L, d = x.shape
    tm = COMBINE_TM
    nt = L // tm
    idx = jnp.concatenate([dest[:L].reshape(nt, 1, tm), dest[L:].reshape(nt, 1, tm)], axis=2)
    smem = lambda f: pl.BlockSpec((1, 1, 2 * tm), f, memory_space=pltpu.SMEM)
    final = final_gain is not None
    gain = (final_gain if final else jnp.ones((d,), F32)).reshape(1, d)
    kern = functools.partial(_combine_kernel, final=final)
    return pl.pallas_call(
        kern,
        out_shape=jax.ShapeDtypeStruct((L, d), F32),
        grid=(nt,),
        in_specs=[smem(lambda i: (i, 0, 0)),
                  smem(lambda i: (jnp.minimum(i + 1, nt - 1), 0, 0)),
                  pl.BlockSpec(memory_space=pl.ANY),
                  pl.BlockSpec((tm, d), lambda i: (i, 0)),
                  pl.BlockSpec((tm, LANES), lambda i: (i, 0)),
                  pl.BlockSpec((1, d), lambda i: (0, 0))],
        out_specs=pl.BlockSpec((tm, d), lambda i: (i, 0)),
        scratch_shapes=[pltpu.VMEM((2, 2 * tm, d), F32), pltpu.SemaphoreType.DMA((2,))],
        compiler_params=_params("arbitrary"),
        name="moe_combine",
    )(idx, idx, y, x, route, gain)


def moe_ffn(x, norm_g, router_w, w_gate, w_up, w_down, final_gain):
    L = x.shape[0]
    h, route = rmsnorm_router(x, norm_g, router_w)
    plan = _moe_plan(route, L)
    hs = moe_dispatch(h, plan["row_token"], plan["nt"])
    t = moe_up(hs, w_gate, w_up, plan)
    y = moe_down(t, w_down, plan)
    return moe_combine(x, y, plan["dest"], route, final_gain)


def _layer_mixers(x, l, tables, norm_mix, w_in_t, conv_w, conv_b, filt, hyena_skip, w_branch_a,
                  gla_gate_w2, gla_gate_b, gla_norm, w_branch_b, w_out):
    hc = (HYENA_ORDER + 1) * HYENA_WIDTH
    o_lr = hc + 2 * GLA_DK_TOTAL + 2 * GLA_DV_TOTAL
    o_g = o_lr + 2 * GLA_LOWRANK
    h = rmsnorm(x, norm_mix[l], BF16)
    proj = matmul([h], [(w_in_t, l, 0)], o_lr, [(0, 0)], _ep_plain, [], F32,
                  tm=1024, tn=1024, name="in_proj", w_transposed=True)
    gates = matmul([h], [(w_in_t[l, o_g:][None], 0, 0)], 2 * D_MODEL, [(0, 0)], _ep_sigmoid, [],
                   BF16, tm=1024, tn=1024, name="in_proj_gates", w_transposed=True)
    la = gla_gates(h, w_in_t[l, o_lr:o_g], gla_gate_w2[l], gla_gate_b[l])

    z_a = hyena_mixer(proj, conv_w[l], conv_b[l], filt, hyena_skip[l], tables)
    o_f, o_b = gla_scan(proj, hc, la)
    z_b = gla_post(o_f, o_b, proj, hc + 2 * GLA_DK_TOTAL + GLA_DV_TOTAL, gla_norm[l])

    mixed = matmul([z_a, z_b], [(w_branch_a, l, 0), (w_branch_b, l, 0)], D_MODEL,
                   [(0, 0), (1, 1)], _ep_merge, [(gates, 0), (gates, D_MODEL)], BF16,
                   tm=1024, tn=1024, name="branch_merge")
    return matmul([mixed], [(w_out, l, 0)], D_MODEL, [(0, 0)], _ep_residual, [(x, 0)], F32,
                  tm=1024, tn=512, name="out_proj")


def _swiglu_ffn(x, h, wg, wu, wd, i):
    f = wg.shape[2]
    t = matmul([h], [(wg, i, 0), (wu, i, 0)], f, [(0, 0), (0, 1)], _ep_swiglu, [], BF16,
               tm=1024, tn=512, name="ffn_up")
    return matmul([t], [(wd, i, 0)], wd.shape[2], [(0, 0)], _ep_residual, [(x, 0)], F32,
                  tm=512, tn=512, name="ffn_down")


def kernel(x, norm_mix, w_in, conv_w, conv_b, filt_w1, filt_b1, filt_freq1, filt_w2, filt_b2, filt_freq2, filt_w3, filt_b3, filt_freq3, filt_w4, hyena_skip, w_branch_a, gla_gate_w2, gla_gate_b, gla_norm, w_branch_b, w_out, norm_ffn, dense_w_gate, dense_w_up, dense_w_down, router_w, moe_w_gate, moe_w_up, moe_w_down, norm_final):
    b, L, d = x.shape
    assert b == 1
    x = x.reshape(L, d)
    tables = _fft_tables(L)
    w_in_t = jnp.swapaxes(w_in, 1, 2)
    for l in range(DEPTH):
        filt = (filt_w1[l], filt_b1[l], filt_freq1[l], filt_w2[l], filt_b2[l], filt_freq2[l],
                filt_w3[l], filt_b3[l], filt_freq3[l], filt_w4[l])
        x = _layer_mixers(x, l, tables, norm_mix, w_in_t, conv_w, conv_b, filt, hyena_skip,
                          w_branch_a, gla_gate_w2, gla_gate_b, gla_norm, w_branch_b, w_out)
        i = l // 2
        last = l == DEPTH - 1
        if l % 2 == 0:
            h = rmsnorm(x, norm_ffn[l], BF16)
            x = _swiglu_ffn(x, h, dense_w_gate, dense_w_up, dense_w_down, i)
            if last:
                x = rmsnorm(x, norm_final, F32)
        else:
            x = moe_ffn(x, norm_ffn[l], router_w[i], moe_w_gate[i], moe_w_up[i], moe_w_down[i],
                        norm_final if last else None)
    return x.reshape(b, L, d)
```

```python
import functools
import math

import jax
import jax.numpy as jnp
from jax import lax
from jax.experimental import pallas as pl
from jax.experimental.pallas import tpu as pltpu

F32 = jnp.float32
BF16 = jnp.bfloat16

D_MODEL = 2048
DEPTH = 2
HYENA_WIDTH = D_MODEL // 2
HYENA_ORDER = 2
SHORT_CONV = 3
POS_EMB_DIM = 33
POS_BANDS = (POS_EMB_DIM - 1) // 2
FILTER_HIDDEN = 64
MIN_DECAY = math.log(1e-2) / 1.5
MAX_DECAY = math.log(1e-2) / 0.3
GLA_HEADS = 4
GLA_DK = 128
GLA_DV = 256
GLA_DK_TOTAL = GLA_HEADS * GLA_DK
GLA_DV_TOTAL = GLA_HEADS * GLA_DV
GLA_LOWRANK = 16
GLA_GATE_TAU = 16.0
GLA_CHUNK = 64
N_EXPERTS = 8
RMS_EPS = 1e-6

LANES = 128
VMEM_LIMIT_BYTES = 56 * 1024 * 1024


def _params(*sem):
    return pltpu.CompilerParams(dimension_semantics=sem, vmem_limit_bytes=VMEM_LIMIT_BYTES)


def _dot(a, b):
    return jnp.dot(a, b, preferred_element_type=F32)


def _split(a):
    hi = a.astype(BF16)
    lo = (a - hi.astype(F32)).astype(BF16)
    return hi, lo


def _dot3(a, b):
    ah, al = _split(a)
    bh, bl = _split(b)
    return _dot(ah, bh) + (_dot(ah, bl) + _dot(al, bh))


def _sigmoid(x):
    return 1.0 / (1.0 + jnp.exp(-x))


def _silu(x):
    return x * _sigmoid(x)


def _log_sigmoid(x):
    return jnp.minimum(x, 0.0) - jnp.log(1.0 + jnp.exp(-jnp.abs(x)))


def _rmsnorm_kernel(x_ref, g_ref, o_ref):
    x = x_ref[...]
    ms = jnp.mean(x * x, axis=-1, keepdims=True)
    o_ref[...] = (x * lax.rsqrt(ms + RMS_EPS) * g_ref[...]).astype(o_ref.dtype)


def rmsnorm(x, g, out_dtype, tm=512):
    m, d = x.shape
    return pl.pallas_call(
        _rmsnorm_kernel,
        out_shape=jax.ShapeDtypeStruct((m, d), out_dtype),
        grid=(m // tm,),
        in_specs=[pl.BlockSpec((tm, d), lambda i: (i, 0)),
                  pl.BlockSpec((1, d), lambda i: (0, 0))],
        out_specs=pl.BlockSpec((tm, d), lambda i: (i, 0)),
        compiler_params=_params("parallel"),
        name="rmsnorm",
    )(x, g.reshape(1, d))


def _rmsnorm_router_kernel(x_ref, g_ref, rw_ref, o_ref, cw_ref):
    x = x_ref[...]
    ms = jnp.mean(x * x, axis=-1, keepdims=True)
    h = x * lax.rsqrt(ms + RMS_EPS) * g_ref[...]
    o_ref[...] = h.astype(o_ref.dtype)
    logits = _dot3(h, rw_ref[...])
    col = lax.broadcasted_iota(jnp.int32, logits.shape, 1).astype(F32)
    neg = jnp.float32(-jnp.inf)
    logits = jnp.where(col < N_EXPERTS, logits, neg)
    m1 = jnp.max(logits, axis=-1, keepdims=True)
    i1 = jnp.min(jnp.where(logits == m1, col, float(LANES)), axis=-1, keepdims=True)
    rest = jnp.where(col == i1, neg, logits)
    m2 = jnp.max(rest, axis=-1, keepdims=True)
    i2 = jnp.min(jnp.where(rest == m2, col, float(LANES)), axis=-1, keepdims=True)
    e2 = jnp.exp(m2 - m1)
    g1 = 1.0 / (1.0 + e2)
    g2 = e2 / (1.0 + e2)
    cw_ref[...] = (jnp.where(col == ROUTE_E1, i1, 0.0) + jnp.where(col == ROUTE_E2, i2, 0.0)
                   + jnp.where(col == ROUTE_G1, g1, 0.0) + jnp.where(col == ROUTE_G2, g2, 0.0))


ROUTE_E1, ROUTE_E2, ROUTE_G1, ROUTE_G2 = 0, 1, 2, 3


def rmsnorm_router(x, g, router_w, tm=256):
    m, d = x.shape
    rw = jnp.zeros((d, LANES), F32).at[:, :N_EXPERTS].set(router_w)
    return pl.pallas_call(
        _rmsnorm_router_kernel,
        out_shape=(jax.ShapeDtypeStruct((m, d), F32), jax.ShapeDtypeStruct((m, LANES), F32)),
        grid=(m // tm,),
        in_specs=[pl.BlockSpec((tm, d), lambda i: (i, 0)),
                  pl.BlockSpec((1, d), lambda i: (0, 0)),
                  pl.BlockSpec((d, LANES), lambda i: (0, 0))],
        out_specs=(pl.BlockSpec((tm, d), lambda i: (i, 0)),
                   pl.BlockSpec((tm, LANES), lambda i: (i, 0))),
        compiler_params=_params("parallel"),
        name="rmsnorm_router",
    )(x, g.reshape(1, d), rw)


def _mm_kernel(*refs, n_a, n_w, dots, n_extra, epilogue, w_transposed):
    a_refs = refs[:n_a]
    w_refs = refs[n_a:n_a + n_w]
    e_refs = refs[n_a + n_w:n_a + n_w + n_extra]
    o_ref = refs[n_a + n_w + n_extra]
    wb_refs = refs[n_a + n_w + n_extra + 1:]

    @pl.when(pl.program_id(1) == 0)
    def _():
        for w, wb in zip(w_refs, wb_refs):
            wb[...] = w[0].astype(BF16)

    a_vals = [a[...].astype(BF16) for a in a_refs]
    nt_dims = (((1,), (1,)), ((), ()))
    accs = [lax.dot_general(a_vals[ai], wb_refs[wi][...], nt_dims, preferred_element_type=F32)
            if w_transposed else _dot(a_vals[ai], wb_refs[wi][...]) for ai, wi in dots]
    o_ref[...] = epilogue(accs, [e[...] for e in e_refs]).astype(o_ref.dtype)


def matmul(a_list, w_list, n, dots, epilogue, extras, out_dtype, *, tm, tn, name,
           w_transposed=False):
    m = a_list[0].shape[0]
    assert m % tm == 0 and n % tn == 0
    in_specs = [pl.BlockSpec((tm, a.shape[1]), lambda j, i: (i, 0)) for a in a_list]
    w_arrays, scratch = [], []
    for w, s, first in w_list:
        assert first % tn == 0
        w_arrays.append(w)
        if w_transposed:
            in_specs.append(pl.BlockSpec((1, tn, w.shape[2]),
                                         lambda j, i, s=s, off=first // tn: (s, off + j, 0)))
            scratch.append(pltpu.VMEM((tn, w.shape[2]), BF16))
            continue
        in_specs.append(pl.BlockSpec((1, w.shape[1], tn),
                                     lambda j, i, s=s, off=first // tn: (s, 0, off + j)))
        scratch.append(pltpu.VMEM((w.shape[1], tn), BF16))
    extra_arrays = []
    for arr, first in extras:
        assert first % tn == 0
        extra_arrays.append(arr)
        in_specs.append(pl.BlockSpec((tm, tn), lambda j, i, off=first // tn: (i, off + j)))
    kern = functools.partial(_mm_kernel, n_a=len(a_list), n_w=len(w_list), dots=tuple(dots),
                             n_extra=len(extras), epilogue=epilogue, w_transposed=w_transposed)
    return pl.pallas_call(
        kern,
        out_shape=jax.ShapeDtypeStruct((m, n), out_dtype),
        grid=(n // tn, m // tm),
        in_specs=in_specs,
        out_specs=pl.BlockSpec((tm, tn), lambda j, i: (i, j)),
        scratch_shapes=scratch,
        compiler_params=_params("arbitrary", "arbitrary"),
        name=name,
    )(*a_list, *w_arrays, *extra_arrays)


def _ep_plain(accs, ex):
    return accs[0]


def _ep_sigmoid(accs, ex):
    return _sigmoid(accs[0])


def _ep_residual(accs, ex):
    return ex[0] + accs[0]


def _ep_merge(accs, ex):
    return ex[0].astype(F32) * accs[0] + ex[1].astype(F32) * accs[1]


def _ep_swiglu(accs, ex):
    return _silu(accs[0]) * accs[1]


def _gate_kernel(h_ref, wlr_ref, w2_ref, b_ref, o_ref):
    lr = lax.dot_general(h_ref[...], wlr_ref[...].astype(BF16), (((1,), (1,)), ((), ())),
                         preferred_element_type=F32)
    z = _dot3(lr, w2_ref[...]) + b_ref[...]
    o_ref[...] = _log_sigmoid(z) * (1.0 / GLA_GATE_TAU)


def gla_gates(h, w_lr_t, gate_w2, gate_b, tm=512):
    m, d = h.shape
    r = GLA_LOWRANK
    n = 2 * GLA_DK_TOTAL
    wlr = jnp.zeros((LANES, d), F32).at[:2 * r].set(w_lr_t)
    w2 = jnp.zeros((LANES, n), F32)
    w2 = w2.at[:r, :GLA_DK_TOTAL].set(gate_w2[0]).at[r:2 * r, GLA_DK_TOTAL:].set(gate_w2[1])
    b = gate_b.reshape(1, n)
    return pl.pallas_call(
        _gate_kernel,
        out_shape=jax.ShapeDtypeStruct((m, n), F32),
        grid=(m // tm,),
        in_specs=[pl.BlockSpec((tm, d), lambda i: (i, 0)),
                  pl.BlockSpec((LANES, d), lambda i: (0, 0)),
                  pl.BlockSpec((LANES, n), lambda i: (0, 0)),
                  pl.BlockSpec((1, n), lambda i: (0, 0))],
        out_specs=pl.BlockSpec((tm, n), lambda i: (i, 0)),
        compiler_params=_params("parallel"),
        name="gla_gates",
    )(h, wlr, w2, b)


GLA_SUB = 4


def _gla_chunk(q, k, v, la, st_ref, h, reverse):
    c = GLA_CHUNK
    row = lax.broadcasted_iota(jnp.int32, (c, c), 0)
    col = lax.broadcasted_iota(jnp.int32, (c, c), 1)
    keep = (col > row) if reverse else (col <= row)
    b = la
    rows = lax.broadcasted_iota(jnp.int32, la.shape, 0)
    step = 1
    while step < c:
        if reverse:
            b = b + jnp.where(rows < c - step, pltpu.roll(b, c - step, 0), 0.0)
        else:
            b = b + jnp.where(rows >= step, pltpu.roll(b, step, 0), 0.0)
        step *= 2
    tot = b[0:1] if reverse else b[c - 1:c]
    q_in = (q * ((GLA_DK ** -0.5) * jnp.exp(b))).astype(BF16)
    k_dec = k * jnp.exp(-b)
    k_in = k_dec.astype(BF16)
    k_out = (k_dec * jnp.exp(tot)).astype(BF16)
    vb = v.astype(BF16)
    s = lax.dot_general(q_in, k_in, (((1,), (1,)), ((), ())), preferred_element_type=F32)
    s = jnp.where(keep, s, 0.0).astype(BF16)
    st = st_ref[h]
    o = _dot(s, vb) + lax.dot_general(q_in, st.astype(BF16), (((1,), (1,)), ((), ())),
                                      preferred_element_type=F32)
    upd = lax.dot_general(vb, k_out, (((0,), (0,)), ((), ())), preferred_element_type=F32)
    st_ref[h] = st * jnp.exp(tot) + upd
    return o


def _gla_kernel(qf_ref, kf_ref, vf_ref, lf_ref, qb_ref, kb_ref, vb_ref, lb_ref,
                of_ref, ob_ref, sf_ref, sb_ref):
    @pl.when(pl.program_id(0) == 0)
    def _():
        sf_ref[...] = jnp.zeros_like(sf_ref)
        sb_ref[...] = jnp.zeros_like(sb_ref)

    def body(s, carry):
        rf = pl.multiple_of(s * GLA_CHUNK, GLA_CHUNK)
        rb = pl.multiple_of((GLA_SUB - 1 - s) * GLA_CHUNK, GLA_CHUNK)
        for h in range(GLA_HEADS):
            kk = slice(h * GLA_DK, (h + 1) * GLA_DK)
            vv = slice(h * GLA_DV, (h + 1) * GLA_DV)
            of_ref[pl.ds(rf, GLA_CHUNK), vv] = _gla_chunk(
                qf_ref[pl.ds(rf, GLA_CHUNK), kk], kf_ref[pl.ds(rf, GLA_CHUNK), kk],
                vf_ref[pl.ds(rf, GLA_CHUNK), vv], lf_ref[pl.ds(rf, GLA_CHUNK), kk],
                sf_ref, h, False)
            ob_ref[pl.ds(rb, GLA_CHUNK), vv] = _gla_chunk(
                qb_ref[pl.ds(rb, GLA_CHUNK), kk], kb_ref[pl.ds(rb, GLA_CHUNK), kk],
                vb_ref[pl.ds(rb, GLA_CHUNK), vv], lb_ref[pl.ds(rb, GLA_CHUNK), kk],
                sb_ref, h, True)
        return carry

    lax.fori_loop(0, GLA_SUB, body, 0, unroll=True)


def gla_scan(qkvr, first, la):
    L = qkvr.shape[0]
    rows = GLA_CHUNK * GLA_SUB
    n = L // rows
    dk, dv = GLA_DK_TOTAL, GLA_DV_TOTAL
    assert first % dv == 0
    qb, vb = first // dk, (first + 2 * dk) // dv
    specs = []
    for rev in (False, True):
        blk = (lambda i: n - 1 - i) if rev else (lambda i: i)
        specs += [pl.BlockSpec((rows, dk), lambda i, blk=blk: (blk(i), qb)),
                  pl.BlockSpec((rows, dk), lambda i, blk=blk: (blk(i), qb + 1)),
                  pl.BlockSpec((rows, dv), lambda i, blk=blk: (blk(i), vb)),
                  pl.BlockSpec((rows, dk), lambda i, blk=blk, c=int(rev): (blk(i), c))]
    return pl.pallas_call(
        _gla_kernel,
        out_shape=(jax.ShapeDtypeStruct((L, dv), F32), jax.ShapeDtypeStruct((L, dv), F32)),
        grid=(n,),
        in_specs=specs,
        out_specs=(pl.BlockSpec((rows, dv), lambda i: (i, 0)),
                   pl.BlockSpec((rows, dv), lambda i: (n - 1 - i, 0))),
        scratch_shapes=[pltpu.VMEM((GLA_HEADS, GLA_DV, GLA_DK), F32),
                        pltpu.VMEM((GLA_HEADS, GLA_DV, GLA_DK), F32)],
        compiler_params=_params("arbitrary"),
        name="gla_scan",
    )(qkvr, qkvr, qkvr, la, qkvr, qkvr, qkvr, la)


def _gla_post_kernel(of_ref, ob_ref, r_ref, g_ref, o_ref):
    g = g_ref[...]
    for h in range(GLA_HEADS):
        vv = slice(h * GLA_DV, (h + 1) * GLA_DV)
        o = of_ref[:, vv] + ob_ref[:, vv]
        ms = jnp.mean(o * o, axis=-1, keepdims=True)
        y = o * lax.rsqrt(ms + RMS_EPS) * g
        o_ref[:, vv] = (y * _silu(r_ref[:, vv])).astype(o_ref.dtype)


def gla_post(o_f, o_b, qkvr, first, norm_g, tm=512):
    L, dv = o_f.shape
    assert first % dv == 0
    return pl.pallas_call(
        _gla_post_kernel,
        out_shape=jax.ShapeDtypeStruct((L, dv), BF16),
        grid=(L // tm,),
        in_specs=[pl.BlockSpec((tm, dv), lambda i: (i, 0)),
                  pl.BlockSpec((tm, dv), lambda i: (i, 0)),
                  pl.BlockSpec((tm, dv), lambda i: (i, first // dv)),
                  pl.BlockSpec((1, GLA_DV), lambda i: (0, 0))],
        out_specs=pl.BlockSpec((tm, dv), lambda i: (i, 0)),
        compiler_params=_params("parallel"),
        name="gla_post",
    )(o_f, o_b, qkvr, norm_g.reshape(1, GLA_DV))


CT = 256
CT_G = 1024
CT_COL = 128
MAX_SLABS = 16
SUB = 8


def _fft_dims(L):
    n = 2 * L
    na = 1 << ((n.bit_length() - 1) // 2)
    nb = n // na
    assert na * nb == n and nb % SUB == 0 and (na // 2) % 8 == 0
    return n, na, nb, na // 2 + 1


def _slabs_per_step(kp):
    return max(s for s in range(1, MAX_SLABS + 1) if kp % s == 0)


def _cis(num, den):
    ang = (2.0 * math.pi / den) * (num % den).astype(F32)
    return jnp.cos(ang), jnp.sin(ang)


def _expand_kernel(t_ref, o_ref):
    t = t_ref[0].astype(BF16)
    cc, cols = t.shape[1], o_ref.shape[2]
    shift, low = SUB.bit_length() - 1, SUB - 1
    src = lax.broadcasted_iota(jnp.int32, (cc, cols), 0)
    dst = lax.broadcasted_iota(jnp.int32, (cc, cols), 1)
    spread = jnp.where(lax.shift_right_logical(dst, shift) == src, 1.0, 0.0).astype(BF16)
    x = _dot(t, spread)
    row = lax.broadcasted_iota(jnp.int32, x.shape, 0)
    col = lax.broadcasted_iota(jnp.int32, x.shape, 1)
    o_ref[0] = jnp.where((col & low) == (row & low), x, 0.0).astype(o_ref.dtype)


def _expand_block_diag(t):
    g, r, c_in = t.shape
    c = -(-c_in // 16) * 16
    if c != c_in:
        return _expand_block_diag(jnp.pad(t, ((0, 0), (0, 0), (0, c - c_in))))[:, :, :c_in * SUB]
    return pl.pallas_call(
        _expand_kernel,
        out_shape=jax.ShapeDtypeStruct((g, r, c * SUB), BF16),
        grid=(g,),
        in_specs=[pl.BlockSpec((1, r, c), lambda i: (i, 0, 0))],
        out_specs=pl.BlockSpec((1, r, c * SUB), lambda i: (i, 0, 0)),
        compiler_params=_params("parallel"),
        name="dft_table_expand",
    )(t)


def _fft_tables(L):
    n, na, nb, kp = _fft_dims(L)
    ha, ng = na // 2, nb // SUB
    ar = lambda m: jnp.arange(m, dtype=jnp.int32)
    g_, ka_, bl_, a_ = ar(ng)[:, None, None, None], ar(kp)[None, :, None, None], \
        ar(SUB)[None, None, :, None], ar(ha)[None, None, None, :]
    c, s = _cis(ka_ * (nb * a_ + SUB * g_ + bl_), n)
    cs = jnp.stack([c, -s], axis=2)
    m8 = _expand_block_diag(cs.reshape(ng, kp * 2 * SUB, ha))
    c, s = _cis(ar(nb)[:, None] * ar(nb)[None, :], nb)
    f2 = jnp.stack([jnp.stack([c, s], axis=1), jnp.stack([-s, c], axis=1)], axis=0)
    f2 = f2.reshape(2, nb, 2, ng, SUB).transpose(0, 1, 3, 2, 4).reshape(2 * nb, 2 * nb)
    ca, sa = _cis(ar(kp)[:, None] * ar(nb)[None, :], n)
    cb, sb = _cis(ar(nb)[:, None] * ar(nb)[None, :], nb)
    ca, sa = (t.reshape(kp, ng, 1, SUB, 1, 1) / n for t in (ca, sa))
    cb, sb = (t.reshape(1, ng, 1, SUB, 1, nb) for t in (cb, sb))
    c, s = ca * cb - sa * sb, sa * cb + ca * sb
    gi = jnp.concatenate([jnp.concatenate([c, -s], axis=4),
                          jnp.concatenate([s, c], axis=4)], axis=2)
    gi = gi.reshape(kp, 2 * nb, 2 * nb)
    kc = ar(kp)[None, :]
    wgt = jnp.where((kc == 0) | (kc == ha), 1.0, jnp.where(kc < ha, 2.0, 0.0))
    c, s = _cis(ar(ha)[:, None] * kc, na)
    pm = jnp.stack([wgt * c, -wgt * s], axis=2).reshape(ha, 1, kp * 2)
    p8 = _expand_block_diag(jnp.broadcast_to(pm, (ha, SUB, kp * 2)).reshape(1, ha * SUB, kp * 2))[0]
    return m8, f2.astype(BF16), gi.astype(BF16), p8


def _short_conv_kernel(u_ref, w_ref, b_ref, o_ref):
    u = u_ref[...]
    L = u.shape[0]
    row = lax.broadcasted_iota(jnp.int32, u.shape, 0)
    prev = jnp.where(row == 0, 0.0, pltpu.roll(u, 1, 0))
    nxt = jnp.where(row == L - 1, 0.0, pltpu.roll(u, L - 1, 0))
    w = w_ref[...]
    o_ref[...] = prev * w[0:1] + u * w[1:2] + nxt * w[2:3] + b_ref[...]


def short_conv(u, w, b):
    L, c = u.shape[0], w.shape[1]
    w8 = jnp.zeros((8, c), F32).at[:SHORT_CONV].set(w)
    return pl.pallas_call(
        _short_conv_kernel,
        out_shape=jax.ShapeDtypeStruct((L, c), F32),
        grid=(c // CT_COL,),
        in_specs=[pl.BlockSpec((L, CT_COL), lambda j: (0, j)),
                  pl.BlockSpec((8, CT_COL), lambda j: (0, j)),
                  pl.BlockSpec((1, CT_COL), lambda j: (0, j))],
        out_specs=pl.BlockSpec((L, CT_COL), lambda j: (0, j)),
        compiler_params=_params("parallel"),
        name="short_conv",
    )(u, w8, b.reshape(1, c))


def _fft_s1_kernel(u_ref, m8_ref, a_ref):
    ha, sub, ct = u_ref.shape
    x = u_ref[...].reshape(ha * sub, ct).astype(BF16)
    r = _dot(m8_ref[0], x)
    a_ref[:, 0, :, :] = r.astype(a_ref.dtype).reshape(a_ref.shape[0], 2 * sub, ct)


def fft_stage1(u, ncols, m8, L):
    n, na, nb, kp = _fft_dims(L)
    ha, ng = na // 2, nb // SUB
    u3 = u.reshape(ha, nb, u.shape[1])
    return pl.pallas_call(
        _fft_s1_kernel,
        out_shape=jax.ShapeDtypeStruct((kp, ng, 2 * SUB, ncols), BF16),
        grid=(ng, ncols // CT_G),
        in_specs=[pl.BlockSpec((ha, SUB, CT_G), lambda g, j: (0, g, j)),
                  pl.BlockSpec((1, kp * 2 * SUB, ha * SUB), lambda g, j: (g, 0, 0))],
        out_specs=pl.BlockSpec((kp, 1, 2 * SUB, CT_G), lambda g, j: (0, g, 0, j)),
        compiler_params=_params("parallel", "parallel"),
        name="fft_stage1",
    )(u3, m8)


def _filter_hidden_kernel(zt_ref, w1_ref, b1_ref, f1_ref, w2_ref, b2_ref, f2_ref, w3_ref, b3_ref,
                          f3_ref, o_ref):
    h = jnp.sin(f1_ref[...] * (_dot3(w1_ref[...], zt_ref[...]) + b1_ref[...]))
    h = jnp.sin(f2_ref[...] * (_dot3(w2_ref[...], h) + b2_ref[...]))
    h = jnp.sin(f3_ref[...] * (_dot3(w3_ref[...], h) + b3_ref[...]))
    o_ref[...] = h.T


def filter_hidden(L, w1, b1, f1, w2, b2, f2, w3, b3, f3, tt=1024):
    hdim = FILTER_HIDDEN
    tt = min(tt, L)
    t = jnp.linspace(0.0, 1.0, L, dtype=F32)[None, :]
    omega = 2.0 * math.pi * jnp.arange(L, dtype=F32)[None, :] / L
    bands = jnp.linspace(1e-4, POS_BANDS - 1, POS_BANDS, dtype=F32)[:, None]
    zt = jnp.concatenate([t, jnp.cos(bands * omega), -jnp.sin(bands * omega)], axis=0)
    zt = jnp.zeros((LANES, L), F32).at[:POS_EMB_DIM].set(zt)
    w1t = jnp.zeros((hdim, LANES), F32).at[:, :POS_EMB_DIM].set(w1.T)
    col = lambda v: v.reshape(hdim, 1)
    full = lambda shape: pl.BlockSpec(shape, lambda i: (0, 0))
    return pl.pallas_call(
        _filter_hidden_kernel,
        out_shape=jax.ShapeDtypeStruct((L, hdim), F32),
        grid=(L // tt,),
        in_specs=[pl.BlockSpec((LANES, tt), lambda i: (0, i)),
                  full((hdim, LANES)), full((hdim, 1)), full((hdim, 1)),
                  full((hdim, hdim)), full((hdim, 1)), full((hdim, 1)),
                  full((hdim, hdim)), full((hdim, 1)), full((hdim, 1))],
        out_specs=pl.BlockSpec((tt, hdim), lambda i: (i, 0)),
        compiler_params=_params("parallel"),
        name="hyena_filter_hidden",
    )(zt, w1t, col(b1), col(f1), w2.T, col(b2), col(f2), w3.T, col(b3), col(f3))


def _filter_s1_kernel(h_ref, w4_ref, dl_ref, m8_ref, a_ref, nrm_ref, *, nb, L, cb):
    g, j = pl.program_id(0), pl.program_id(1)
    ha, sub, hdim = h_ref.shape
    rows = ha * sub
    wh = w4_ref[...].astype(BF16)
    ct = wh.shape[1]
    k = _dot(h_ref[...].reshape(rows, hdim).astype(BF16), wh).reshape(ha, sub, ct)
    a3 = lax.broadcasted_iota(jnp.int32, (ha, sub, ct), 0)
    b3 = lax.broadcasted_iota(jnp.int32, (ha, sub, ct), 1)
    tpos = a3 * nb + (b3 + g * sub)
    k = k * jnp.exp(-(tpos.astype(F32) * (1.0 / (L - 1))) * dl_ref[...])
    k = jnp.where(jnp.logical_and((j // cb) % 2 == 1, tpos == 0), 0.0, k).reshape(rows, ct)
    nrm_ref[0] = jnp.sum(jnp.abs(k), axis=0, keepdims=True)
    q = _dot(m8_ref[0], k.astype(BF16))
    a_ref[:, 0, :, :] = q.astype(a_ref.dtype).reshape(a_ref.shape[0], 2 * sub, ct)


def filter_stage1(L, w1, b1, f1, w2, b2, f2, w3, b3, f3, w4, m8):
    n, na, nb, kp = _fft_dims(L)
    ha, ng = na // 2, nb // SUB
    c = HYENA_WIDTH
    hdim = FILTER_HIDDEN
    hid = filter_hidden(L, w1, b1, f1, w2, b2, f2, w3, b3, f3).reshape(ha, nb, hdim)
    deltas = jnp.abs(jnp.linspace(MIN_DECAY, MAX_DECAY, c, dtype=F32)).reshape(1, c)
    nc = w4.shape[1]
    cb = c // CT_G
    kern = functools.partial(_filter_s1_kernel, nb=nb, L=L, cb=cb)
    return pl.pallas_call(
        kern,
        out_shape=(jax.ShapeDtypeStruct((kp, ng, 2 * SUB, nc), BF16),
                   jax.ShapeDtypeStruct((ng, 1, nc), F32)),
        grid=(ng, nc // CT_G),
        in_specs=[pl.BlockSpec((ha, SUB, hdim), lambda g, j: (0, g, 0)),
                  pl.BlockSpec((hdim, CT_G), lambda g, j: (0, j)),
                  pl.BlockSpec((1, CT_G), lambda g, j: (0, j % cb)),
                  pl.BlockSpec((1, kp * 2 * SUB, ha * SUB), lambda g, j: (g, 0, 0))],
        out_specs=(pl.BlockSpec((kp, 1, 2 * SUB, CT_G), lambda g, j: (0, g, 0, j)),
                   pl.BlockSpec((1, 1, CT_G), lambda g, j: (g, 0, j))),
        compiler_params=_params("parallel", "parallel"),
        name="hyena_filter_stage1",
    )(hid, w4, deltas, m8)


def _slab(ref, s):
    _, ng, rows, ct = ref.shape
    return ref[s].reshape(ng * rows, ct)


def _filter_spec_kernel(af_ref, ab_ref, f2_ref, nf_ref, nbk_ref, h_ref):
    f2 = f2_ref[...]
    nb = f2.shape[0] // 2
    inv = 1.0 / (jnp.sum(nf_ref[...], axis=0) + jnp.sum(nbk_ref[...], axis=0))
    for s in range(af_ref.shape[0]):
        xf = _dot(f2, _slab(af_ref, s))
        xb = _dot(f2, _slab(ab_ref, s))
        h_ref[0, s] = ((xf[:nb] + xb[:nb]) * inv).astype(h_ref.dtype)
        h_ref[1, s] = ((xf[nb:] - xb[nb:]) * inv).astype(h_ref.dtype)


def filter_spectrum(a, nrm, f2, L):
    n, na, nb, kp = _fft_dims(L)
    ng = nb // SUB
    c = HYENA_WIDTH
    cb = c // CT
    ms = _slabs_per_step(kp)
    fcol = lambda j: (j // cb) * 2 * cb + j % cb
    return pl.pallas_call(
        _filter_spec_kernel,
        out_shape=jax.ShapeDtypeStruct((2, kp, nb, HYENA_ORDER * c), BF16),
        grid=(HYENA_ORDER * cb, kp // ms),
        in_specs=[pl.BlockSpec((ms, ng, 2 * SUB, CT), lambda j, k: (k, 0, 0, fcol(j))),
                  pl.BlockSpec((ms, ng, 2 * SUB, CT), lambda j, k: (k, 0, 0, fcol(j) + cb)),
                  pl.BlockSpec((2 * nb, 2 * nb), lambda j, k: (0, 0)),
                  pl.BlockSpec((ng, 1, CT), lambda j, k: (0, 0, fcol(j))),
                  pl.BlockSpec((ng, 1, CT), lambda j, k: (0, 0, fcol(j) + cb))],
        out_specs=pl.BlockSpec((2, ms, nb, CT), lambda j, k: (0, k, 0, j)),
        compiler_params=_params("parallel", "parallel"),
        name="hyena_filter_spectrum",
    )(a, a, f2, nrm, nrm)


def _fft_mid_kernel(a_ref, h_ref, f2_ref, g_ref, q_ref):
    f2 = f2_ref[...]
    nb = f2.shape[0] // 2
    slabs, ng, rows, ct = q_ref.shape
    for s in range(slabs):
        x = _dot(f2, _slab(a_ref, s))
        xr, xi = x[:nb], x[nb:]
        hr, hi = h_ref[0, s].astype(F32), h_ref[1, s].astype(F32)
        y = jnp.concatenate([xr * hr - xi * hi, xr * hi + xi * hr], axis=0).astype(BF16)
        q_ref[s] = _dot(g_ref[s], y).astype(q_ref.dtype).reshape(ng, rows, ct)


def fft_mid(a, hf, order, f2, g, L):
    n, na, nb, kp = _fft_dims(L)
    ng = nb // SUB
    c = a.shape[3]
    cb = c // CT
    ms = _slabs_per_step(kp)
    return pl.pallas_call(
        _fft_mid_kernel,
        out_shape=jax.ShapeDtypeStruct(a.shape, BF16),
        grid=(kp // ms, cb),
        in_specs=[pl.BlockSpec((ms, ng, 2 * SUB, CT), lambda k, j: (k, 0, 0, j)),
                  pl.BlockSpec((2, ms, nb, CT), lambda k, j: (0, k, 0, order * cb + j)),
                  pl.BlockSpec((2 * nb, 2 * nb), lambda k, j: (0, 0)),
                  pl.BlockSpec((ms, 2 * nb, 2 * nb), lambda k, j: (k, 0, 0))],
        out_specs=pl.BlockSpec((ms, ng, 2 * SUB, CT), lambda k, j: (k, 0, 0, j)),
        compiler_params=_params("parallel", "parallel"),
        name="fft_mid",
    )(a, hf, f2, g)


def _fft_i2_kernel(q_ref, p8_ref, u_ref, x_ref, s_ref, o_ref):
    kp, _, rows, ct = q_ref.shape
    q = q_ref[:, 0, :, :].reshape(kp * rows, ct)
    y = _dot(p8_ref[...], q).reshape(o_ref.shape)
    o_ref[...] = x_ref[...] * (y + u_ref[...] * s_ref[...])


def fft_stage_out(q, u, gate_src, gate_blk, skip, p8, L):
    n, na, nb, kp = _fft_dims(L)
    ha, ng = na // 2, nb // SUB
    c = q.shape[3]
    u3 = u.reshape(ha, nb, u.shape[1])
    g3 = gate_src.reshape(ha, nb, gate_src.shape[1])
    out = pl.pallas_call(
        _fft_i2_kernel,
        out_shape=jax.ShapeDtypeStruct((ha, nb, c), F32),
        grid=(ng, c // CT_G),
        in_specs=[pl.BlockSpec((kp, 1, 2 * SUB, CT_G), lambda g, j: (0, g, 0, j)),
                  pl.BlockSpec((ha * SUB, kp * 2 * SUB), lambda g, j: (0, 0)),
                  pl.BlockSpec((ha, SUB, CT_G), lambda g, j: (0, g, j)),
                  pl.BlockSpec((ha, SUB, CT_G), lambda g, j: (0, g, gate_blk + j)),
                  pl.BlockSpec((1, 1, CT_G), lambda g, j: (0, 0, j))],
        out_specs=pl.BlockSpec((ha, SUB, CT_G), lambda g, j: (0, g, j)),
        compiler_params=_params("parallel", "parallel"),
        name="fft_stage_out",
    )(q, p8, u3, g3, skip.reshape(1, 1, c))
    return out.reshape(L, c)


def hyena_mixer(u_hy, conv_w, conv_b, filt, skip, tables):
    L = u_hy.shape[0]
    c = HYENA_WIDTH
    m8, f2, g, p8 = tables
    uc = short_conv(u_hy, conv_w, conv_b)
    a_filt, nrm = filter_stage1(L, *filt, m8)
    hf = filter_spectrum(a_filt, nrm, f2, L)
    z = uc
    for order in range(HYENA_ORDER):
        a = fft_stage1(z, c, m8, L)
        q = fft_mid(a, hf, order, f2, g, L)
        z = fft_stage_out(q, z, uc, (order + 1) * (c // CT_G), skip[order], p8, L)
    return z


MOE_TM = 512
MOE_TN = 512
COMBINE_TM = 256


def _moe_plan(route, L):
    tm = MOE_TM
    nt = -(-(2 * L + N_EXPERTS * (tm - 1)) // tm)
    e_flat = jnp.concatenate([route[:, ROUTE_E1], route[:, ROUTE_E2]]).astype(jnp.int32)
    onehot = (e_flat[:, None] == jnp.arange(N_EXPERTS, dtype=jnp.int32)[None, :]).astype(jnp.int32)
    csum = jnp.cumsum(onehot, axis=0)
    rank = jnp.sum(onehot * (csum - 1), axis=1)
    counts = csum[-1]
    padded = ((counts + tm - 1) // tm) * tm
    ends = jnp.cumsum(padded)
    dest = (ends - padded)[e_flat] + rank
    tok = jnp.tile(jnp.arange(L, dtype=jnp.int32), 2)
    row_token = jnp.zeros((nt * tm,), jnp.int32).at[dest].set(tok, unique_indices=True)
    n_used = ends[-1] // tm
    tile_row = jnp.arange(nt, dtype=jnp.int32) * tm
    tile_expert = jnp.sum((tile_row[:, None] >= ends[None, :]).astype(jnp.int32), axis=1)
    last = jnp.take(tile_expert, jnp.maximum(n_used - 1, 0))
    tile_expert = jnp.where(jnp.arange(nt) < n_used, tile_expert, last)
    tile_first = jnp.concatenate([jnp.ones((1,), jnp.int32),
                                  (tile_expert[1:] != tile_expert[:-1]).astype(jnp.int32)])
    tile_group = jnp.cumsum(tile_first) - 1
    group_expert = jnp.zeros((N_EXPERTS,), jnp.int32).at[tile_group].set(tile_expert)
    n_groups = jnp.take(tile_group, jnp.maximum(n_used - 1, 0)) + 1
    i32 = lambda v: v.reshape(1).astype(jnp.int32)
    last_row = (ends - padded + counts)[tile_expert]
    tile_valid = jnp.where(jnp.arange(nt) < n_used, jnp.clip(last_row - tile_row, 0, tm), 0)
    tile_chunks = ((tile_valid + GATHER_CHUNK - 1) // GATHER_CHUNK).astype(jnp.int32)
    return dict(nt=nt, row_token=row_token, dest=dest, tile_chunks=tile_chunks,
                scalars=(tile_expert, tile_first, i32(n_used), tile_group.astype(jnp.int32),
                         i32(n_groups), group_expert))


def _row_copy(src_hbm, idx, buf, slot, r, sem):
    return pltpu.make_async_copy(src_hbm.at[pl.ds(idx, 1)], buf.at[slot, pl.ds(r, 1)],
                                 sem.at[slot])


def _gather_rows(idx_ref, src_hbm, buf, slot, sem, n_rows, start):
    if not start:
        pltpu.make_async_copy(src_hbm.at[pl.ds(0, n_rows)], buf.at[slot], sem.at[slot]).wait()
        return

    def body(r, carry):
        _row_copy(src_hbm, idx_ref[0, 0, r], buf, slot, r, sem).start(priority=1)
        return carry
    lax.fori_loop(0, n_rows, body, 0, unroll=8)


def _pipelined_gather(cur_ref, nxt_ref, src_hbm, buf, sem, n_rows):
    i, n = pl.program_id(0), pl.num_programs(0)
    slot = lax.rem(i, 2)

    @pl.when(i == 0)
    def _():
        _gather_rows(cur_ref, src_hbm, buf, 0, sem, n_rows, True)

    @pl.when(i + 1 < n)
    def _():
        _gather_rows(nxt_ref, src_hbm, buf, 1 - slot, sem, n_rows, True)

    _gather_rows(cur_ref, src_hbm, buf, slot, sem, n_rows, False)
    return slot


GATHER_CHUNK = 32


def _dispatch_rows(idx_ref, h_hbm, buf, slot, sem, n_chunks, start):
    def body(c, carry):
        base = pl.multiple_of(c * GATHER_CHUNK, GATHER_CHUNK)
        if start:
            for r in range(GATHER_CHUNK):
                _row_copy(h_hbm, idx_ref[0, 0, base + r], buf, slot, base + r, sem).start(priority=1)
        else:
            pltpu.make_async_copy(h_hbm.at[pl.ds(0, GATHER_CHUNK)],
                                  buf.at[slot, pl.ds(base, GATHER_CHUNK)], sem.at[slot]).wait()
        return carry
    lax.fori_loop(0, n_chunks, body, 0)


def _dispatch_kernel(nch_ref, cur_ref, nxt_ref, h_hbm, o_ref, buf, sem):
    i, n = pl.program_id(0), pl.num_programs(0)
    slot = lax.rem(i, 2)

    @pl.when(i == 0)
    def _():
        buf[...] = jnp.zeros_like(buf)
        _dispatch_rows(cur_ref, h_hbm, buf, 0, sem, nch_ref[0], True)

    @pl.when(i + 1 < n)
    def _():
        _dispatch_rows(nxt_ref, h_hbm, buf, 1 - slot, sem, nch_ref[jnp.minimum(i + 1, n - 1)], True)

    _dispatch_rows(cur_ref, h_hbm, buf, slot, sem, nch_ref[i], False)
    o_ref[...] = buf[slot].astype(o_ref.dtype)


def moe_dispatch(h, row_token, tile_chunks, nt):
    L, d = h.shape
    idx = row_token.reshape(nt, 1, MOE_TM)
    smem = lambda f: pl.BlockSpec((1, 1, MOE_TM), f, memory_space=pltpu.SMEM)
    return pl.pallas_call(
        _dispatch_kernel,
        out_shape=jax.ShapeDtypeStruct((nt * MOE_TM, d), BF16),
        grid_spec=pltpu.PrefetchScalarGridSpec(
            num_scalar_prefetch=1,
            grid=(nt,),
            in_specs=[smem(lambda i, nch: (i, 0, 0)),
                      smem(lambda i, nch: (jnp.minimum(i + 1, nt - 1), 0, 0)),
                      pl.BlockSpec(memory_space=pl.ANY)],
            out_specs=pl.BlockSpec((MOE_TM, d), lambda i, nch: (i, 0)),
            scratch_shapes=[pltpu.VMEM((2, MOE_TM, d), F32), pltpu.SemaphoreType.DMA((2,))]),
        compiler_params=_params("arbitrary"),
        name="moe_dispatch",
    )(tile_chunks, idx, idx, h)


def _weight_copy(w_hbm, e, jcol, wbuf, k, slot, sem):
    tn = wbuf.shape[3]
    return pltpu.make_async_copy(w_hbm.at[e, :, pl.ds(pl.multiple_of(jcol * tn, tn), tn)],
                                 wbuf.at[k, slot], sem.at[k, slot])


def _refresh_expert_weights(scalars, w_hbms, wbuf, wbs, sem):
    te_ref, tf_ref, _, tg_ref, ng_ref, ge_ref = scalars
    j, i = pl.program_id(0), pl.program_id(1)
    nj = pl.num_programs(0)

    @pl.when(tf_ref[i] == 1)
    def _():
        g, ng = tg_ref[i], ng_ref[0]
        blk = j * ng + g
        slot = lax.rem(blk, 2)

        @pl.when(blk == 0)
        def _():
            for k, w in enumerate(w_hbms):
                _weight_copy(w, te_ref[i], j, wbuf, k, 0, sem).start()

        for k, w in enumerate(w_hbms):
            _weight_copy(w, 0, 0, wbuf, k, slot, sem).wait()
        for k, wb in enumerate(wbs):
            wb[...] = wbuf[k, slot].astype(BF16)

        wrap = g + 1 == ng
        nxt_e = ge_ref[jnp.where(wrap, 0, g + 1)]
        nxt_j = j + wrap.astype(jnp.int32)

        @pl.when(nxt_j < nj)
        def _():
            for k, w in enumerate(w_hbms):
                _weight_copy(w, nxt_e, nxt_j, wbuf, k, 1 - slot, sem).start()


def _moe_up_kernel(te_ref, tf_ref, nu_ref, tg_ref, ng_ref, ge_ref, a_ref, wg_hbm, wu_hbm, o_ref,
                   wbuf, wgb, wub, sem):
    i = pl.program_id(1)
    _refresh_expert_weights((te_ref, tf_ref, nu_ref, tg_ref, ng_ref, ge_ref),
                            (wg_hbm, wu_hbm), wbuf, (wgb, wub), sem)

    @pl.when(i < nu_ref[0])
    def _():
        a = a_ref[...]
        o_ref[...] = (_silu(_dot(a, wgb[...])) * _dot(a, wub[...])).astype(o_ref.dtype)

    @pl.when(i >= nu_ref[0])
    def _():
        o_ref[...] = jnp.zeros_like(o_ref)


def moe_up(hs, w_gate, w_up, plan):
    r, d = hs.shape
    f = w_gate.shape[2]
    nt = plan["nt"]
    anyspec = pl.BlockSpec(memory_space=pl.ANY)
    return pl.pallas_call(
        _moe_up_kernel,
        out_shape=jax.ShapeDtypeStruct((r, f), BF16),
        grid_spec=pltpu.PrefetchScalarGridSpec(
            num_scalar_prefetch=6,
            grid=(f // MOE_TN, nt),
            in_specs=[pl.BlockSpec((MOE_TM, d), lambda j, i, *_: (i, 0)), anyspec, anyspec],
            out_specs=pl.BlockSpec((MOE_TM, MOE_TN), lambda j, i, *_: (i, j)),
            scratch_shapes=[pltpu.VMEM((2, 2, d, MOE_TN), F32),
                            pltpu.VMEM((d, MOE_TN), BF16), pltpu.VMEM((d, MOE_TN), BF16),
                            pltpu.SemaphoreType.DMA((2, 2))]),
        compiler_params=_params("arbitrary", "arbitrary"),
        name="moe_up",
    )(*plan["scalars"], hs, w_gate, w_up)


def _moe_down_kernel(te_ref, tf_ref, nu_ref, tg_ref, ng_ref, ge_ref, t_ref, wd_hbm, o_ref,
                     wbuf, wdb, sem):
    i = pl.program_id(1)
    _refresh_expert_weights((te_ref, tf_ref, nu_ref, tg_ref, ng_ref, ge_ref),
                            (wd_hbm,), wbuf, (wdb,), sem)

    @pl.when(i < nu_ref[0])
    def _():
        o_ref[...] = _dot(t_ref[...], wdb[...])

    @pl.when(i >= nu_ref[0])
    def _():
        o_ref[...] = jnp.zeros_like(o_ref)


def moe_down(t, w_down, plan):
    r, f = t.shape
    d = w_down.shape[2]
    nt = plan["nt"]
    return pl.pallas_call(
        _moe_down_kernel,
        out_shape=jax.ShapeDtypeStruct((r, d), F32),
        grid_spec=pltpu.PrefetchScalarGridSpec(
            num_scalar_prefetch=6,
            grid=(d // MOE_TN, nt),
            in_specs=[pl.BlockSpec((MOE_TM, f), lambda j, i, *_: (i, 0)),
                      pl.BlockSpec(memory_space=pl.ANY)],
            out_specs=pl.BlockSpec((MOE_TM, MOE_TN), lambda j, i, *_: (i, j)),
            scratch_shapes=[pltpu.VMEM((1, 2, f, MOE_TN), F32), pltpu.VMEM((f, MOE_TN), BF16),
                            pltpu.SemaphoreType.DMA((1, 2))]),
        compiler_params=_params("arbitrary", "arbitrary"),
        name="moe_down",
    )(*plan["scalars"], t, w_down)


def _combine_kernel(cur_ref, nxt_ref, y_hbm, x_ref, route_ref, g_ref, o_ref, buf, sem, *, final):
    tm = x_ref.shape[0]
    slot = _pipelined_gather(cur_ref, nxt_ref, y_hbm, buf, sem, 2 * tm)
    route = route_ref[...]
    g1 = route[:, ROUTE_G1:ROUTE_G1 + 1]
    g2 = route[:, ROUTE_G2:ROUTE_G2 + 1]
    x = x_ref[...] + g1 * buf[slot, :tm, :] + g2 * buf[slot, tm:, :]
    if final:
        ms = jnp.mean(x * x, axis=-1, keepdims=True)
        x = x * lax.rsqrt(ms + RMS_EPS) * g_ref[...]
    o_ref[...] = x


def moe_combine(x, y, dest, route, final_gain):
    L, d = x.shape
    tm = COMBINE_TM
    nt = L // tm
    idx = jnp.concatenate([dest[:L].reshape(nt, 1, tm), dest[L:].reshape(nt, 1, tm)], axis=2)
    smem = lambda f: pl.BlockSpec((1, 1, 2 * tm), f, memory_space=pltpu.SMEM)
    final = final_gain is not None
    gain = (final_gain if final else jnp.ones((d,), F32)).reshape(1, d)
    kern = functools.partial(_combine_kernel, final=final)
    return pl.pallas_call(
        kern,
        out_shape=jax.ShapeDtypeStruct((L, d), F32),
        grid=(nt,),
        in_specs=[smem(lambda i: (i, 0, 0)),
                  smem(lambda i: (jnp.minimum(i + 1, nt - 1), 0, 0)),
                  pl.BlockSpec(memory_space=pl.ANY),
                  pl.BlockSpec((tm, d), lambda i: (i, 0)),
                  pl.BlockSpec((tm, LANES), lambda i: (i, 0)),
                  pl.BlockSpec((1, d), lambda i: (0, 0))],
        out_specs=pl.BlockSpec((tm, d), lambda i: (i, 0)),
        scratch_shapes=[pltpu.VMEM((2, 2 * tm, d), F32), pltpu.SemaphoreType.DMA((2,))],
        compiler_params=_params("arbitrary"),
        name="moe_combine",
    )(idx, idx, y, x, route, gain)


def moe_ffn(x, norm_g, router_w, w_gate, w_up, w_down, final_gain):
    L = x.shape[0]
    h, route = rmsnorm_router(x, norm_g, router_w)
    plan = _moe_plan(route, L)
    hs = moe_dispatch(h, plan["row_token"], plan["tile_chunks"], plan["nt"])
    t = moe_up(hs, w_gate, w_up, plan)
    y = moe_down(t, w_down, plan)
    return moe_combine(x, y, plan["dest"], route, final_gain)


def _layer_mixers(x, l, tables, norm_mix, w_in_t, conv_w, conv_b, filt, hyena_skip, w_branch_a,
                  gla_gate_w2, gla_gate_b, gla_norm, w_branch_b, w_out):
    hc = (HYENA_ORDER + 1) * HYENA_WIDTH
    o_lr = hc + 2 * GLA_DK_TOTAL + 2 * GLA_DV_TOTAL
    o_g = o_lr + 2 * GLA_LOWRANK
    h = rmsnorm(x, norm_mix[l], BF16)
    proj = matmul([h], [(w_in_t, l, 0)], o_lr, [(0, 0)], _ep_plain, [], F32,
                  tm=1024, tn=1024, name="in_proj", w_transposed=True)
    gates = matmul([h], [(w_in_t[l, o_g:][None], 0, 0)], 2 * D_MODEL, [(0, 0)], _ep_sigmoid, [],
                   BF16, tm=1024, tn=1024, name="in_proj_gates", w_transposed=True)
    la = gla_gates(h, w_in_t[l, o_lr:o_g], gla_gate_w2[l], gla_gate_b[l])

    z_a = hyena_mixer(proj, conv_w[l], conv_b[l], filt, hyena_skip[l], tables)
    o_f, o_b = gla_scan(proj, hc, la)
    z_b = gla_post(o_f, o_b, proj, hc + 2 * GLA_DK_TOTAL + GLA_DV_TOTAL, gla_norm[l])

    mixed = matmul([z_a, z_b], [(w_branch_a, l, 0), (w_branch_b, l, 0)], D_MODEL,
                   [(0, 0), (1, 1)], _ep_merge, [(gates, 0), (gates, D_MODEL)], BF16,
                   tm=1024, tn=1024, name="branch_merge")
    return matmul([mixed], [(w_out, l, 0)], D_MODEL, [(0, 0)], _ep_residual, [(x, 0)], F32,
                  tm=1024, tn=512, name="out_proj")


def _swiglu_ffn(x, h, wg, wu, wd, i):
    f = wg.shape[2]
    t = matmul([h], [(wg, i, 0), (wu, i, 0)], f, [(0, 0), (0, 1)], _ep_swiglu, [], BF16,
               tm=1024, tn=512, name="ffn_up")
    return matmul([t], [(wd, i, 0)], wd.shape[2], [(0, 0)], _ep_residual, [(x, 0)], F32,
                  tm=512, tn=512, name="ffn_down")


def kernel(x, norm_mix, w_in, conv_w, conv_b, filt_w1, filt_b1, filt_freq1, filt_w2, filt_b2, filt_freq2, filt_w3, filt_b3, filt_freq3, filt_w4, hyena_skip, w_branch_a, gla_gate_w2, gla_gate_b, gla_norm, w_branch_b, w_out, norm_ffn, dense_w_gate, dense_w_up, dense_w_down, router_w, moe_w_gate, moe_w_up, moe_w_down, norm_final):
    b, L, d = x.shape
    assert b == 1
    x = x.reshape(L, d)
    tables = _fft_tables(L)
    w_in_t = jnp.swapaxes(w_in, 1, 2)
    for l in range(DEPTH):
        filt = (filt_w1[l], filt_b1[l], filt_freq1[l], filt_w2[l], filt_b2[l], filt_freq2[l],
                filt_w3[l], filt_b3[l], filt_freq3[l], filt_w4[l])
        x = _layer_mixers(x, l, tables, norm_mix, w_in_t, conv_w, conv_b, filt, hyena_skip,
                          w_branch_a, gla_gate_w2, gla_gate_b, gla_norm, w_branch_b, w_out)
        i = l // 2
        last = l == DEPTH - 1
        if l % 2 == 0:
            h = rmsnorm(x, norm_ffn[l], BF16)
            x = _swiglu_ffn(x, h, dense_w_gate, dense_w_up, dense_w_down, i)
            if last:
                x = rmsnorm(x, norm_final, F32)
        else:
            x = moe_ffn(x, norm_ffn[l], router_w[i], moe_w_gate[i], moe_w_up[i], moe_w_down[i],
                        norm_final if last else None)
    return x.reshape(b, L, d)
```

```python
import functools
import math

import jax
import jax.numpy as jnp
from jax import lax
from jax.experimental import pallas as pl
from jax.experimental.pallas import tpu as pltpu

F32 = jnp.float32
BF16 = jnp.bfloat16

D_MODEL = 2048
DEPTH = 2
HYENA_WIDTH = D_MODEL // 2
HYENA_ORDER = 2
SHORT_CONV = 3
POS_EMB_DIM = 33
POS_BANDS = (POS_EMB_DIM - 1) // 2
FILTER_HIDDEN = 64
MIN_DECAY = math.log(1e-2) / 1.5
MAX_DECAY = math.log(1e-2) / 0.3
GLA_HEADS = 4
GLA_DK = 128
GLA_DV = 256
GLA_DK_TOTAL = GLA_HEADS * GLA_DK
GLA_DV_TOTAL = GLA_HEADS * GLA_DV
GLA_LOWRANK = 16
GLA_GATE_TAU = 16.0
GLA_CHUNK = 64
N_EXPERTS = 8
RMS_EPS = 1e-6

LANES = 128
VMEM_LIMIT_BYTES = 56 * 1024 * 1024


def _params(*sem):
    return pltpu.CompilerParams(dimension_semantics=sem, vmem_limit_bytes=VMEM_LIMIT_BYTES)


def _dot(a, b):
    return jnp.dot(a, b, preferred_element_type=F32)


def _split(a):
    hi = a.astype(BF16)
    lo = (a - hi.astype(F32)).astype(BF16)
    return hi, lo


def _dot3(a, b):
    ah, al = _split(a)
    bh, bl = _split(b)
    return _dot(ah, bh) + (_dot(ah, bl) + _dot(al, bh))


def _sigmoid(x):
    return 1.0 / (1.0 + jnp.exp(-x))


def _silu(x):
    return x * _sigmoid(x)


def _log_sigmoid(x):
    return jnp.minimum(x, 0.0) - jnp.log(1.0 + jnp.exp(-jnp.abs(x)))


def _rmsnorm_kernel(x_ref, g_ref, o_ref):
    x = x_ref[...]
    ms = jnp.mean(x * x, axis=-1, keepdims=True)
    o_ref[...] = (x * lax.rsqrt(ms + RMS_EPS) * g_ref[...]).astype(o_ref.dtype)


def rmsnorm(x, g, out_dtype, tm=512):
    m, d = x.shape
    return pl.pallas_call(
        _rmsnorm_kernel,
        out_shape=jax.ShapeDtypeStruct((m, d), out_dtype),
        grid=(m // tm,),
        in_specs=[pl.BlockSpec((tm, d), lambda i: (i, 0)),
                  pl.BlockSpec((1, d), lambda i: (0, 0))],
        out_specs=pl.BlockSpec((tm, d), lambda i: (i, 0)),
        compiler_params=_params("parallel"),
        name="rmsnorm",
    )(x, g.reshape(1, d))


def _rmsnorm_router_kernel(x_ref, g_ref, rw_ref, o_ref, cw_ref, cnt_ref):
    @pl.when(pl.program_id(0) == 0)
    def _():
        cnt_ref[...] = jnp.zeros_like(cnt_ref)

    x = x_ref[...]
    ms = jnp.mean(x * x, axis=-1, keepdims=True)
    h = x * lax.rsqrt(ms + RMS_EPS) * g_ref[...]
    o_ref[...] = h.astype(o_ref.dtype)
    logits = _dot3(h, rw_ref[...])
    col = lax.broadcasted_iota(jnp.int32, logits.shape, 1).astype(F32)
    neg = jnp.float32(-jnp.inf)
    logits = jnp.where(col < N_EXPERTS, logits, neg)
    m1 = jnp.max(logits, axis=-1, keepdims=True)
    i1 = jnp.min(jnp.where(logits == m1, col, float(LANES)), axis=-1, keepdims=True)
    rest = jnp.where(col == i1, neg, logits)
    m2 = jnp.max(rest, axis=-1, keepdims=True)
    i2 = jnp.min(jnp.where(rest == m2, col, float(LANES)), axis=-1, keepdims=True)
    e2 = jnp.exp(m2 - m1)
    g1 = 1.0 / (1.0 + e2)
    g2 = e2 / (1.0 + e2)
    hit1, hit2 = col == i1, col == i2
    c = jnp.where(hit1, 1.0, 0.0) + jnp.where(hit2, 1.0, 0.0)
    tm = c.shape[0]
    rows = lax.broadcasted_iota(jnp.int32, c.shape, 0)
    incl, step = c, 1
    while step < tm:
        incl = incl + jnp.where(rows >= step, pltpu.roll(incl, step, 0), 0.0)
        step *= 2
    before = incl - c + cnt_ref[...]
    r1 = jnp.sum(jnp.where(hit1, before, 0.0), axis=-1, keepdims=True)
    r2 = jnp.sum(jnp.where(hit2, before, 0.0), axis=-1, keepdims=True)
    cnt_ref[...] = cnt_ref[...] + incl[tm - 1:tm]
    cw_ref[...] = (jnp.where(col == ROUTE_E1, i1, 0.0) + jnp.where(col == ROUTE_E2, i2, 0.0)
                   + jnp.where(col == ROUTE_G1, g1, 0.0) + jnp.where(col == ROUTE_G2, g2, 0.0)
                   + jnp.where(col == ROUTE_R1, r1, 0.0) + jnp.where(col == ROUTE_R2, r2, 0.0))


ROUTE_E1, ROUTE_E2, ROUTE_G1, ROUTE_G2, ROUTE_R1, ROUTE_R2 = 0, 1, 2, 3, 4, 5


def rmsnorm_router(x, g, router_w, tm=256):
    m, d = x.shape
    rw = jnp.zeros((d, LANES), F32).at[:, :N_EXPERTS].set(router_w)
    return pl.pallas_call(
        _rmsnorm_router_kernel,
        out_shape=(jax.ShapeDtypeStruct((m, d), F32), jax.ShapeDtypeStruct((m, LANES), F32),
                   jax.ShapeDtypeStruct((1, LANES), F32)),
        grid=(m // tm,),
        in_specs=[pl.BlockSpec((tm, d), lambda i: (i, 0)),
                  pl.BlockSpec((1, d), lambda i: (0, 0)),
                  pl.BlockSpec((d, LANES), lambda i: (0, 0))],
        out_specs=(pl.BlockSpec((tm, d), lambda i: (i, 0)),
                   pl.BlockSpec((tm, LANES), lambda i: (i, 0)),
                   pl.BlockSpec((1, LANES), lambda i: (0, 0))),
        compiler_params=_params("arbitrary"),
        name="rmsnorm_router",
    )(x, g.reshape(1, d), rw)


def _mm_kernel(*refs, n_a, n_w, dots, n_extra, epilogue, w_transposed):
    a_refs = refs[:n_a]
    w_refs = refs[n_a:n_a + n_w]
    e_refs = refs[n_a + n_w:n_a + n_w + n_extra]
    o_ref = refs[n_a + n_w + n_extra]
    wb_refs = refs[n_a + n_w + n_extra + 1:]

    @pl.when(pl.program_id(1) == 0)
    def _():
        for w, wb in zip(w_refs, wb_refs):
            wb[...] = w[0].astype(BF16)

    a_vals = [a[...].astype(BF16) for a in a_refs]
    nt_dims = (((1,), (1,)), ((), ()))
    accs = [lax.dot_general(a_vals[ai], wb_refs[wi][...], nt_dims, preferred_element_type=F32)
            if w_transposed else _dot(a_vals[ai], wb_refs[wi][...]) for ai, wi in dots]
    o_ref[...] = epilogue(accs, [e[...] for e in e_refs]).astype(o_ref.dtype)


def matmul(a_list, w_list, n, dots, epilogue, extras, out_dtype, *, tm, tn, name,
           w_transposed=False):
    m = a_list[0].shape[0]
    assert m % tm == 0 and n % tn == 0
    in_specs = [pl.BlockSpec((tm, a.shape[1]), lambda j, i: (i, 0)) for a in a_list]
    w_arrays, scratch = [], []
    for w, s, first in w_list:
        assert first % tn == 0
        w_arrays.append(w)
        if w_transposed:
            in_specs.append(pl.BlockSpec((1, tn, w.shape[2]),
                                         lambda j, i, s=s, off=first // tn: (s, off + j, 0)))
            scratch.append(pltpu.VMEM((tn, w.shape[2]), BF16))
            continue
        in_specs.append(pl.BlockSpec((1, w.shape[1], tn),
                                     lambda j, i, s=s, off=first // tn: (s, 0, off + j)))
        scratch.append(pltpu.VMEM((w.shape[1], tn), BF16))
    extra_arrays = []
    for arr, first in extras:
        assert first % tn == 0
        extra_arrays.append(arr)
        in_specs.append(pl.BlockSpec((tm, tn), lambda j, i, off=first // tn: (i, off + j)))
    kern = functools.partial(_mm_kernel, n_a=len(a_list), n_w=len(w_list), dots=tuple(dots),
                             n_extra=len(extras), epilogue=epilogue, w_transposed=w_transposed)
    return pl.pallas_call(
        kern,
        out_shape=jax.ShapeDtypeStruct((m, n), out_dtype),
        grid=(n // tn, m // tm),
        in_specs=in_specs,
        out_specs=pl.BlockSpec((tm, tn), lambda j, i: (i, j)),
        scratch_shapes=scratch,
        compiler_params=_params("arbitrary", "arbitrary"),
        name=name,
    )(*a_list, *w_arrays, *extra_arrays)


def _ep_plain(accs, ex):
    return accs[0]


def _ep_sigmoid(accs, ex):
    return _sigmoid(accs[0])


def _ep_residual(accs, ex):
    return ex[0] + accs[0]


def _ep_merge(accs, ex):
    return ex[0].astype(F32) * accs[0] + ex[1].astype(F32) * accs[1]


def _ep_swiglu(accs, ex):
    return _silu(accs[0]) * accs[1]


def _gate_kernel(h_ref, wlr_ref, w2_ref, b_ref, o_ref):
    lr = lax.dot_general(h_ref[...], wlr_ref[...].astype(BF16), (((1,), (1,)), ((), ())),
                         preferred_element_type=F32)
    z = _dot3(lr, w2_ref[...]) + b_ref[...]
    o_ref[...] = _log_sigmoid(z) * (1.0 / GLA_GATE_TAU)


def gla_gates(h, w_lr_t, gate_w2, gate_b, tm=512):
    m, d = h.shape
    r = GLA_LOWRANK
    n = 2 * GLA_DK_TOTAL
    wlr = jnp.zeros((LANES, d), F32).at[:2 * r].set(w_lr_t)
    w2 = jnp.zeros((LANES, n), F32)
    w2 = w2.at[:r, :GLA_DK_TOTAL].set(gate_w2[0]).at[r:2 * r, GLA_DK_TOTAL:].set(gate_w2[1])
    b = gate_b.reshape(1, n)
    return pl.pallas_call(
        _gate_kernel,
        out_shape=jax.ShapeDtypeStruct((m, n), F32),
        grid=(m // tm,),
        in_specs=[pl.BlockSpec((tm, d), lambda i: (i, 0)),
                  pl.BlockSpec((LANES, d), lambda i: (0, 0)),
                  pl.BlockSpec((LANES, n), lambda i: (0, 0)),
                  pl.BlockSpec((1, n), lambda i: (0, 0))],
        out_specs=pl.BlockSpec((tm, n), lambda i: (i, 0)),
        compiler_params=_params("parallel"),
        name="gla_gates",
    )(h, wlr, w2, b)


GLA_SUB = 4


def _gla_chunk(q, k, v, la, st_ref, h, reverse):
    c = GLA_CHUNK
    row = lax.broadcasted_iota(jnp.int32, (c, c), 0)
    col = lax.broadcasted_iota(jnp.int32, (c, c), 1)
    keep = (col > row) if reverse else (col <= row)
    b = la
    rows = lax.broadcasted_iota(jnp.int32, la.shape, 0)
    step = 1
    while step < c:
        if reverse:
            b = b + jnp.where(rows < c - step, pltpu.roll(b, c - step, 0), 0.0)
        else:
            b = b + jnp.where(rows >= step, pltpu.roll(b, step, 0), 0.0)
        step *= 2
    tot = b[0:1] if reverse else b[c - 1:c]
    q_in = (q * ((GLA_DK ** -0.5) * jnp.exp(b))).astype(BF16)
    k_dec = k * jnp.exp(-b)
    k_in = k_dec.astype(BF16)
    k_out = (k_dec * jnp.exp(tot)).astype(BF16)
    vb = v.astype(BF16)
    s = lax.dot_general(q_in, k_in, (((1,), (1,)), ((), ())), preferred_element_type=F32)
    s = jnp.where(keep, s, 0.0).astype(BF16)
    st = st_ref[h]
    o = _dot(s, vb) + lax.dot_general(q_in, st.astype(BF16), (((1,), (1,)), ((), ())),
                                      preferred_element_type=F32)
    upd = lax.dot_general(vb, k_out, (((0,), (0,)), ((), ())), preferred_element_type=F32)
    st_ref[h] = st * jnp.exp(tot) + upd
    return o


def _gla_kernel(qf_ref, kf_ref, vf_ref, lf_ref, qb_ref, kb_ref, vb_ref, lb_ref,
                of_ref, ob_ref, sf_ref, sb_ref):
    @pl.when(pl.program_id(0) == 0)
    def _():
        sf_ref[...] = jnp.zeros_like(sf_ref)
        sb_ref[...] = jnp.zeros_like(sb_ref)

    def body(s, carry):
        rf = pl.multiple_of(s * GLA_CHUNK, GLA_CHUNK)
        rb = pl.multiple_of((GLA_SUB - 1 - s) * GLA_CHUNK, GLA_CHUNK)
        for h in range(GLA_HEADS):
            kk = slice(h * GLA_DK, (h + 1) * GLA_DK)
            vv = slice(h * GLA_DV, (h + 1) * GLA_DV)
            of_ref[pl.ds(rf, GLA_CHUNK), vv] = _gla_chunk(
                qf_ref[pl.ds(rf, GLA_CHUNK), kk], kf_ref[pl.ds(rf, GLA_CHUNK), kk],
                vf_ref[pl.ds(rf, GLA_CHUNK), vv], lf_ref[pl.ds(rf, GLA_CHUNK), kk],
                sf_ref, h, False)
            ob_ref[pl.ds(rb, GLA_CHUNK), vv] = _gla_chunk(
                qb_ref[pl.ds(rb, GLA_CHUNK), kk], kb_ref[pl.ds(rb, GLA_CHUNK), kk],
                vb_ref[pl.ds(rb, GLA_CHUNK), vv], lb_ref[pl.ds(rb, GLA_CHUNK), kk],
                sb_ref, h, True)
        return carry

    lax.fori_loop(0, GLA_SUB, body, 0, unroll=True)


def gla_scan(qkvr, first, la):
    L = qkvr.shape[0]
    rows = GLA_CHUNK * GLA_SUB
    n = L // rows
    dk, dv = GLA_DK_TOTAL, GLA_DV_TOTAL
    assert first % dv == 0
    qb, vb = first // dk, (first + 2 * dk) // dv
    specs = []
    for rev in (False, True):
        blk = (lambda i: n - 1 - i) if rev else (lambda i: i)
        specs += [pl.BlockSpec((rows, dk), lambda i, blk=blk: (blk(i), qb)),
                  pl.BlockSpec((rows, dk), lambda i, blk=blk: (blk(i), qb + 1)),
                  pl.BlockSpec((rows, dv), lambda i, blk=blk: (blk(i), vb)),
                  pl.BlockSpec((rows, dk), lambda i, blk=blk, c=int(rev): (blk(i), c))]
    return pl.pallas_call(
        _gla_kernel,
        out_shape=(jax.ShapeDtypeStruct((L, dv), F32), jax.ShapeDtypeStruct((L, dv), F32)),
        grid=(n,),
        in_specs=specs,
        out_specs=(pl.BlockSpec((rows, dv), lambda i: (i, 0)),
                   pl.BlockSpec((rows, dv), lambda i: (n - 1 - i, 0))),
        scratch_shapes=[pltpu.VMEM((GLA_HEADS, GLA_DV, GLA_DK), F32),
                        pltpu.VMEM((GLA_HEADS, GLA_DV, GLA_DK), F32)],
        compiler_params=_params("arbitrary"),
        name="gla_scan",
    )(qkvr, qkvr, qkvr, la, qkvr, qkvr, qkvr, la)


def _gla_post_kernel(of_ref, ob_ref, r_ref, g_ref, o_ref):
    g = g_ref[...]
    for h in range(GLA_HEADS):
        vv = slice(h * GLA_DV, (h + 1) * GLA_DV)
        o = of_ref[:, vv] + ob_ref[:, vv]
        ms = jnp.mean(o * o, axis=-1, keepdims=True)
        y = o * lax.rsqrt(ms + RMS_EPS) * g
        o_ref[:, vv] = (y * _silu(r_ref[:, vv])).astype(o_ref.dtype)


def gla_post(o_f, o_b, qkvr, first, norm_g, tm=512):
    L, dv = o_f.shape
    assert first % dv == 0
    return pl.pallas_call(
        _gla_post_kernel,
        out_shape=jax.ShapeDtypeStruct((L, dv), BF16),
        grid=(L // tm,),
        in_specs=[pl.BlockSpec((tm, dv), lambda i: (i, 0)),
                  pl.BlockSpec((tm, dv), lambda i: (i, 0)),
                  pl.BlockSpec((tm, dv), lambda i: (i, first // dv)),
                  pl.BlockSpec((1, GLA_DV), lambda i: (0, 0))],
        out_specs=pl.BlockSpec((tm, dv), lambda i: (i, 0)),
        compiler_params=_params("parallel"),
        name="gla_post",
    )(o_f, o_b, qkvr, norm_g.reshape(1, GLA_DV))


CT = 256
CT_G = 1024
CT_COL = 128
MAX_SLABS = 16
SUB = 8


def _fft_dims(L):
    n = 2 * L
    na = 1 << ((n.bit_length() - 1) // 2)
    nb = n // na
    assert na * nb == n and nb % SUB == 0 and (na // 2) % 8 == 0
    return n, na, nb, na // 2 + 1


def _slabs_per_step(kp):
    return max(s for s in range(1, MAX_SLABS + 1) if kp % s == 0)


def _cis(num, den):
    ang = (2.0 * math.pi / den) * (num % den).astype(F32)
    return jnp.cos(ang), jnp.sin(ang)


def _expand_kernel(t_ref, o_ref):
    t = t_ref[0].astype(BF16)
    cc, cols = t.shape[1], o_ref.shape[2]
    shift, low = SUB.bit_length() - 1, SUB - 1
    src = lax.broadcasted_iota(jnp.int32, (cc, cols), 0)
    dst = lax.broadcasted_iota(jnp.int32, (cc, cols), 1)
    spread = jnp.where(lax.shift_right_logical(dst, shift) == src, 1.0, 0.0).astype(BF16)
    x = _dot(t, spread)
    row = lax.broadcasted_iota(jnp.int32, x.shape, 0)
    col = lax.broadcasted_iota(jnp.int32, x.shape, 1)
    o_ref[0] = jnp.where((col & low) == (row & low), x, 0.0).astype(o_ref.dtype)


def _expand_block_diag(t):
    g, r, c_in = t.shape
    c = -(-c_in // 16) * 16
    if c != c_in:
        return _expand_block_diag(jnp.pad(t, ((0, 0), (0, 0), (0, c - c_in))))[:, :, :c_in * SUB]
    return pl.pallas_call(
        _expand_kernel,
        out_shape=jax.ShapeDtypeStruct((g, r, c * SUB), BF16),
        grid=(g,),
        in_specs=[pl.BlockSpec((1, r, c), lambda i: (i, 0, 0))],
        out_specs=pl.BlockSpec((1, r, c * SUB), lambda i: (i, 0, 0)),
        compiler_params=_params("parallel"),
        name="dft_table_expand",
    )(t)


def _fft_tables(L):
    n, na, nb, kp = _fft_dims(L)
    ha, ng = na // 2, nb // SUB
    ar = lambda m: jnp.arange(m, dtype=jnp.int32)
    g_, ka_, bl_, a_ = ar(ng)[:, None, None, None], ar(kp)[None, :, None, None], \
        ar(SUB)[None, None, :, None], ar(ha)[None, None, None, :]
    c, s = _cis(ka_ * (nb * a_ + SUB * g_ + bl_), n)
    cs = jnp.stack([c, -s], axis=2)
    m8 = _expand_block_diag(cs.reshape(ng, kp * 2 * SUB, ha))
    c, s = _cis(ar(nb)[:, None] * ar(nb)[None, :], nb)
    f2 = jnp.stack([jnp.stack([c, s], axis=1), jnp.stack([-s, c], axis=1)], axis=0)
    f2 = f2.reshape(2, nb, 2, ng, SUB).transpose(0, 1, 3, 2, 4).reshape(2 * nb, 2 * nb)
    ca, sa = _cis(ar(kp)[:, None] * ar(nb)[None, :], n)
    cb, sb = _cis(ar(nb)[:, None] * ar(nb)[None, :], nb)
    ca, sa = (t.reshape(kp, ng, 1, SUB, 1, 1) / n for t in (ca, sa))
    cb, sb = (t.reshape(1, ng, 1, SUB, 1, nb) for t in (cb, sb))
    c, s = ca * cb - sa * sb, sa * cb + ca * sb
    gi = jnp.concatenate([jnp.concatenate([c, -s], axis=4),
                          jnp.concatenate([s, c], axis=4)], axis=2)
    gi = gi.reshape(kp, 2 * nb, 2 * nb)
    kc = ar(kp)[None, :]
    wgt = jnp.where((kc == 0) | (kc == ha), 1.0, jnp.where(kc < ha, 2.0, 0.0))
    c, s = _cis(ar(ha)[:, None] * kc, na)
    pm = jnp.stack([wgt * c, -wgt * s], axis=2).reshape(ha, 1, kp * 2)
    p8 = _expand_block_diag(jnp.broadcast_to(pm, (ha, SUB, kp * 2)).reshape(1, ha * SUB, kp * 2))[0]
    return m8, f2.astype(BF16), gi.astype(BF16), p8


def _short_conv_kernel(u_ref, w_ref, b_ref, o_ref):
    u = u_ref[...]
    L = u.shape[0]
    row = lax.broadcasted_iota(jnp.int32, u.shape, 0)
    prev = jnp.where(row == 0, 0.0, pltpu.roll(u, 1, 0))
    nxt = jnp.where(row == L - 1, 0.0, pltpu.roll(u, L - 1, 0))
    w = w_ref[...]
    o_ref[...] = prev * w[0:1] + u * w[1:2] + nxt * w[2:3] + b_ref[...]


def short_conv(u, w, b):
    L, c = u.shape[0], w.shape[1]
    w8 = jnp.zeros((8, c), F32).at[:SHORT_CONV].set(w)
    return pl.pallas_call(
        _short_conv_kernel,
        out_shape=jax.ShapeDtypeStruct((L, c), F32),
        grid=(c // CT_COL,),
        in_specs=[pl.BlockSpec((L, CT_COL), lambda j: (0, j)),
                  pl.BlockSpec((8, CT_COL), lambda j: (0, j)),
                  pl.BlockSpec((1, CT_COL), lambda j: (0, j))],
        out_specs=pl.BlockSpec((L, CT_COL), lambda j: (0, j)),
        compiler_params=_params("parallel"),
        name="short_conv",
    )(u, w8, b.reshape(1, c))


def _fft_s1_kernel(u_ref, m8_ref, a_ref):
    ha, sub, ct = u_ref.shape
    x = u_ref[...].reshape(ha * sub, ct).astype(BF16)
    r = _dot(m8_ref[0], x)
    a_ref[:, 0, :, :] = r.astype(a_ref.dtype).reshape(a_ref.shape[0], 2 * sub, ct)


def fft_stage1(u, ncols, m8, L):
    n, na, nb, kp = _fft_dims(L)
    ha, ng = na // 2, nb // SUB
    u3 = u.reshape(ha, nb, u.shape[1])
    return pl.pallas_call(
        _fft_s1_kernel,
        out_shape=jax.ShapeDtypeStruct((kp, ng, 2 * SUB, ncols), BF16),
        grid=(ng, ncols // CT_G),
        in_specs=[pl.BlockSpec((ha, SUB, CT_G), lambda g, j: (0, g, j)),
                  pl.BlockSpec((1, kp * 2 * SUB, ha * SUB), lambda g, j: (g, 0, 0))],
        out_specs=pl.BlockSpec((kp, 1, 2 * SUB, CT_G), lambda g, j: (0, g, 0, j)),
        compiler_params=_params("parallel", "parallel"),
        name="fft_stage1",
    )(u3, m8)


def _filter_hidden_kernel(zt_ref, w1_ref, b1_ref, f1_ref, w2_ref, b2_ref, f2_ref, w3_ref, b3_ref,
                          f3_ref, o_ref):
    h = jnp.sin(f1_ref[...] * (_dot3(w1_ref[...], zt_ref[...]) + b1_ref[...]))
    h = jnp.sin(f2_ref[...] * (_dot3(w2_ref[...], h) + b2_ref[...]))
    h = jnp.sin(f3_ref[...] * (_dot3(w3_ref[...], h) + b3_ref[...]))
    o_ref[...] = h.T


def filter_hidden(L, w1, b1, f1, w2, b2, f2, w3, b3, f3, tt=1024):
    hdim = FILTER_HIDDEN
    tt = min(tt, L)
    t = jnp.linspace(0.0, 1.0, L, dtype=F32)[None, :]
    omega = 2.0 * math.pi * jnp.arange(L, dtype=F32)[None, :] / L
    bands = jnp.linspace(1e-4, POS_BANDS - 1, POS_BANDS, dtype=F32)[:, None]
    zt = jnp.concatenate([t, jnp.cos(bands * omega), -jnp.sin(bands * omega)], axis=0)
    zt = jnp.zeros((LANES, L), F32).at[:POS_EMB_DIM].set(zt)
    w1t = jnp.zeros((hdim, LANES), F32).at[:, :POS_EMB_DIM].set(w1.T)
    col = lambda v: v.reshape(hdim, 1)
    full = lambda shape: pl.BlockSpec(shape, lambda i: (0, 0))
    return pl.pallas_call(
        _filter_hidden_kernel,
        out_shape=jax.ShapeDtypeStruct((L, hdim), F32),
        grid=(L // tt,),
        in_specs=[pl.BlockSpec((LANES, tt), lambda i: (0, i)),
                  full((hdim, LANES)), full((hdim, 1)), full((hdim, 1)),
                  full((hdim, hdim)), full((hdim, 1)), full((hdim, 1)),
                  full((hdim, hdim)), full((hdim, 1)), full((hdim, 1))],
        out_specs=pl.BlockSpec((tt, hdim), lambda i: (i, 0)),
        compiler_params=_params("parallel"),
        name="hyena_filter_hidden",
    )(zt, w1t, col(b1), col(f1), w2.T, col(b2), col(f2), w3.T, col(b3), col(f3))


def _filter_s1_kernel(h_ref, w4_ref, dl_ref, m8_ref, a_ref, nrm_ref, *, nb, L, cb):
    g, j = pl.program_id(0), pl.program_id(1)
    ha, sub, hdim = h_ref.shape
    rows = ha * sub
    wh = w4_ref[...].astype(BF16)
    ct = wh.shape[1]
    k = _dot(h_ref[...].reshape(rows, hdim).astype(BF16), wh).reshape(ha, sub, ct)
    a3 = lax.broadcasted_iota(jnp.int32, (ha, sub, ct), 0)
    b3 = lax.broadcasted_iota(jnp.int32, (ha, sub, ct), 1)
    tpos = a3 * nb + (b3 + g * sub)
    k = k * jnp.exp(-(tpos.astype(F32) * (1.0 / (L - 1))) * dl_ref[...])
    k = jnp.where(jnp.logical_and((j // cb) % 2 == 1, tpos == 0), 0.0, k).reshape(rows, ct)
    nrm_ref[0] = jnp.sum(jnp.abs(k), axis=0, keepdims=True)
    q = _dot(m8_ref[0], k.astype(BF16))
    a_ref[:, 0, :, :] = q.astype(a_ref.dtype).reshape(a_ref.shape[0], 2 * sub, ct)


def filter_stage1(L, w1, b1, f1, w2, b2, f2, w3, b3, f3, w4, m8):
    n, na, nb, kp = _fft_dims(L)
    ha, ng = na // 2, nb // SUB
    c = HYENA_WIDTH
    hdim = FILTER_HIDDEN
    hid = filter_hidden(L, w1, b1, f1, w2, b2, f2, w3, b3, f3).reshape(ha, nb, hdim)
    deltas = jnp.abs(jnp.linspace(MIN_DECAY, MAX_DECAY, c, dtype=F32)).reshape(1, c)
    nc = w4.shape[1]
    cb = c // CT_G
    kern = functools.partial(_filter_s1_kernel, nb=nb, L=L, cb=cb)
    return pl.pallas_call(
        kern,
        out_shape=(jax.ShapeDtypeStruct((kp, ng, 2 * SUB, nc), BF16),
                   jax.ShapeDtypeStruct((ng, 1, nc), F32)),
        grid=(ng, nc // CT_G),
        in_specs=[pl.BlockSpec((ha, SUB, hdim), lambda g, j: (0, g, 0)),
                  pl.BlockSpec((hdim, CT_G), lambda g, j: (0, j)),
                  pl.BlockSpec((1, CT_G), lambda g, j: (0, j % cb)),
                  pl.BlockSpec((1, kp * 2 * SUB, ha * SUB), lambda g, j: (g, 0, 0))],
        out_specs=(pl.BlockSpec((kp, 1, 2 * SUB, CT_G), lambda g, j: (0, g, 0, j)),
                   pl.BlockSpec((1, 1, CT_G), lambda g, j: (g, 0, j))),
        compiler_params=_params("parallel", "parallel"),
        name="hyena_filter_stage1",
    )(hid, w4, deltas, m8)


def _slab(ref, s):
    _, ng, rows, ct = ref.shape
    return ref[s].reshape(ng * rows, ct)


def _filter_spec_kernel(af_ref, ab_ref, f2_ref, nf_ref, nbk_ref, h_ref):
    f2 = f2_ref[...]
    nb = f2.shape[0] // 2
    inv = 1.0 / (jnp.sum(nf_ref[...], axis=0) + jnp.sum(nbk_ref[...], axis=0))
    for s in range(af_ref.shape[0]):
        xf = _dot(f2, _slab(af_ref, s))
        xb = _dot(f2, _slab(ab_ref, s))
        h_ref[0, s] = ((xf[:nb] + xb[:nb]) * inv).astype(h_ref.dtype)
        h_ref[1, s] = ((xf[nb:] - xb[nb:]) * inv).astype(h_ref.dtype)


def filter_spectrum(a, nrm, f2, L):
    n, na, nb, kp = _fft_dims(L)
    ng = nb // SUB
    c = HYENA_WIDTH
    cb = c // CT
    ms = _slabs_per_step(kp)
    fcol = lambda j: (j // cb) * 2 * cb + j % cb
    return pl.pallas_call(
        _filter_spec_kernel,
        out_shape=jax.ShapeDtypeStruct((2, kp, nb, HYENA_ORDER * c), BF16),
        grid=(HYENA_ORDER * cb, kp // ms),
        in_specs=[pl.BlockSpec((ms, ng, 2 * SUB, CT), lambda j, k: (k, 0, 0, fcol(j))),
                  pl.BlockSpec((ms, ng, 2 * SUB, CT), lambda j, k: (k, 0, 0, fcol(j) + cb)),
                  pl.BlockSpec((2 * nb, 2 * nb), lambda j, k: (0, 0)),
                  pl.BlockSpec((ng, 1, CT), lambda j, k: (0, 0, fcol(j))),
                  pl.BlockSpec((ng, 1, CT), lambda j, k: (0, 0, fcol(j) + cb))],
        out_specs=pl.BlockSpec((2, ms, nb, CT), lambda j, k: (0, k, 0, j)),
        compiler_params=_params("parallel", "parallel"),
        name="hyena_filter_spectrum",
    )(a, a, f2, nrm, nrm)


def _fft_mid_kernel(a_ref, h_ref, f2_ref, g_ref, q_ref):
    f2 = f2_ref[...]
    nb = f2.shape[0] // 2
    slabs, ng, rows, ct = q_ref.shape
    for s in range(slabs):
        x = _dot(f2, _slab(a_ref, s))
        xr, xi = x[:nb], x[nb:]
        hr, hi = h_ref[0, s].astype(F32), h_ref[1, s].astype(F32)
        y = jnp.concatenate([xr * hr - xi * hi, xr * hi + xi * hr], axis=0).astype(BF16)
        q_ref[s] = _dot(g_ref[s], y).astype(q_ref.dtype).reshape(ng, rows, ct)


def fft_mid(a, hf, order, f2, g, L):
    n, na, nb, kp = _fft_dims(L)
    ng = nb // SUB
    c = a.shape[3]
    cb = c // CT
    ms = _slabs_per_step(kp)
    return pl.pallas_call(
        _fft_mid_kernel,
        out_shape=jax.ShapeDtypeStruct(a.shape, BF16),
        grid=(kp // ms, cb),
        in_specs=[pl.BlockSpec((ms, ng, 2 * SUB, CT), lambda k, j: (k, 0, 0, j)),
                  pl.BlockSpec((2, ms, nb, CT), lambda k, j: (0, k, 0, order * cb + j)),
                  pl.BlockSpec((2 * nb, 2 * nb), lambda k, j: (0, 0)),
                  pl.BlockSpec((ms, 2 * nb, 2 * nb), lambda k, j: (k, 0, 0))],
        out_specs=pl.BlockSpec((ms, ng, 2 * SUB, CT), lambda k, j: (k, 0, 0, j)),
        compiler_params=_params("parallel", "parallel"),
        name="fft_mid",
    )(a, hf, f2, g)


def _fft_i2_kernel(q_ref, p8_ref, u_ref, x_ref, s_ref, o_ref):
    kp, _, rows, ct = q_ref.shape
    q = q_ref[:, 0, :, :].reshape(kp * rows, ct)
    y = _dot(p8_ref[...], q).reshape(o_ref.shape)
    o_ref[...] = x_ref[...] * (y + u_ref[...] * s_ref[...])


def fft_stage_out(q, u, gate_src, gate_blk, skip, p8, L):
    n, na, nb, kp = _fft_dims(L)
    ha, ng = na // 2, nb // SUB
    c = q.shape[3]
    u3 = u.reshape(ha, nb, u.shape[1])
    g3 = gate_src.reshape(ha, nb, gate_src.shape[1])
    out = pl.pallas_call(
        _fft_i2_kernel,
        out_shape=jax.ShapeDtypeStruct((ha, nb, c), F32),
        grid=(ng, c // CT_G),
        in_specs=[pl.BlockSpec((kp, 1, 2 * SUB, CT_G), lambda g, j: (0, g, 0, j)),
                  pl.BlockSpec((ha * SUB, kp * 2 * SUB), lambda g, j: (0, 0)),
                  pl.BlockSpec((ha, SUB, CT_G), lambda g, j: (0, g, j)),
                  pl.BlockSpec((ha, SUB, CT_G), lambda g, j: (0, g, gate_blk + j)),
                  pl.BlockSpec((1, 1, CT_G), lambda g, j: (0, 0, j))],
        out_specs=pl.BlockSpec((ha, SUB, CT_G), lambda g, j: (0, g, j)),
        compiler_params=_params("parallel", "parallel"),
        name="fft_stage_out",
    )(q, p8, u3, g3, skip.reshape(1, 1, c))
    return out.reshape(L, c)


def hyena_mixer(u_hy, conv_w, conv_b, filt, skip, tables):
    L = u_hy.shape[0]
    c = HYENA_WIDTH
    m8, f2, g, p8 = tables
    uc = short_conv(u_hy, conv_w, conv_b)
    a_filt, nrm = filter_stage1(L, *filt, m8)
    hf = filter_spectrum(a_filt, nrm, f2, L)
    z = uc
    for order in range(HYENA_ORDER):
        a = fft_stage1(z, c, m8, L)
        q = fft_mid(a, hf, order, f2, g, L)
        z = fft_stage_out(q, z, uc, (order + 1) * (c // CT_G), skip[order], p8, L)
    return z


MOE_TM = 512
MOE_TN = 512
COMBINE_TM = 256


def _moe_plan(route, counts, L):
    tm = MOE_TM
    nt = -(-(2 * L + N_EXPERTS * (tm - 1)) // tm)
    e_flat = jnp.concatenate([route[:, ROUTE_E1], route[:, ROUTE_E2]]).astype(jnp.int32)
    rank = jnp.concatenate([route[:, ROUTE_R1], route[:, ROUTE_R2]]).astype(jnp.int32)
    counts = counts[0, :N_EXPERTS].astype(jnp.int32)
    padded = ((counts + tm - 1) // tm) * tm
    ends = jnp.cumsum(padded)
    dest = (ends - padded)[e_flat] + rank
    tok = jnp.tile(jnp.arange(L, dtype=jnp.int32), 2)
    row_token = jnp.zeros((nt * tm,), jnp.int32).at[dest].set(tok, unique_indices=True)
    n_used = ends[-1] // tm
    tile_row = jnp.arange(nt, dtype=jnp.int32) * tm
    tile_expert = jnp.sum((tile_row[:, None] >= ends[None, :]).astype(jnp.int32), axis=1)
    last = jnp.take(tile_expert, jnp.maximum(n_used - 1, 0))
    tile_expert = jnp.where(jnp.arange(nt) < n_used, tile_expert, last)
    tile_first = jnp.concatenate([jnp.ones((1,), jnp.int32),
                                  (tile_expert[1:] != tile_expert[:-1]).astype(jnp.int32)])
    tile_group = jnp.cumsum(tile_first) - 1
    group_expert = jnp.zeros((N_EXPERTS,), jnp.int32).at[tile_group].set(tile_expert)
    n_groups = jnp.take(tile_group, jnp.maximum(n_used - 1, 0)) + 1
    i32 = lambda v: v.reshape(1).astype(jnp.int32)
    last_row = (ends - padded + counts)[tile_expert]
    tile_valid = jnp.where(jnp.arange(nt) < n_used, jnp.clip(last_row - tile_row, 0, tm), 0)
    tile_chunks = ((tile_valid + GATHER_CHUNK - 1) // GATHER_CHUNK).astype(jnp.int32)
    return dict(nt=nt, row_token=row_token, dest=dest, tile_chunks=tile_chunks,
                scalars=(tile_expert, tile_first, i32(n_used), tile_group.astype(jnp.int32),
                         i32(n_groups), group_expert))


def _row_copy(src_hbm, idx, buf, slot, r, sem):
    return pltpu.make_async_copy(src_hbm.at[pl.ds(idx, 1)], buf.at[slot, pl.ds(r, 1)],
                                 sem.at[slot])


def _gather_rows(idx_ref, src_hbm, buf, slot, sem, n_rows, start):
    if not start:
        pltpu.make_async_copy(src_hbm.at[pl.ds(0, n_rows)], buf.at[slot], sem.at[slot]).wait()
        return

    def body(r, carry):
        _row_copy(src_hbm, idx_ref[0, 0, r], buf, slot, r, sem).start(priority=1)
        return carry
    lax.fori_loop(0, n_rows, body, 0, unroll=8)


def _pipelined_gather(cur_ref, nxt_ref, src_hbm, buf, sem, n_rows):
    i, n = pl.program_id(0), pl.num_programs(0)
    slot = lax.rem(i, 2)

    @pl.when(i == 0)
    def _():
        _gather_rows(cur_ref, src_hbm, buf, 0, sem, n_rows, True)

    @pl.when(i + 1 < n)
    def _():
        _gather_rows(nxt_ref, src_hbm, buf, 1 - slot, sem, n_rows, True)

    _gather_rows(cur_ref, src_hbm, buf, slot, sem, n_rows, False)
    return slot


GATHER_CHUNK = 32


def _dispatch_rows(idx_ref, h_hbm, buf, slot, sem, n_chunks, start):
    def body(c, carry):
        base = pl.multiple_of(c * GATHER_CHUNK, GATHER_CHUNK)
        if start:
            for r in range(GATHER_CHUNK):
                _row_copy(h_hbm, idx_ref[0, 0, base + r], buf, slot, base + r, sem).start(priority=1)
        else:
            pltpu.make_async_copy(h_hbm.at[pl.ds(0, GATHER_CHUNK)],
                                  buf.at[slot, pl.ds(base, GATHER_CHUNK)], sem.at[slot]).wait()
        return carry
    lax.fori_loop(0, n_chunks, body, 0)


def _dispatch_kernel(nch_ref, cur_ref, nxt_ref, h_hbm, o_ref, buf, sem):
    i, n = pl.program_id(0), pl.num_programs(0)
    slot = lax.rem(i, 2)

    @pl.when(i == 0)
    def _():
        buf[...] = jnp.zeros_like(buf)
        _dispatch_rows(cur_ref, h_hbm, buf, 0, sem, nch_ref[0], True)

    @pl.when(i + 1 < n)
    def _():
        _dispatch_rows(nxt_ref, h_hbm, buf, 1 - slot, sem, nch_ref[jnp.minimum(i + 1, n - 1)], True)

    _dispatch_rows(cur_ref, h_hbm, buf, slot, sem, nch_ref[i], False)
    o_ref[...] = buf[slot].astype(o_ref.dtype)


def moe_dispatch(h, row_token, tile_chunks, nt):
    L, d = h.shape
    idx = row_token.reshape(nt, 1, MOE_TM)
    smem = lambda f: pl.BlockSpec((1, 1, MOE_TM), f, memory_space=pltpu.SMEM)
    return pl.pallas_call(
        _dispatch_kernel,
        out_shape=jax.ShapeDtypeStruct((nt * MOE_TM, d), BF16),
        grid_spec=pltpu.PrefetchScalarGridSpec(
            num_scalar_prefetch=1,
            grid=(nt,),
            in_specs=[smem(lambda i, nch: (i, 0, 0)),
                      smem(lambda i, nch: (jnp.minimum(i + 1, nt - 1), 0, 0)),
                      pl.BlockSpec(memory_space=pl.ANY)],
            out_specs=pl.BlockSpec((MOE_TM, d), lambda i, nch: (i, 0)),
            scratch_shapes=[pltpu.VMEM((2, MOE_TM, d), F32), pltpu.SemaphoreType.DMA((2,))]),
        compiler_params=_params("arbitrary"),
        name="moe_dispatch",
    )(tile_chunks, idx, idx, h)


def _weight_copy(w_hbm, e, jcol, wbuf, k, slot, sem):
    tn = wbuf.shape[3]
    return pltpu.make_async_copy(w_hbm.at[e, :, pl.ds(pl.multiple_of(jcol * tn, tn), tn)],
                                 wbuf.at[k, slot], sem.at[k, slot])


def _refresh_expert_weights(scalars, w_hbms, wbuf, wbs, sem):
    te_ref, tf_ref, _, tg_ref, ng_ref, ge_ref = scalars
    j, i = pl.program_id(0), pl.program_id(1)
    nj = pl.num_programs(0)

    @pl.when(tf_ref[i] == 1)
    def _():
        g, ng = tg_ref[i], ng_ref[0]
        blk = j * ng + g
        slot = lax.rem(blk, 2)

        @pl.when(blk == 0)
        def _():
            for k, w in enumerate(w_hbms):
                _weight_copy(w, te_ref[i], j, wbuf, k, 0, sem).start()

        for k, w in enumerate(w_hbms):
            _weight_copy(w, 0, 0, wbuf, k, slot, sem).wait()
        for k, wb in enumerate(wbs):
            wb[...] = wbuf[k, slot].astype(BF16)

        wrap = g + 1 == ng
        nxt_e = ge_ref[jnp.where(wrap, 0, g + 1)]
        nxt_j = j + wrap.astype(jnp.int32)

        @pl.when(nxt_j < nj)
        def _():
            for k, w in enumerate(w_hbms):
                _weight_copy(w, nxt_e, nxt_j, wbuf, k, 1 - slot, sem).start()


def _moe_up_kernel(te_ref, tf_ref, nu_ref, tg_ref, ng_ref, ge_ref, a_ref, wg_hbm, wu_hbm, o_ref,
                   wbuf, wgb, wub, sem):
    i = pl.program_id(1)
    _refresh_expert_weights((te_ref, tf_ref, nu_ref, tg_ref, ng_ref, ge_ref),
                            (wg_hbm, wu_hbm), wbuf, (wgb, wub), sem)

    @pl.when(i < nu_ref[0])
    def _():
        a = a_ref[...]
        o_ref[...] = (_silu(_dot(a, wgb[...])) * _dot(a, wub[...])).astype(o_ref.dtype)

    @pl.when(i >= nu_ref[0])
    def _():
        o_ref[...] = jnp.zeros_like(o_ref)


def moe_up(hs, w_gate, w_up, plan):
    r, d = hs.shape
    f = w_gate.shape[2]
    nt = plan["nt"]
    anyspec = pl.BlockSpec(memory_space=pl.ANY)
    return pl.pallas_call(
        _moe_up_kernel,
        out_shape=jax.ShapeDtypeStruct((r, f), BF16),
        grid_spec=pltpu.PrefetchScalarGridSpec(
            num_scalar_prefetch=6,
            grid=(f // MOE_TN, nt),
            in_specs=[pl.BlockSpec((MOE_TM, d), lambda j, i, *_: (i, 0)), anyspec, anyspec],
            out_specs=pl.BlockSpec((MOE_TM, MOE_TN), lambda j, i, *_: (i, j)),
            scratch_shapes=[pltpu.VMEM((2, 2, d, MOE_TN), F32),
                            pltpu.VMEM((d, MOE_TN), BF16), pltpu.VMEM((d, MOE_TN), BF16),
                            pltpu.SemaphoreType.DMA((2, 2))]),
        compiler_params=_params("arbitrary", "arbitrary"),
        name="moe_up",
    )(*plan["scalars"], hs, w_gate, w_up)


def _moe_down_kernel(te_ref, tf_ref, nu_ref, tg_ref, ng_ref, ge_ref, t_ref, wd_hbm, o_ref,
                     wbuf, wdb, sem):
    i = pl.program_id(1)
    _refresh_expert_weights((te_ref, tf_ref, nu_ref, tg_ref, ng_ref, ge_ref),
                            (wd_hbm,), wbuf, (wdb,), sem)

    @pl.when(i < nu_ref[0])
    def _():
        o_ref[...] = _dot(t_ref[...], wdb[...])

    @pl.when(i >= nu_ref[0])
    def _():
        o_ref[...] = jnp.zeros_like(o_ref)


def moe_down(t, w_down, plan):
    r, f = t.shape
    d = w_down.shape[2]
    nt = plan["nt"]
    return pl.pallas_call(
        _moe_down_kernel,
        out_shape=jax.ShapeDtypeStruct((r, d), F32),
        grid_spec=pltpu.PrefetchScalarGridSpec(
            num_scalar_prefetch=6,
            grid=(d // MOE_TN, nt),
            in_specs=[pl.BlockSpec((MOE_TM, f), lambda j, i, *_: (i, 0)),
                      pl.BlockSpec(memory_space=pl.ANY)],
            out_specs=pl.BlockSpec((MOE_TM, MOE_TN), lambda j, i, *_: (i, j)),
            scratch_shapes=[pltpu.VMEM((1, 2, f, MOE_TN), F32), pltpu.VMEM((f, MOE_TN), BF16),
                            pltpu.SemaphoreType.DMA((1, 2))]),
        compiler_params=_params("arbitrary", "arbitrary"),
        name="moe_down",
    )(*plan["scalars"], t, w_down)


def _combine_kernel(cur_ref, nxt_ref, y_hbm, x_ref, route_ref, g_ref, o_ref, buf, sem, *, final):
    tm = x_ref.shape[0]
    slot = _pipelined_gather(cur_ref, nxt_ref, y_hbm, buf, sem, 2 * tm)
    route = route_ref[...]
    g1 = route[:, ROUTE_G1:ROUTE_G1 + 1]
    g2 = route[:, ROUTE_G2:ROUTE_G2 + 1]
    x = x_ref[...] + g1 * buf[slot, :tm, :] + g2 * buf[slot, tm:, :]
    if final:
        ms = jnp.mean(x * x, axis=-1, keepdims=True)
        x = x * lax.rsqrt(ms + RMS_EPS) * g_ref[...]
    o_ref[...] = x


def moe_combine(x, y, dest, route, final_gain):
    L, d = x.shape
    tm = COMBINE_TM
    nt = L // tm
    idx = jnp.concatenate([dest[:L].reshape(nt, 1, tm), dest[L:].reshape(nt, 1, tm)], axis=2)
    smem = lambda f: pl.BlockSpec((1, 1, 2 * tm), f, memory_space=pltpu.SMEM)
    final = final_gain is not None
    gain = (final_gain if final else jnp.ones((d,), F32)).reshape(1, d)
    kern = functools.partial(_combine_kernel, final=final)
    return pl.pallas_call(
        kern,
        out_shape=jax.ShapeDtypeStruct((L, d), F32),
        grid=(nt,),
        in_specs=[smem(lambda i: (i, 0, 0)),
                  smem(lambda i: (jnp.minimum(i + 1, nt - 1), 0, 0)),
                  pl.BlockSpec(memory_space=pl.ANY),
                  pl.BlockSpec((tm, d), lambda i: (i, 0)),
                  pl.BlockSpec((tm, LANES), lambda i: (i, 0)),
                  pl.BlockSpec((1, d), lambda i: (0, 0))],
        out_specs=pl.BlockSpec((tm, d), lambda i: (i, 0)),
        scratch_shapes=[pltpu.VMEM((2, 2 * tm, d), F32), pltpu.SemaphoreType.DMA((2,))],
        compiler_params=_params("arbitrary"),
        name="moe_combine",
    )(idx, idx, y, x, route, gain)


def moe_ffn(x, norm_g, router_w, w_gate, w_up, w_down, final_gain):
    L = x.shape[0]
    h, route, counts = rmsnorm_router(x, norm_g, router_w)
    plan = _moe_plan(route, counts, L)
    hs = moe_dispatch(h, plan["row_token"], plan["tile_chunks"], plan["nt"])
    t = moe_up(hs, w_gate, w_up, plan)
    y = moe_down(t, w_down, plan)
    return moe_combine(x, y, plan["dest"], route, final_gain)


def _layer_mixers(x, l, tables, norm_mix, w_in_t, conv_w, conv_b, filt, hyena_skip, w_branch_a,
                  gla_gate_w2, gla_gate_b, gla_norm, w_branch_b, w_out):
    hc = (HYENA_ORDER + 1) * HYENA_WIDTH
    o_lr = hc + 2 * GLA_DK_TOTAL + 2 * GLA_DV_TOTAL
    o_g = o_lr + 2 * GLA_LOWRANK
    h = rmsnorm(x, norm_mix[l], BF16)
    proj = matmul([h], [(w_in_t, l, 0)], o_lr, [(0, 0)], _ep_plain, [], F32,
                  tm=1024, tn=1024, name="in_proj", w_transposed=True)
    gates = matmul([h], [(w_in_t[l, o_g:][None], 0, 0)], 2 * D_MODEL, [(0, 0)], _ep_sigmoid, [],
                   BF16, tm=1024, tn=1024, name="in_proj_gates", w_transposed=True)
    la = gla_gates(h, w_in_t[l, o_lr:o_g], gla_gate_w2[l], gla_gate_b[l])

    z_a = hyena_mixer(proj, conv_w[l], conv_b[l], filt, hyena_skip[l], tables)
    o_f, o_b = gla_scan(proj, hc, la)
    z_b = gla_post(o_f, o_b, proj, hc + 2 * GLA_DK_TOTAL + GLA_DV_TOTAL, gla_norm[l])

    mixed = matmul([z_a, z_b], [(w_branch_a, l, 0), (w_branch_b, l, 0)], D_MODEL,
                   [(0, 0), (1, 1)], _ep_merge, [(gates, 0), (gates, D_MODEL)], BF16,
                   tm=1024, tn=1024, name="branch_merge")
    return matmul([mixed], [(w_out, l, 0)], D_MODEL, [(0, 0)], _ep_residual, [(x, 0)], F32,
                  tm=1024, tn=512, name="out_proj")


def _swiglu_ffn(x, h, wg, wu, wd, i):
    f = wg.shape[2]
    t = matmul([h], [(wg, i, 0), (wu, i, 0)], f, [(0, 0), (0, 1)], _ep_swiglu, [], BF16,
               tm=1024, tn=512, name="ffn_up")
    return matmul([t], [(wd, i, 0)], wd.shape[2], [(0, 0)], _ep_residual, [(x, 0)], F32,
                  tm=512, tn=512, name="ffn_down")


def kernel(x, norm_mix, w_in, conv_w, conv_b, filt_w1, filt_b1, filt_freq1, filt_w2, filt_b2, filt_freq2, filt_w3, filt_b3, filt_freq3, filt_w4, hyena_skip, w_branch_a, gla_gate_w2, gla_gate_b, gla_norm, w_branch_b, w_out, norm_ffn, dense_w_gate, dense_w_up, dense_w_down, router_w, moe_w_gate, moe_w_up, moe_w_down, norm_final):
    b, L, d = x.shape
    assert b == 1
    x = x.reshape(L, d)
    tables = _fft_tables(L)
    w_in_t = jnp.swapaxes(w_in, 1, 2)
    for l in range(DEPTH):
        filt = (filt_w1[l], filt_b1[l], filt_freq1[l], filt_w2[l], filt_b2[l], filt_freq2[l],
                filt_w3[l], filt_b3[l], filt_freq3[l], filt_w4[l])
        x = _layer_mixers(x, l, tables, norm_mix, w_in_t, conv_w, conv_b, filt, hyena_skip,
                          w_branch_a, gla_gate_w2, gla_gate_b, gla_norm, w_branch_b, w_out)
        i = l // 2
        last = l == DEPTH - 1
        if l % 2 == 0:
            h = rmsnorm(x, norm_ffn[l], BF16)
            x = _swiglu_ffn(x, h, dense_w_gate, dense_w_up, dense_w_down, i)
            if last:
                x = rmsnorm(x, norm_final, F32)
        else:
            x = moe_ffn(x, norm_ffn[l], router_w[i], moe_w_gate[i], moe_w_up[i], moe_w_down[i],
                        norm_final if last else None)
    return x.reshape(b, L, d)
```

```python
import functools
import math

import jax
import jax.numpy as jnp
from jax import lax
from jax.experimental import pallas as pl
from jax.experimental.pallas import tpu as pltpu

F32 = jnp.float32
BF16 = jnp.bfloat16

D_MODEL = 2048
DEPTH = 2
HYENA_WIDTH = D_MODEL // 2
HYENA_ORDER = 2
SHORT_CONV = 3
POS_EMB_DIM = 33
POS_BANDS = (POS_EMB_DIM - 1) // 2
FILTER_HIDDEN = 64
MIN_DECAY = math.log(1e-2) / 1.5
MAX_DECAY = math.log(1e-2) / 0.3
GLA_HEADS = 4
GLA_DK = 128
GLA_DV = 256
GLA_DK_TOTAL = GLA_HEADS * GLA_DK
GLA_DV_TOTAL = GLA_HEADS * GLA_DV
GLA_LOWRANK = 16
GLA_GATE_TAU = 16.0
GLA_CHUNK = 64
N_EXPERTS = 8
RMS_EPS = 1e-6

LANES = 128
VMEM_LIMIT_BYTES = 56 * 1024 * 1024


def _params(*sem):
    return pltpu.CompilerParams(dimension_semantics=sem, vmem_limit_bytes=VMEM_LIMIT_BYTES)


def _dot(a, b):
    return jnp.dot(a, b, preferred_element_type=F32)


def _split(a):
    hi = a.astype(BF16)
    lo = (a - hi.astype(F32)).astype(BF16)
    return hi, lo


def _dot3(a, b):
    ah, al = _split(a)
    bh, bl = _split(b)
    return _dot(ah, bh) + (_dot(ah, bl) + _dot(al, bh))


def _sigmoid(x):
    return 1.0 / (1.0 + jnp.exp(-x))


def _silu(x):
    return x * _sigmoid(x)


def _log_sigmoid(x):
    return jnp.minimum(x, 0.0) - jnp.log(1.0 + jnp.exp(-jnp.abs(x)))


def _rmsnorm_kernel(x_ref, g_ref, o_ref):
    x = x_ref[...]
    ms = jnp.mean(x * x, axis=-1, keepdims=True)
    o_ref[...] = (x * lax.rsqrt(ms + RMS_EPS) * g_ref[...]).astype(o_ref.dtype)


def rmsnorm(x, g, out_dtype, tm=512):
    m, d = x.shape
    return pl.pallas_call(
        _rmsnorm_kernel,
        out_shape=jax.ShapeDtypeStruct((m, d), out_dtype),
        grid=(m // tm,),
        in_specs=[pl.BlockSpec((tm, d), lambda i: (i, 0)),
                  pl.BlockSpec((1, d), lambda i: (0, 0))],
        out_specs=pl.BlockSpec((tm, d), lambda i: (i, 0)),
        compiler_params=_params("parallel"),
        name="rmsnorm",
    )(x, g.reshape(1, d))


def _rmsnorm_router_kernel(x_ref, g_ref, rw_ref, o_ref, cw_ref, cnt_ref):
    @pl.when(pl.program_id(0) == 0)
    def _():
        cnt_ref[...] = jnp.zeros_like(cnt_ref)

    x = x_ref[...]
    ms = jnp.mean(x * x, axis=-1, keepdims=True)
    h = x * lax.rsqrt(ms + RMS_EPS) * g_ref[...]
    o_ref[...] = h.astype(o_ref.dtype)
    logits = _dot3(h, rw_ref[...])
    col = lax.broadcasted_iota(jnp.int32, logits.shape, 1).astype(F32)
    neg = jnp.float32(-jnp.inf)
    logits = jnp.where(col < N_EXPERTS, logits, neg)
    m1 = jnp.max(logits, axis=-1, keepdims=True)
    i1 = jnp.min(jnp.where(logits == m1, col, float(LANES)), axis=-1, keepdims=True)
    rest = jnp.where(col == i1, neg, logits)
    m2 = jnp.max(rest, axis=-1, keepdims=True)
    i2 = jnp.min(jnp.where(rest == m2, col, float(LANES)), axis=-1, keepdims=True)
    e2 = jnp.exp(m2 - m1)
    g1 = 1.0 / (1.0 + e2)
    g2 = e2 / (1.0 + e2)
    hit1, hit2 = col == i1, col == i2
    c = jnp.where(hit1, 1.0, 0.0) + jnp.where(hit2, 1.0, 0.0)
    tm = c.shape[0]
    rows = lax.broadcasted_iota(jnp.int32, c.shape, 0)
    incl, step = c, 1
    while step < tm:
        incl = incl + jnp.where(rows >= step, pltpu.roll(incl, step, 0), 0.0)
        step *= 2
    before = incl - c + cnt_ref[...]
    r1 = jnp.sum(jnp.where(hit1, before, 0.0), axis=-1, keepdims=True)
    r2 = jnp.sum(jnp.where(hit2, before, 0.0), axis=-1, keepdims=True)
    cnt_ref[...] = cnt_ref[...] + incl[tm - 1:tm]
    cw_ref[...] = (jnp.where(col == ROUTE_E1, i1, 0.0) + jnp.where(col == ROUTE_E2, i2, 0.0)
                   + jnp.where(col == ROUTE_G1, g1, 0.0) + jnp.where(col == ROUTE_G2, g2, 0.0)
                   + jnp.where(col == ROUTE_R1, r1, 0.0) + jnp.where(col == ROUTE_R2, r2, 0.0))


ROUTE_E1, ROUTE_E2, ROUTE_G1, ROUTE_G2, ROUTE_R1, ROUTE_R2 = 0, 1, 2, 3, 4, 5


def rmsnorm_router(x, g, router_w, tm=256):
    m, d = x.shape
    rw = jnp.zeros((d, LANES), F32).at[:, :N_EXPERTS].set(router_w)
    return pl.pallas_call(
        _rmsnorm_router_kernel,
        out_shape=(jax.ShapeDtypeStruct((m, d), F32), jax.ShapeDtypeStruct((m, LANES), F32),
                   jax.ShapeDtypeStruct((1, LANES), F32)),
        grid=(m // tm,),
        in_specs=[pl.BlockSpec((tm, d), lambda i: (i, 0)),
                  pl.BlockSpec((1, d), lambda i: (0, 0)),
                  pl.BlockSpec((d, LANES), lambda i: (0, 0))],
        out_specs=(pl.BlockSpec((tm, d), lambda i: (i, 0)),
                   pl.BlockSpec((tm, LANES), lambda i: (i, 0)),
                   pl.BlockSpec((1, LANES), lambda i: (0, 0))),
        compiler_params=_params("arbitrary"),
        name="rmsnorm_router",
    )(x, g.reshape(1, d), rw)


def _mm_kernel(*refs, n_a, n_w, dots, n_extra, epilogue, w_transposed):
    a_refs = refs[:n_a]
    w_refs = refs[n_a:n_a + n_w]
    e_refs = refs[n_a + n_w:n_a + n_w + n_extra]
    o_ref = refs[n_a + n_w + n_extra]
    wb_refs = refs[n_a + n_w + n_extra + 1:]

    @pl.when(pl.program_id(1) == 0)
    def _():
        for w, wb in zip(w_refs, wb_refs):
            wb[...] = w[0].astype(BF16)

    a_vals = [a[...].astype(BF16) for a in a_refs]
    nt_dims = (((1,), (1,)), ((), ()))
    accs = [lax.dot_general(a_vals[ai], wb_refs[wi][...], nt_dims, preferred_element_type=F32)
            if w_transposed else _dot(a_vals[ai], wb_refs[wi][...]) for ai, wi in dots]
    o_ref[...] = epilogue(accs, [e[...] for e in e_refs]).astype(o_ref.dtype)


def matmul(a_list, w_list, n, dots, epilogue, extras, out_dtype, *, tm, tn, name,
           w_transposed=False):
    m = a_list[0].shape[0]
    assert m % tm == 0 and n % tn == 0
    in_specs = [pl.BlockSpec((tm, a.shape[1]), lambda j, i: (i, 0)) for a in a_list]
    w_arrays, scratch = [], []
    for w, s, first in w_list:
        assert first % tn == 0
        w_arrays.append(w)
        if w_transposed:
            in_specs.append(pl.BlockSpec((1, tn, w.shape[2]),
                                         lambda j, i, s=s, off=first // tn: (s, off + j, 0)))
            scratch.append(pltpu.VMEM((tn, w.shape[2]), BF16))
            continue
        in_specs.append(pl.BlockSpec((1, w.shape[1], tn),
                                     lambda j, i, s=s, off=first // tn: (s, 0, off + j)))
        scratch.append(pltpu.VMEM((w.shape[1], tn), BF16))
    extra_arrays = []
    for arr, first in extras:
        assert first % tn == 0
        extra_arrays.append(arr)
        in_specs.append(pl.BlockSpec((tm, tn), lambda j, i, off=first // tn: (i, off + j)))
    kern = functools.partial(_mm_kernel, n_a=len(a_list), n_w=len(w_list), dots=tuple(dots),
                             n_extra=len(extras), epilogue=epilogue, w_transposed=w_transposed)
    return pl.pallas_call(
        kern,
        out_shape=jax.ShapeDtypeStruct((m, n), out_dtype),
        grid=(n // tn, m // tm),
        in_specs=in_specs,
        out_specs=pl.BlockSpec((tm, tn), lambda j, i: (i, j)),
        scratch_shapes=scratch,
        compiler_params=_params("arbitrary", "arbitrary"),
        name=name,
    )(*a_list, *w_arrays, *extra_arrays)


def _gate_proj_kernel(h_ref, wt_hbm, o_ref, wbuf, wb, sem, *, layer, first):
    j, i = pl.program_id(0), pl.program_id(1)
    tn = wb.shape[0]

    def copy(jj, slot):
        rows = pl.ds(pl.multiple_of(first + jj * tn, 8), tn)
        return pltpu.make_async_copy(wt_hbm.at[layer, rows, :], wbuf.at[slot], sem.at[slot])

    @pl.when(i == 0)
    def _():
        slot = lax.rem(j, 2)

        @pl.when(j == 0)
        def _():
            copy(0, 0).start()

        copy(j, slot).wait()
        wb[...] = wbuf[slot].astype(BF16)

        @pl.when(j + 1 < pl.num_programs(0))
        def _():
            copy(j + 1, 1 - slot).start()

    acc = lax.dot_general(h_ref[...], wb[...], (((1,), (1,)), ((), ())), preferred_element_type=F32)
    o_ref[...] = _sigmoid(acc).astype(o_ref.dtype)


def gate_projection(h, w_t, layer, first, n, tm=1024, tn=1024):
    m, k = h.shape
    assert first % 8 == 0 and n % tn == 0 and m % tm == 0
    kern = functools.partial(_gate_proj_kernel, layer=layer, first=first)
    return pl.pallas_call(
        kern,
        out_shape=jax.ShapeDtypeStruct((m, n), BF16),
        grid=(n // tn, m // tm),
        in_specs=[pl.BlockSpec((tm, k), lambda j, i: (i, 0)), pl.BlockSpec(memory_space=pl.ANY)],
        out_specs=pl.BlockSpec((tm, tn), lambda j, i: (i, j)),
        scratch_shapes=[pltpu.VMEM((2, tn, k), F32), pltpu.VMEM((tn, k), BF16),
                        pltpu.SemaphoreType.DMA((2,))],
        compiler_params=_params("arbitrary", "arbitrary"),
        name="in_proj_gates",
    )(h, w_t)


def _ep_plain(accs, ex):
    return accs[0]


def _ep_residual(accs, ex):
    return ex[0] + accs[0]


def _ep_merge(accs, ex):
    return ex[0].astype(F32) * accs[0] + ex[1].astype(F32) * accs[1]


def _ep_swiglu(accs, ex):
    return _silu(accs[0]) * accs[1]


def _gate_kernel(h_ref, wlr_ref, w2_ref, b_ref, o_ref):
    lr = lax.dot_general(h_ref[...], wlr_ref[...].astype(BF16), (((1,), (1,)), ((), ())),
                         preferred_element_type=F32)
    z = _dot3(lr, w2_ref[...]) + b_ref[...]
    o_ref[...] = _log_sigmoid(z) * (1.0 / GLA_GATE_TAU)


def gla_gates(h, w_lr_t, gate_w2, gate_b, tm=512):
    m, d = h.shape
    r = GLA_LOWRANK
    n = 2 * GLA_DK_TOTAL
    wlr = jnp.zeros((LANES, d), F32).at[:2 * r].set(w_lr_t)
    w2 = jnp.zeros((LANES, n), F32)
    w2 = w2.at[:r, :GLA_DK_TOTAL].set(gate_w2[0]).at[r:2 * r, GLA_DK_TOTAL:].set(gate_w2[1])
    b = gate_b.reshape(1, n)
    return pl.pallas_call(
        _gate_kernel,
        out_shape=jax.ShapeDtypeStruct((m, n), F32),
        grid=(m // tm,),
        in_specs=[pl.BlockSpec((tm, d), lambda i: (i, 0)),
                  pl.BlockSpec((LANES, d), lambda i: (0, 0)),
                  pl.BlockSpec((LANES, n), lambda i: (0, 0)),
                  pl.BlockSpec((1, n), lambda i: (0, 0))],
        out_specs=pl.BlockSpec((tm, n), lambda i: (i, 0)),
        compiler_params=_params("parallel"),
        name="gla_gates",
    )(h, wlr, w2, b)


GLA_SUB = 4


def _gla_chunk(q, k, v, la, st_ref, h, reverse):
    c = GLA_CHUNK
    row = lax.broadcasted_iota(jnp.int32, (c, c), 0)
    col = lax.broadcasted_iota(jnp.int32, (c, c), 1)
    keep = (col > row) if reverse else (col <= row)
    b = la
    rows = lax.broadcasted_iota(jnp.int32, la.shape, 0)
    step = 1
    while step < c:
        if reverse:
            b = b + jnp.where(rows < c - step, pltpu.roll(b, c - step, 0), 0.0)
        else:
            b = b + jnp.where(rows >= step, pltpu.roll(b, step, 0), 0.0)
        step *= 2
    tot = b[0:1] if reverse else b[c - 1:c]
    q_in = (q * ((GLA_DK ** -0.5) * jnp.exp(b))).astype(BF16)
    k_dec = k * jnp.exp(-b)
    k_in = k_dec.astype(BF16)
    k_out = (k_dec * jnp.exp(tot)).astype(BF16)
    vb = v.astype(BF16)
    s = lax.dot_general(q_in, k_in, (((1,), (1,)), ((), ())), preferred_element_type=F32)
    s = jnp.where(keep, s, 0.0).astype(BF16)
    st = st_ref[h]
    o = _dot(s, vb) + lax.dot_general(q_in, st.astype(BF16), (((1,), (1,)), ((), ())),
                                      preferred_element_type=F32)
    upd = lax.dot_general(vb, k_out, (((0,), (0,)), ((), ())), preferred_element_type=F32)
    st_ref[h] = st * jnp.exp(tot) + upd
    return o


def _gla_kernel(qf_ref, kf_ref, vf_ref, lf_ref, qb_ref, kb_ref, vb_ref, lb_ref,
                of_ref, ob_ref, sf_ref, sb_ref):
    @pl.when(pl.program_id(0) == 0)
    def _():
        sf_ref[...] = jnp.zeros_like(sf_ref)
        sb_ref[...] = jnp.zeros_like(sb_ref)

    def body(s, carry):
        rf = pl.multiple_of(s * GLA_CHUNK, GLA_CHUNK)
        rb = pl.multiple_of((GLA_SUB - 1 - s) * GLA_CHUNK, GLA_CHUNK)
        for h in range(GLA_HEADS):
            kk = slice(h * GLA_DK, (h + 1) * GLA_DK)
            vv = slice(h * GLA_DV, (h + 1) * GLA_DV)
            of_ref[pl.ds(rf, GLA_CHUNK), vv] = _gla_chunk(
                qf_ref[pl.ds(rf, GLA_CHUNK), kk], kf_ref[pl.ds(rf, GLA_CHUNK), kk],
                vf_ref[pl.ds(rf, GLA_CHUNK), vv], lf_ref[pl.ds(rf, GLA_CHUNK), kk],
                sf_ref, h, False)
            ob_ref[pl.ds(rb, GLA_CHUNK), vv] = _gla_chunk(
                qb_ref[pl.ds(rb, GLA_CHUNK), kk], kb_ref[pl.ds(rb, GLA_CHUNK), kk],
                vb_ref[pl.ds(rb, GLA_CHUNK), vv], lb_ref[pl.ds(rb, GLA_CHUNK), kk],
                sb_ref, h, True)
        return carry

    lax.fori_loop(0, GLA_SUB, body, 0, unroll=True)


def gla_scan(qkvr, first, la):
    L = qkvr.shape[0]
    rows = GLA_CHUNK * GLA_SUB
    n = L // rows
    dk, dv = GLA_DK_TOTAL, GLA_DV_TOTAL
    assert first % dv == 0
    qb, vb = first // dk, (first + 2 * dk) // dv
    specs = []
    for rev in (False, True):
        blk = (lambda i: n - 1 - i) if rev else (lambda i: i)
        specs += [pl.BlockSpec((rows, dk), lambda i, blk=blk: (blk(i), qb)),
                  pl.BlockSpec((rows, dk), lambda i, blk=blk: (blk(i), qb + 1)),
                  pl.BlockSpec((rows, dv), lambda i, blk=blk: (blk(i), vb)),
                  pl.BlockSpec((rows, dk), lambda i, blk=blk, c=int(rev): (blk(i), c))]
    return pl.pallas_call(
        _gla_kernel,
        out_shape=(jax.ShapeDtypeStruct((L, dv), F32), jax.ShapeDtypeStruct((L, dv), F32)),
        grid=(n,),
        in_specs=specs,
        out_specs=(pl.BlockSpec((rows, dv), lambda i: (i, 0)),
                   pl.BlockSpec((rows, dv), lambda i: (n - 1 - i, 0))),
        scratch_shapes=[pltpu.VMEM((GLA_HEADS, GLA_DV, GLA_DK), F32),
                        pltpu.VMEM((GLA_HEADS, GLA_DV, GLA_DK), F32)],
        compiler_params=_params("arbitrary"),
        name="gla_scan",
    )(qkvr, qkvr, qkvr, la, qkvr, qkvr, qkvr, la)


def _gla_post_kernel(of_ref, ob_ref, r_ref, g_ref, o_ref):
    g = g_ref[...]
    for h in range(GLA_HEADS):
        vv = slice(h * GLA_DV, (h + 1) * GLA_DV)
        o = of_ref[:, vv] + ob_ref[:, vv]
        ms = jnp.mean(o * o, axis=-1, keepdims=True)
        y = o * lax.rsqrt(ms + RMS_EPS) * g
        o_ref[:, vv] = (y * _silu(r_ref[:, vv])).astype(o_ref.dtype)


def gla_post(o_f, o_b, qkvr, first, norm_g, tm=512):
    L, dv = o_f.shape
    assert first % dv == 0
    return pl.pallas_call(
        _gla_post_kernel,
        out_shape=jax.ShapeDtypeStruct((L, dv), BF16),
        grid=(L // tm,),
        in_specs=[pl.BlockSpec((tm, dv), lambda i: (i, 0)),
                  pl.BlockSpec((tm, dv), lambda i: (i, 0)),
                  pl.BlockSpec((tm, dv), lambda i: (i, first // dv)),
                  pl.BlockSpec((1, GLA_DV), lambda i: (0, 0))],
        out_specs=pl.BlockSpec((tm, dv), lambda i: (i, 0)),
        compiler_params=_params("parallel"),
        name="gla_post",
    )(o_f, o_b, qkvr, norm_g.reshape(1, GLA_DV))


CT = 256
CT_G = 1024
CT_COL = 128
MAX_SLABS = 16
SUB = 8


def _fft_dims(L):
    n = 2 * L
    na = 1 << ((n.bit_length() - 1) // 2)
    nb = n // na
    assert na * nb == n and nb % SUB == 0 and (na // 2) % 8 == 0
    return n, na, nb, na // 2 + 1


def _slabs_per_step(kp):
    return max(s for s in range(1, MAX_SLABS + 1) if kp % s == 0)


def _cis(num, den):
    ang = (2.0 * math.pi / den) * (num % den).astype(F32)
    return jnp.cos(ang), jnp.sin(ang)


def _expand_kernel(t_ref, o_ref):
    t = t_ref[0].astype(BF16)
    cc, cols = t.shape[1], o_ref.shape[2]
    shift, low = SUB.bit_length() - 1, SUB - 1
    src = lax.broadcasted_iota(jnp.int32, (cc, cols), 0)
    dst = lax.broadcasted_iota(jnp.int32, (cc, cols), 1)
    spread = jnp.where(lax.shift_right_logical(dst, shift) == src, 1.0, 0.0).astype(BF16)
    x = _dot(t, spread)
    row = lax.broadcasted_iota(jnp.int32, x.shape, 0)
    col = lax.broadcasted_iota(jnp.int32, x.shape, 1)
    o_ref[0] = jnp.where((col & low) == (row & low), x, 0.0).astype(o_ref.dtype)


def _expand_block_diag(t):
    g, r, c_in = t.shape
    c = -(-c_in // 16) * 16
    if c != c_in:
        return _expand_block_diag(jnp.pad(t, ((0, 0), (0, 0), (0, c - c_in))))[:, :, :c_in * SUB]
    return pl.pallas_call(
        _expand_kernel,
        out_shape=jax.ShapeDtypeStruct((g, r, c * SUB), BF16),
        grid=(g,),
        in_specs=[pl.BlockSpec((1, r, c), lambda i: (i, 0, 0))],
        out_specs=pl.BlockSpec((1, r, c * SUB), lambda i: (i, 0, 0)),
        compiler_params=_params("parallel"),
        name="dft_table_expand",
    )(t)


def _fft_tables(L):
    n, na, nb, kp = _fft_dims(L)
    ha, ng = na // 2, nb // SUB
    ar = lambda m: jnp.arange(m, dtype=jnp.int32)
    g_, ka_, bl_, a_ = ar(ng)[:, None, None, None], ar(kp)[None, :, None, None], \
        ar(SUB)[None, None, :, None], ar(ha)[None, None, None, :]
    c, s = _cis(ka_ * (nb * a_ + SUB * g_ + bl_), n)
    cs = jnp.stack([c, -s], axis=2)
    m8 = _expand_block_diag(cs.reshape(ng, kp * 2 * SUB, ha))
    c, s = _cis(ar(nb)[:, None] * ar(nb)[None, :], nb)
    f2 = jnp.stack([jnp.stack([c, s], axis=1), jnp.stack([-s, c], axis=1)], axis=0)
    f2 = f2.reshape(2, nb, 2, ng, SUB).transpose(0, 1, 3, 2, 4).reshape(2 * nb, 2 * nb)
    ca, sa = _cis(ar(kp)[:, None] * ar(nb)[None, :], n)
    cb, sb = _cis(ar(nb)[:, None] * ar(nb)[None, :], nb)
    ca, sa = (t.reshape(kp, ng, 1, SUB, 1, 1) / n for t in (ca, sa))
    cb, sb = (t.reshape(1, ng, 1, SUB, 1, nb) for t in (cb, sb))
    c, s = ca * cb - sa * sb, sa * cb + ca * sb
    gi = jnp.concatenate([jnp.concatenate([c, -s], axis=4),
                          jnp.concatenate([s, c], axis=4)], axis=2)
    gi = gi.reshape(kp, 2 * nb, 2 * nb)
    kc = ar(kp)[None, :]
    wgt = jnp.where((kc == 0) | (kc == ha), 1.0, jnp.where(kc < ha, 2.0, 0.0))
    c, s = _cis(ar(ha)[:, None] * kc, na)
    pm = jnp.stack([wgt * c, -wgt * s], axis=2).reshape(ha, 1, kp * 2)
    p8 = _expand_block_diag(jnp.broadcast_to(pm, (ha, SUB, kp * 2)).reshape(1, ha * SUB, kp * 2))[0]
    return m8, f2.astype(BF16), gi.astype(BF16), p8


def _short_conv_kernel(u_ref, w_ref, b_ref, o_ref):
    u = u_ref[...]
    L = u.shape[0]
    row = lax.broadcasted_iota(jnp.int32, u.shape, 0)
    prev = jnp.where(row == 0, 0.0, pltpu.roll(u, 1, 0))
    nxt = jnp.where(row == L - 1, 0.0, pltpu.roll(u, L - 1, 0))
    w = w_ref[...]
    o_ref[...] = prev * w[0:1] + u * w[1:2] + nxt * w[2:3] + b_ref[...]


def short_conv(u, w, b):
    L, c = u.shape[0], w.shape[1]
    w8 = jnp.zeros((8, c), F32).at[:SHORT_CONV].set(w)
    return pl.pallas_call(
        _short_conv_kernel,
        out_shape=jax.ShapeDtypeStruct((L, c), F32),
        grid=(c // CT_COL,),
        in_specs=[pl.BlockSpec((L, CT_COL), lambda j: (0, j)),
                  pl.BlockSpec((8, CT_COL), lambda j: (0, j)),
                  pl.BlockSpec((1, CT_COL), lambda j: (0, j))],
        out_specs=pl.BlockSpec((L, CT_COL), lambda j: (0, j)),
        compiler_params=_params("parallel"),
        name="short_conv",
    )(u, w8, b.reshape(1, c))


def _fft_s1_kernel(u_ref, m8_ref, a_ref):
    ha, sub, ct = u_ref.shape
    x = u_ref[...].reshape(ha * sub, ct).astype(BF16)
    r = _dot(m8_ref[0], x)
    a_ref[:, 0, :, :] = r.astype(a_ref.dtype).reshape(a_ref.shape[0], 2 * sub, ct)


def fft_stage1(u, ncols, m8, L):
    n, na, nb, kp = _fft_dims(L)
    ha, ng = na // 2, nb // SUB
    u3 = u.reshape(ha, nb, u.shape[1])
    return pl.pallas_call(
        _fft_s1_kernel,
        out_shape=jax.ShapeDtypeStruct((kp, ng, 2 * SUB, ncols), BF16),
        grid=(ng, ncols // CT_G),
        in_specs=[pl.BlockSpec((ha, SUB, CT_G), lambda g, j: (0, g, j)),
                  pl.BlockSpec((1, kp * 2 * SUB, ha * SUB), lambda g, j: (g, 0, 0))],
        out_specs=pl.BlockSpec((kp, 1, 2 * SUB, CT_G), lambda g, j: (0, g, 0, j)),
        compiler_params=_params("parallel", "parallel"),
        name="fft_stage1",
    )(u3, m8)


def _filter_hidden_kernel(zt_ref, w1_ref, b1_ref, f1_ref, w2_ref, b2_ref, f2_ref, w3_ref, b3_ref,
                          f3_ref, o_ref):
    h = jnp.sin(f1_ref[...] * (_dot3(w1_ref[...], zt_ref[...]) + b1_ref[...]))
    h = jnp.sin(f2_ref[...] * (_dot3(w2_ref[...], h) + b2_ref[...]))
    h = jnp.sin(f3_ref[...] * (_dot3(w3_ref[...], h) + b3_ref[...]))
    o_ref[...] = h.T


def filter_hidden(L, w1, b1, f1, w2, b2, f2, w3, b3, f3, tt=1024):
    hdim = FILTER_HIDDEN
    tt = min(tt, L)
    t = jnp.linspace(0.0, 1.0, L, dtype=F32)[None, :]
    omega = 2.0 * math.pi * jnp.arange(L, dtype=F32)[None, :] / L
    bands = jnp.linspace(1e-4, POS_BANDS - 1, POS_BANDS, dtype=F32)[:, None]
    zt = jnp.concatenate([t, jnp.cos(bands * omega), -jnp.sin(bands * omega)], axis=0)
    zt = jnp.zeros((LANES, L), F32).at[:POS_EMB_DIM].set(zt)
    w1t = jnp.zeros((hdim, LANES), F32).at[:, :POS_EMB_DIM].set(w1.T)
    col = lambda v: v.reshape(hdim, 1)
    full = lambda shape: pl.BlockSpec(shape, lambda i: (0, 0))
    return pl.pallas_call(
        _filter_hidden_kernel,
        out_shape=jax.ShapeDtypeStruct((L, hdim), F32),
        grid=(L // tt,),
        in_specs=[pl.BlockSpec((LANES, tt), lambda i: (0, i)),
                  full((hdim, LANES)), full((hdim, 1)), full((hdim, 1)),
                  full((hdim, hdim)), full((hdim, 1)), full((hdim, 1)),
                  full((hdim, hdim)), full((hdim, 1)), full((hdim, 1))],
        out_specs=pl.BlockSpec((tt, hdim), lambda i: (i, 0)),
        compiler_params=_params("parallel"),
        name="hyena_filter_hidden",
    )(zt, w1t, col(b1), col(f1), w2.T, col(b2), col(f2), w3.T, col(b3), col(f3))


def _filter_s1_kernel(h_ref, w4_ref, dl_ref, m8_ref, a_ref, nrm_ref, *, nb, L, cb):
    g, j = pl.program_id(0), pl.program_id(1)
    ha, sub, hdim = h_ref.shape
    rows = ha * sub
    wh = w4_ref[...].astype(BF16)
    ct = wh.shape[1]
    k = _dot(h_ref[...].reshape(rows, hdim).astype(BF16), wh).reshape(ha, sub, ct)
    a3 = lax.broadcasted_iota(jnp.int32, (ha, sub, ct), 0)
    b3 = lax.broadcasted_iota(jnp.int32, (ha, sub, ct), 1)
    tpos = a3 * nb + (b3 + g * sub)
    k = k * jnp.exp(-(tpos.astype(F32) * (1.0 / (L - 1))) * dl_ref[...])
    k = jnp.where(jnp.logical_and((j // cb) % 2 == 1, tpos == 0), 0.0, k).reshape(rows, ct)
    nrm_ref[0] = jnp.sum(jnp.abs(k), axis=0, keepdims=True)
    q = _dot(m8_ref[0], k.astype(BF16))
    a_ref[:, 0, :, :] = q.astype(a_ref.dtype).reshape(a_ref.shape[0], 2 * sub, ct)


def filter_stage1(L, w1, b1, f1, w2, b2, f2, w3, b3, f3, w4, m8):
    n, na, nb, kp = _fft_dims(L)
    ha, ng = na // 2, nb // SUB
    c = HYENA_WIDTH
    hdim = FILTER_HIDDEN
    hid = filter_hidden(L, w1, b1, f1, w2, b2, f2, w3, b3, f3).reshape(ha, nb, hdim)
    deltas = jnp.abs(jnp.linspace(MIN_DECAY, MAX_DECAY, c, dtype=F32)).reshape(1, c)
    nc = w4.shape[1]
    cb = c // CT_G
    kern = functools.partial(_filter_s1_kernel, nb=nb, L=L, cb=cb)
    return pl.pallas_call(
        kern,
        out_shape=(jax.ShapeDtypeStruct((kp, ng, 2 * SUB, nc), BF16),
                   jax.ShapeDtypeStruct((ng, 1, nc), F32)),
        grid=(ng, nc // CT_G),
        in_specs=[pl.BlockSpec((ha, SUB, hdim), lambda g, j: (0, g, 0)),
                  pl.BlockSpec((hdim, CT_G), lambda g, j: (0, j)),
                  pl.BlockSpec((1, CT_G), lambda g, j: (0, j % cb)),
                  pl.BlockSpec((1, kp * 2 * SUB, ha * SUB), lambda g, j: (g, 0, 0))],
        out_specs=(pl.BlockSpec((kp, 1, 2 * SUB, CT_G), lambda g, j: (0, g, 0, j)),
                   pl.BlockSpec((1, 1, CT_G), lambda g, j: (g, 0, j))),
        compiler_params=_params("parallel", "parallel"),
        name="hyena_filter_stage1",
    )(hid, w4, deltas, m8)


def _slab(ref, s):
    _, ng, rows, ct = ref.shape
    return ref[s].reshape(ng * rows, ct)


def _filter_spec_kernel(af_ref, ab_ref, f2_ref, nf_ref, nbk_ref, h_ref):
    f2 = f2_ref[...]
    nb = f2.shape[0] // 2
    inv = 1.0 / (jnp.sum(nf_ref[...], axis=0) + jnp.sum(nbk_ref[...], axis=0))
    for s in range(af_ref.shape[0]):
        xf = _dot(f2, _slab(af_ref, s))
        xb = _dot(f2, _slab(ab_ref, s))
        h_ref[0, s] = ((xf[:nb] + xb[:nb]) * inv).astype(h_ref.dtype)
        h_ref[1, s] = ((xf[nb:] - xb[nb:]) * inv).astype(h_ref.dtype)


def filter_spectrum(a, nrm, f2, L):
    n, na, nb, kp = _fft_dims(L)
    ng = nb // SUB
    c = HYENA_WIDTH
    cb = c // CT
    ms = _slabs_per_step(kp)
    fcol = lambda j: (j // cb) * 2 * cb + j % cb
    return pl.pallas_call(
        _filter_spec_kernel,
        out_shape=jax.ShapeDtypeStruct((2, kp, nb, HYENA_ORDER * c), BF16),
        grid=(HYENA_ORDER * cb, kp // ms),
        in_specs=[pl.BlockSpec((ms, ng, 2 * SUB, CT), lambda j, k: (k, 0, 0, fcol(j))),
                  pl.BlockSpec((ms, ng, 2 * SUB, CT), lambda j, k: (k, 0, 0, fcol(j) + cb)),
                  pl.BlockSpec((2 * nb, 2 * nb), lambda j, k: (0, 0)),
                  pl.BlockSpec((ng, 1, CT), lambda j, k: (0, 0, fcol(j))),
                  pl.BlockSpec((ng, 1, CT), lambda j, k: (0, 0, fcol(j) + cb))],
        out_specs=pl.BlockSpec((2, ms, nb, CT), lambda j, k: (0, k, 0, j)),
        compiler_params=_params("parallel", "parallel"),
        name="hyena_filter_spectrum",
    )(a, a, f2, nrm, nrm)


def _fft_mid_kernel(a_ref, h_ref, f2_ref, g_ref, q_ref):
    f2 = f2_ref[...]
    nb = f2.shape[0] // 2
    slabs, ng, rows, ct = q_ref.shape
    for s in range(slabs):
        x = _dot(f2, _slab(a_ref, s))
        xr, xi = x[:nb], x[nb:]
        hr, hi = h_ref[0, s].astype(F32), h_ref[1, s].astype(F32)
        y = jnp.concatenate([xr * hr - xi * hi, xr * hi + xi * hr], axis=0).astype(BF16)
        q_ref[s] = _dot(g_ref[s], y).astype(q_ref.dtype).reshape(ng, rows, ct)


def fft_mid(a, hf, order, f2, g, L):
    n, na, nb, kp = _fft_dims(L)
    ng = nb // SUB
    c = a.shape[3]
    cb = c // CT
    ms = _slabs_per_step(kp)
    return pl.pallas_call(
        _fft_mid_kernel,
        out_shape=jax.ShapeDtypeStruct(a.shape, BF16),
        grid=(kp // ms, cb),
        in_specs=[pl.BlockSpec((ms, ng, 2 * SUB, CT), lambda k, j: (k, 0, 0, j)),
                  pl.BlockSpec((2, ms, nb, CT), lambda k, j: (0, k, 0, order * cb + j)),
                  pl.BlockSpec((2 * nb, 2 * nb), lambda k, j: (0, 0)),
                  pl.BlockSpec((ms, 2 * nb, 2 * nb), lambda k, j: (k, 0, 0))],
        out_specs=pl.BlockSpec((ms, ng, 2 * SUB, CT), lambda k, j: (k, 0, 0, j)),
        compiler_params=_params("parallel", "parallel"),
        name="fft_mid",
    )(a, hf, f2, g)


def _fft_i2_kernel(q_ref, p8_ref, u_ref, x_ref, s_ref, o_ref):
    kp, _, rows, ct = q_ref.shape
    q = q_ref[:, 0, :, :].reshape(kp * rows, ct)
    y = _dot(p8_ref[...], q).reshape(o_ref.shape)
    o_ref[...] = x_ref[...] * (y + u_ref[...] * s_ref[...])


def fft_stage_out(q, u, gate_src, gate_blk, skip, p8, L):
    n, na, nb, kp = _fft_dims(L)
    ha, ng = na // 2, nb // SUB
    c = q.shape[3]
    u3 = u.reshape(ha, nb, u.shape[1])
    g3 = gate_src.reshape(ha, nb, gate_src.shape[1])
    out = pl.pallas_call(
        _fft_i2_kernel,
        out_shape=jax.ShapeDtypeStruct((ha, nb, c), F32),
        grid=(ng, c // CT_G),
        in_specs=[pl.BlockSpec((kp, 1, 2 * SUB, CT_G), lambda g, j: (0, g, 0, j)),
                  pl.BlockSpec((ha * SUB, kp * 2 * SUB), lambda g, j: (0, 0)),
                  pl.BlockSpec((ha, SUB, CT_G), lambda g, j: (0, g, j)),
                  pl.BlockSpec((ha, SUB, CT_G), lambda g, j: (0, g, gate_blk + j)),
                  pl.BlockSpec((1, 1, CT_G), lambda g, j: (0, 0, j))],
        out_specs=pl.BlockSpec((ha, SUB, CT_G), lambda g, j: (0, g, j)),
        compiler_params=_params("parallel", "parallel"),
        name="fft_stage_out",
    )(q, p8, u3, g3, skip.reshape(1, 1, c))
    return out.reshape(L, c)


def hyena_mixer(u_hy, conv_w, conv_b, filt, skip, tables):
    L = u_hy.shape[0]
    c = HYENA_WIDTH
    m8, f2, g, p8 = tables
    uc = short_conv(u_hy, conv_w, conv_b)
    a_filt, nrm = filter_stage1(L, *filt, m8)
    hf = filter_spectrum(a_filt, nrm, f2, L)
    z = uc
    for order in range(HYENA_ORDER):
        a = fft_stage1(z, c, m8, L)
        q = fft_mid(a, hf, order, f2, g, L)
        z = fft_stage_out(q, z, uc, (order + 1) * (c // CT_G), skip[order], p8, L)
    return z


MOE_TM = 512
MOE_TN = 512
COMBINE_TM = 256


def _moe_plan(route, counts, L):
    tm = MOE_TM
    nt = -(-(2 * L + N_EXPERTS * (tm - 1)) // tm)
    e_flat = jnp.concatenate([route[:, ROUTE_E1], route[:, ROUTE_E2]]).astype(jnp.int32)
    rank = jnp.concatenate([route[:, ROUTE_R1], route[:, ROUTE_R2]]).astype(jnp.int32)
    counts = counts[0, :N_EXPERTS].astype(jnp.int32)
    padded = ((counts + tm - 1) // tm) * tm
    ends = jnp.cumsum(padded)
    dest = (ends - padded)[e_flat] + rank
    tok = jnp.tile(jnp.arange(L, dtype=jnp.int32), 2)
    row_token = jnp.zeros((nt * tm,), jnp.int32).at[dest].set(tok, unique_indices=True)
    n_used = ends[-1] // tm
    tile_row = jnp.arange(nt, dtype=jnp.int32) * tm
    tile_expert = jnp.sum((tile_row[:, None] >= ends[None, :]).astype(jnp.int32), axis=1)
    last = jnp.take(tile_expert, jnp.maximum(n_used - 1, 0))
    tile_expert = jnp.where(jnp.arange(nt) < n_used, tile_expert, last)
    tile_first = jnp.concatenate([jnp.ones((1,), jnp.int32),
                                  (tile_expert[1:] != tile_expert[:-1]).astype(jnp.int32)])
    tile_group = jnp.cumsum(tile_first) - 1
    group_expert = jnp.zeros((N_EXPERTS,), jnp.int32).at[tile_group].set(tile_expert)
    n_groups = jnp.take(tile_group, jnp.maximum(n_used - 1, 0)) + 1
    i32 = lambda v: v.reshape(1).astype(jnp.int32)
    last_row = (ends - padded + counts)[tile_expert]
    tile_valid = jnp.where(jnp.arange(nt) < n_used, jnp.clip(last_row - tile_row, 0, tm), 0)
    tile_chunks = ((tile_valid + GATHER_CHUNK - 1) // GATHER_CHUNK).astype(jnp.int32)
    return dict(nt=nt, row_token=row_token, dest=dest, tile_chunks=tile_chunks,
                scalars=(tile_expert, tile_first, i32(n_used), tile_group.astype(jnp.int32),
                         i32(n_groups), group_expert))


def _row_copy(src_hbm, idx, buf, slot, r, sem):
    return pltpu.make_async_copy(src_hbm.at[pl.ds(idx, 1)], buf.at[slot, pl.ds(r, 1)],
                                 sem.at[slot])


def _gather_rows(idx_ref, src_hbm, buf, slot, sem, n_rows, start):
    if not start:
        pltpu.make_async_copy(src_hbm.at[pl.ds(0, n_rows)], buf.at[slot], sem.at[slot]).wait()
        return

    def body(r, carry):
        _row_copy(src_hbm, idx_ref[0, 0, r], buf, slot, r, sem).start(priority=1)
        return carry
    lax.fori_loop(0, n_rows, body, 0, unroll=8)


def _pipelined_gather(cur_ref, nxt_ref, src_hbm, buf, sem, n_rows):
    i, n = pl.program_id(0), pl.num_programs(0)
    slot = lax.rem(i, 2)

    @pl.when(i == 0)
    def _():
        _gather_rows(cur_ref, src_hbm, buf, 0, sem, n_rows, True)

    @pl.when(i + 1 < n)
    def _():
        _gather_rows(nxt_ref, src_hbm, buf, 1 - slot, sem, n_rows, True)

    _gather_rows(cur_ref, src_hbm, buf, slot, sem, n_rows, False)
    return slot


GATHER_CHUNK = 32


def _dispatch_rows(idx_ref, h_hbm, buf, slot, sem, n_chunks, start):
    def body(c, carry):
        base = pl.multiple_of(c * GATHER_CHUNK, GATHER_CHUNK)
        if start:
            for r in range(GATHER_CHUNK):
                _row_copy(h_hbm, idx_ref[0, 0, base + r], buf, slot, base + r, sem).start(priority=1)
        else:
            pltpu.make_async_copy(h_hbm.at[pl.ds(0, GATHER_CHUNK)],
                                  buf.at[slot, pl.ds(base, GATHER_CHUNK)], sem.at[slot]).wait()
        return carry
    lax.fori_loop(0, n_chunks, body, 0)


def _dispatch_kernel(nch_ref, cur_ref, nxt_ref, h_hbm, o_ref, buf, sem):
    i, n = pl.program_id(0), pl.num_programs(0)
    slot = lax.rem(i, 2)

    @pl.when(i == 0)
    def _():
        buf[...] = jnp.zeros_like(buf)
        _dispatch_rows(cur_ref, h_hbm, buf, 0, sem, nch_ref[0], True)

    @pl.when(i + 1 < n)
    def _():
        _dispatch_rows(nxt_ref, h_hbm, buf, 1 - slot, sem, nch_ref[jnp.minimum(i + 1, n - 1)], True)

    _dispatch_rows(cur_ref, h_hbm, buf, slot, sem, nch_ref[i], False)
    o_ref[...] = buf[slot].astype(o_ref.dtype)


def moe_dispatch(h, row_token, tile_chunks, nt):
    L, d = h.shape
    idx = row_token.reshape(nt, 1, MOE_TM)
    smem = lambda f: pl.BlockSpec((1, 1, MOE_TM), f, memory_space=pltpu.SMEM)
    return pl.pallas_call(
        _dispatch_kernel,
        out_shape=jax.ShapeDtypeStruct((nt * MOE_TM, d), BF16),
        grid_spec=pltpu.PrefetchScalarGridSpec(
            num_scalar_prefetch=1,
            grid=(nt,),
            in_specs=[smem(lambda i, nch: (i, 0, 0)),
                      smem(lambda i, nch: (jnp.minimum(i + 1, nt - 1), 0, 0)),
                      pl.BlockSpec(memory_space=pl.ANY)],
            out_specs=pl.BlockSpec((MOE_TM, d), lambda i, nch: (i, 0)),
            scratch_shapes=[pltpu.VMEM((2, MOE_TM, d), F32), pltpu.SemaphoreType.DMA((2,))]),
        compiler_params=_params("arbitrary"),
        name="moe_dispatch",
    )(tile_chunks, idx, idx, h)


def _weight_copy(w_hbm, e, jcol, wbuf, k, slot, sem):
    tn = wbuf.shape[3]
    return pltpu.make_async_copy(w_hbm.at[e, :, pl.ds(pl.multiple_of(jcol * tn, tn), tn)],
                                 wbuf.at[k, slot], sem.at[k, slot])


def _refresh_expert_weights(scalars, w_hbms, wbuf, wbs, sem):
    te_ref, tf_ref, _, tg_ref, ng_ref, ge_ref = scalars
    j, i = pl.program_id(0), pl.program_id(1)
    nj = pl.num_programs(0)

    @pl.when(tf_ref[i] == 1)
    def _():
        g, ng = tg_ref[i], ng_ref[0]
        blk = j * ng + g
        slot = lax.rem(blk, 2)

        @pl.when(blk == 0)
        def _():
            for k, w in enumerate(w_hbms):
                _weight_copy(w, te_ref[i], j, wbuf, k, 0, sem).start()

        for k, w in enumerate(w_hbms):
            _weight_copy(w, 0, 0, wbuf, k, slot, sem).wait()
        for k, wb in enumerate(wbs):
            wb[...] = wbuf[k, slot].astype(BF16)

        wrap = g + 1 == ng
        nxt_e = ge_ref[jnp.where(wrap, 0, g + 1)]
        nxt_j = j + wrap.astype(jnp.int32)

        @pl.when(nxt_j < nj)
        def _():
            for k, w in enumerate(w_hbms):
                _weight_copy(w, nxt_e, nxt_j, wbuf, k, 1 - slot, sem).start()


def _moe_up_kernel(te_ref, tf_ref, nu_ref, tg_ref, ng_ref, ge_ref, a_ref, wg_hbm, wu_hbm, o_ref,
                   wbuf, wgb, wub, sem):
    i = pl.program_id(1)
    _refresh_expert_weights((te_ref, tf_ref, nu_ref, tg_ref, ng_ref, ge_ref),
                            (wg_hbm, wu_hbm), wbuf, (wgb, wub), sem)

    @pl.when(i < nu_ref[0])
    def _():
        a = a_ref[...]
        o_ref[...] = (_silu(_dot(a, wgb[...])) * _dot(a, wub[...])).astype(o_ref.dtype)

    @pl.when(i >= nu_ref[0])
    def _():
        o_ref[...] = jnp.zeros_like(o_ref)


def moe_up(hs, w_gate, w_up, plan):
    r, d = hs.shape
    f = w_gate.shape[2]
    nt = plan["nt"]
    anyspec = pl.BlockSpec(memory_space=pl.ANY)
    return pl.pallas_call(
        _moe_up_kernel,
        out_shape=jax.ShapeDtypeStruct((r, f), BF16),
        grid_spec=pltpu.PrefetchScalarGridSpec(
            num_scalar_prefetch=6,
            grid=(f // MOE_TN, nt),
            in_specs=[pl.BlockSpec((MOE_TM, d), lambda j, i, *_: (i, 0)), anyspec, anyspec],
            out_specs=pl.BlockSpec((MOE_TM, MOE_TN), lambda j, i, *_: (i, j)),
            scratch_shapes=[pltpu.VMEM((2, 2, d, MOE_TN), F32),
                            pltpu.VMEM((d, MOE_TN), BF16), pltpu.VMEM((d, MOE_TN), BF16),
                            pltpu.SemaphoreType.DMA((2, 2))]),
        compiler_params=_params("arbitrary", "arbitrary"),
        name="moe_up",
    )(*plan["scalars"], hs, w_gate, w_up)


def _moe_down_kernel(te_ref, tf_ref, nu_ref, tg_ref, ng_ref, ge_ref, t_ref, wd_hbm, o_ref,
                     wbuf, wdb, sem):
    i = pl.program_id(1)
    _refresh_expert_weights((te_ref, tf_ref, nu_ref, tg_ref, ng_ref, ge_ref),
                            (wd_hbm,), wbuf, (wdb,), sem)

    @pl.when(i < nu_ref[0])
    def _():
        o_ref[...] = _dot(t_ref[...], wdb[...])

    @pl.when(i >= nu_ref[0])
    def _():
        o_ref[...] = jnp.zeros_like(o_ref)


def moe_down(t, w_down, plan):
    r, f = t.shape
    d = w_down.shape[2]
    nt = plan["nt"]
    return pl.pallas_call(
        _moe_down_kernel,
        out_shape=jax.ShapeDtypeStruct((r, d), F32),
        grid_spec=pltpu.PrefetchScalarGridSpec(
            num_scalar_prefetch=6,
            grid=(d // MOE_TN, nt),
            in_specs=[pl.BlockSpec((MOE_TM, f), lambda j, i, *_: (i, 0)),
                      pl.BlockSpec(memory_space=pl.ANY)],
            out_specs=pl.BlockSpec((MOE_TM, MOE_TN), lambda j, i, *_: (i, j)),
            scratch_shapes=[pltpu.VMEM((1, 2, f, MOE_TN), F32), pltpu.VMEM((f, MOE_TN), BF16),
                            pltpu.SemaphoreType.DMA((1, 2))]),
        compiler_params=_params("arbitrary", "arbitrary"),
        name="moe_down",
    )(*plan["scalars"], t, w_down)


def _combine_kernel(cur_ref, nxt_ref, y_hbm, x_ref, route_ref, g_ref, o_ref, buf, sem, *, final):
    tm = x_ref.shape[0]
    slot = _pipelined_gather(cur_ref, nxt_ref, y_hbm, buf, sem, 2 * tm)
    route = route_ref[...]
    g1 = route[:, ROUTE_G1:ROUTE_G1 + 1]
    g2 = route[:, ROUTE_G2:ROUTE_G2 + 1]
    x = x_ref[...] + g1 * buf[slot, :tm, :] + g2 * buf[slot, tm:, :]
    if final:
        ms = jnp.mean(x * x, axis=-1, keepdims=True)
        x = x * lax.rsqrt(ms + RMS_EPS) * g_ref[...]
    o_ref[...] = x


def moe_combine(x, y, dest, route, final_gain):
    L, d = x.shape
    tm = COMBINE_TM
    nt = L // tm
    idx = jnp.concatenate([dest[:L].reshape(nt, 1, tm), dest[L:].reshape(nt, 1, tm)], axis=2)
    smem = lambda f: pl.BlockSpec((1, 1, 2 * tm), f, memory_space=pltpu.SMEM)
    final = final_gain is not None
    gain = (final_gain if final else jnp.ones((d,), F32)).reshape(1, d)
    kern = functools.partial(_combine_kernel, final=final)
    return pl.pallas_call(
        kern,
        out_shape=jax.ShapeDtypeStruct((L, d), F32),
        grid=(nt,),
        in_specs=[smem(lambda i: (i, 0, 0)),
                  smem(lambda i: (jnp.minimum(i + 1, nt - 1), 0, 0)),
                  pl.BlockSpec(memory_space=pl.ANY),
                  pl.BlockSpec((tm, d), lambda i: (i, 0)),
                  pl.BlockSpec((tm, LANES), lambda i: (i, 0)),
                  pl.BlockSpec((1, d), lambda i: (0, 0))],
        out_specs=pl.BlockSpec((tm, d), lambda i: (i, 0)),
        scratch_shapes=[pltpu.VMEM((2, 2 * tm, d), F32), pltpu.SemaphoreType.DMA((2,))],
        compiler_params=_params("arbitrary"),
        name="moe_combine",
    )(idx, idx, y, x, route, gain)


def moe_ffn(x, norm_g, router_w, w_gate, w_up, w_down, final_gain):
    L = x.shape[0]
    h, route, counts = rmsnorm_router(x, norm_g, router_w)
    plan = _moe_plan(route, counts, L)
    hs = moe_dispatch(h, plan["row_token"], plan["tile_chunks"], plan["nt"])
    t = moe_up(hs, w_gate, w_up, plan)
    y = moe_down(t, w_down, plan)
    return moe_combine(x, y, plan["dest"], route, final_gain)


def _layer_mixers(x, l, tables, norm_mix, w_in_t, conv_w, conv_b, filt, hyena_skip, w_branch_a,
                  gla_gate_w2, gla_gate_b, gla_norm, w_branch_b, w_out):
    hc = (HYENA_ORDER + 1) * HYENA_WIDTH
    o_lr = hc + 2 * GLA_DK_TOTAL + 2 * GLA_DV_TOTAL
    o_g = o_lr + 2 * GLA_LOWRANK
    h = rmsnorm(x, norm_mix[l], BF16)
    proj = matmul([h], [(w_in_t, l, 0)], o_lr, [(0, 0)], _ep_plain, [], F32,
                  tm=1024, tn=1024, name="in_proj", w_transposed=True)
    gates = gate_projection(h, w_in_t, l, o_g, 2 * D_MODEL)
    la = gla_gates(h, w_in_t[l, o_lr:o_g], gla_gate_w2[l], gla_gate_b[l])

    z_a = hyena_mixer(proj, conv_w[l], conv_b[l], filt, hyena_skip[l], tables)
    o_f, o_b = gla_scan(proj, hc, la)
    z_b = gla_post(o_f, o_b, proj, hc + 2 * GLA_DK_TOTAL + GLA_DV_TOTAL, gla_norm[l])

    mixed = matmul([z_a, z_b], [(w_branch_a, l, 0), (w_branch_b, l, 0)], D_MODEL,
                   [(0, 0), (1, 1)], _ep_merge, [(gates, 0), (gates, D_MODEL)], BF16,
                   tm=1024, tn=1024, name="branch_merge")
    return matmul([mixed], [(w_out, l, 0)], D_MODEL, [(0, 0)], _ep_residual, [(x, 0)], F32,
                  tm=1024, tn=512, name="out_proj")


def _swiglu_ffn(x, h, wg, wu, wd, i):
    f = wg.shape[2]
    t = matmul([h], [(wg, i, 0), (wu, i, 0)], f, [(0, 0), (0, 1)], _ep_swiglu, [], BF16,
               tm=1024, tn=512, name="ffn_up")
    return matmul([t], [(wd, i, 0)], wd.shape[2], [(0, 0)], _ep_residual, [(x, 0)], F32,
                  tm=512, tn=512, name="ffn_down")


def kernel(x, norm_mix, w_in, conv_w, conv_b, filt_w1, filt_b1, filt_freq1, filt_w2, filt_b2, filt_freq2, filt_w3, filt_b3, filt_freq3, filt_w4, hyena_skip, w_branch_a, gla_gate_w2, gla_gate_b, gla_norm, w_branch_b, w_out, norm_ffn, dense_w_gate, dense_w_up, dense_w_down, router_w, moe_w_gate, moe_w_up, moe_w_down, norm_final):
    b, L, d = x.shape
    assert b == 1
    x = x.reshape(L, d)
    tables = _fft_tables(L)
    w_in_t = jnp.swapaxes(w_in, 1, 2)
    for l in range(DEPTH):
        filt = (filt_w1[l], filt_b1[l], filt_freq1[l], filt_w2[l], filt_b2[l], filt_freq2[l],
                filt_w3[l], filt_b3[l], filt_freq3[l], filt_w4[l])
        x = _layer_mixers(x, l, tables, norm_mix, w_in_t, conv_w, conv_b, filt, hyena_skip,
                          w_branch_a, gla_gate_w2, gla_gate_b, gla_norm, w_branch_b, w_out)
        i = l // 2
        last = l == DEPTH - 1
        if l % 2 == 0:
            h = rmsnorm(x, norm_ffn[l], BF16)
            x = _swiglu_ffn(x, h, dense_w_gate, dense_w_up, dense_w_down, i)
            if last:
                x = rmsnorm(x, norm_final, F32)
        else:
            x = moe_ffn(x, norm_ffn[l], router_w[i], moe_w_gate[i], moe_w_up[i], moe_w_down[i],
                        norm_final if last else None)
    return x.reshape(b, L, d)
```

```python
import functools
import math

import jax
import jax.numpy as jnp
from jax import lax
from jax.experimental import pallas as pl
from jax.experimental.pallas import tpu as pltpu

F32 = jnp.float32
BF16 = jnp.bfloat16

D_MODEL = 2048
DEPTH = 2
HYENA_WIDTH = D_MODEL // 2
HYENA_ORDER = 2
SHORT_CONV = 3
POS_EMB_DIM = 33
POS_BANDS = (POS_EMB_DIM - 1) // 2
FILTER_HIDDEN = 64
MIN_DECAY = math.log(1e-2) / 1.5
MAX_DECAY = math.log(1e-2) / 0.3
GLA_HEADS = 4
GLA_DK = 128
GLA_DV = 256
GLA_DK_TOTAL = GLA_HEADS * GLA_DK
GLA_DV_TOTAL = GLA_HEADS * GLA_DV
GLA_LOWRANK = 16
GLA_GATE_TAU = 16.0
GLA_CHUNK = 64
N_EXPERTS = 8
RMS_EPS = 1e-6

LANES = 128
VMEM_LIMIT_BYTES = 56 * 1024 * 1024


def _params(*sem):
    return pltpu.CompilerParams(dimension_semantics=sem, vmem_limit_bytes=VMEM_LIMIT_BYTES)


def _dot(a, b):
    return jnp.dot(a, b, preferred_element_type=F32)


def _split(a):
    hi = a.astype(BF16)
    lo = (a - hi.astype(F32)).astype(BF16)
    return hi, lo


def _dot3(a, b):
    ah, al = _split(a)
    bh, bl = _split(b)
    return _dot(ah, bh) + (_dot(ah, bl) + _dot(al, bh))


def _sigmoid(x):
    return 1.0 / (1.0 + jnp.exp(-x))


def _silu(x):
    return x * _sigmoid(x)


def _log_sigmoid(x):
    return jnp.minimum(x, 0.0) - jnp.log(1.0 + jnp.exp(-jnp.abs(x)))


def _rmsnorm_kernel(x_ref, g_ref, o_ref):
    x = x_ref[...]
    ms = jnp.mean(x * x, axis=-1, keepdims=True)
    o_ref[...] = (x * lax.rsqrt(ms + RMS_EPS) * g_ref[...]).astype(o_ref.dtype)


def rmsnorm(x, g, out_dtype, tm=1024):
    m, d = x.shape
    return pl.pallas_call(
        _rmsnorm_kernel,
        out_shape=jax.ShapeDtypeStruct((m, d), out_dtype),
        grid=(m // tm,),
        in_specs=[pl.BlockSpec((tm, d), lambda i: (i, 0)),
                  pl.BlockSpec((1, d), lambda i: (0, 0))],
        out_specs=pl.BlockSpec((tm, d), lambda i: (i, 0)),
        compiler_params=_params("parallel"),
        name="rmsnorm",
    )(x, g.reshape(1, d))


def _rmsnorm_router_kernel(x_ref, g_ref, rw_ref, o_ref, cw_ref, cnt_ref):
    @pl.when(pl.program_id(0) == 0)
    def _():
        cnt_ref[...] = jnp.zeros_like(cnt_ref)

    x = x_ref[...]
    ms = jnp.mean(x * x, axis=-1, keepdims=True)
    h = x * lax.rsqrt(ms + RMS_EPS) * g_ref[...]
    o_ref[...] = h.astype(o_ref.dtype)
    logits = _dot3(h, rw_ref[...])
    col = lax.broadcasted_iota(jnp.int32, logits.shape, 1).astype(F32)
    neg = jnp.float32(-jnp.inf)
    logits = jnp.where(col < N_EXPERTS, logits, neg)
    m1 = jnp.max(logits, axis=-1, keepdims=True)
    i1 = jnp.min(jnp.where(logits == m1, col, float(LANES)), axis=-1, keepdims=True)
    rest = jnp.where(col == i1, neg, logits)
    m2 = jnp.max(rest, axis=-1, keepdims=True)
    i2 = jnp.min(jnp.where(rest == m2, col, float(LANES)), axis=-1, keepdims=True)
    e2 = jnp.exp(m2 - m1)
    g1 = 1.0 / (1.0 + e2)
    g2 = e2 / (1.0 + e2)
    hit1, hit2 = col == i1, col == i2
    c = jnp.where(hit1, 1.0, 0.0) + jnp.where(hit2, 1.0, 0.0)
    tm = c.shape[0]
    rows = lax.broadcasted_iota(jnp.int32, c.shape, 0)
    incl, step = c, 1
    while step < tm:
        incl = incl + jnp.where(rows >= step, pltpu.roll(incl, step, 0), 0.0)
        step *= 2
    before = incl - c + cnt_ref[...]
    r1 = jnp.sum(jnp.where(hit1, before, 0.0), axis=-1, keepdims=True)
    r2 = jnp.sum(jnp.where(hit2, before, 0.0), axis=-1, keepdims=True)
    cnt_ref[...] = cnt_ref[...] + incl[tm - 1:tm]
    cw_ref[...] = (jnp.where(col == ROUTE_E1, i1, 0.0) + jnp.where(col == ROUTE_E2, i2, 0.0)
                   + jnp.where(col == ROUTE_G1, g1, 0.0) + jnp.where(col == ROUTE_G2, g2, 0.0)
                   + jnp.where(col == ROUTE_R1, r1, 0.0) + jnp.where(col == ROUTE_R2, r2, 0.0))


ROUTE_E1, ROUTE_E2, ROUTE_G1, ROUTE_G2, ROUTE_R1, ROUTE_R2 = 0, 1, 2, 3, 4, 5


def rmsnorm_router(x, g, router_w, tm=256):
    m, d = x.shape
    rw = jnp.zeros((d, LANES), F32).at[:, :N_EXPERTS].set(router_w)
    return pl.pallas_call(
        _rmsnorm_router_kernel,
        out_shape=(jax.ShapeDtypeStruct((m, d), F32), jax.ShapeDtypeStruct((m, LANES), F32),
                   jax.ShapeDtypeStruct((1, LANES), F32)),
        grid=(m // tm,),
        in_specs=[pl.BlockSpec((tm, d), lambda i: (i, 0)),
                  pl.BlockSpec((1, d), lambda i: (0, 0)),
                  pl.BlockSpec((d, LANES), lambda i: (0, 0))],
        out_specs=(pl.BlockSpec((tm, d), lambda i: (i, 0)),
                   pl.BlockSpec((tm, LANES), lambda i: (i, 0)),
                   pl.BlockSpec((1, LANES), lambda i: (0, 0))),
        compiler_params=_params("arbitrary"),
        name="rmsnorm_router",
    )(x, g.reshape(1, d), rw)


def _mm_kernel(*refs, n_a, n_w, dots, n_extra, epilogue, w_transposed):
    a_refs = refs[:n_a]
    w_refs = refs[n_a:n_a + n_w]
    e_refs = refs[n_a + n_w:n_a + n_w + n_extra]
    o_ref = refs[n_a + n_w + n_extra]
    wb_refs = refs[n_a + n_w + n_extra + 1:]

    @pl.when(pl.program_id(1) == 0)
    def _():
        for w, wb in zip(w_refs, wb_refs):
            wb[...] = w[0].astype(BF16)

    a_vals = [a[...].astype(BF16) for a in a_refs]
    nt_dims = (((1,), (1,)), ((), ()))
    accs = [lax.dot_general(a_vals[ai], wb_refs[wi][...], nt_dims, preferred_element_type=F32)
            if w_transposed else _dot(a_vals[ai], wb_refs[wi][...]) for ai, wi in dots]
    o_ref[...] = epilogue(accs, [e[...] for e in e_refs]).astype(o_ref.dtype)


def matmul(a_list, w_list, n, dots, epilogue, extras, out_dtype, *, tm, tn, name,
           w_transposed=False):
    m = a_list[0].shape[0]
    assert m % tm == 0 and n % tn == 0
    in_specs = [pl.BlockSpec((tm, a.shape[1]), lambda j, i: (i, 0)) for a in a_list]
    w_arrays, scratch = [], []
    for w, s, first in w_list:
        assert first % tn == 0
        w_arrays.append(w)
        if w_transposed:
            in_specs.append(pl.BlockSpec((1, tn, w.shape[2]),
                                         lambda j, i, s=s, off=first // tn: (s, off + j, 0)))
            scratch.append(pltpu.VMEM((tn, w.shape[2]), BF16))
            continue
        in_specs.append(pl.BlockSpec((1, w.shape[1], tn),
                                     lambda j, i, s=s, off=first // tn: (s, 0, off + j)))
        scratch.append(pltpu.VMEM((w.shape[1], tn), BF16))
    extra_arrays = []
    for arr, first in extras:
        assert first % tn == 0
        extra_arrays.append(arr)
        in_specs.append(pl.BlockSpec((tm, tn), lambda j, i, off=first // tn: (i, off + j)))
    kern = functools.partial(_mm_kernel, n_a=len(a_list), n_w=len(w_list), dots=tuple(dots),
                             n_extra=len(extras), epilogue=epilogue, w_transposed=w_transposed)
    return pl.pallas_call(
        kern,
        out_shape=jax.ShapeDtypeStruct((m, n), out_dtype),
        grid=(n // tn, m // tm),
        in_specs=in_specs,
        out_specs=pl.BlockSpec((tm, tn), lambda j, i: (i, j)),
        scratch_shapes=scratch,
        compiler_params=_params("arbitrary", "arbitrary"),
        name=name,
    )(*a_list, *w_arrays, *extra_arrays)


def _gate_proj_kernel(h_ref, wt_hbm, o_ref, wbuf, wb, sem, *, layer, first):
    j, i = pl.program_id(0), pl.program_id(1)
    tn = wb.shape[0]

    def copy(jj, slot):
        rows = pl.ds(pl.multiple_of(first + jj * tn, 8), tn)
        return pltpu.make_async_copy(wt_hbm.at[layer, rows, :], wbuf.at[slot], sem.at[slot])

    @pl.when(i == 0)
    def _():
        slot = lax.rem(j, 2)

        @pl.when(j == 0)
        def _():
            copy(0, 0).start()

        copy(j, slot).wait()
        wb[...] = wbuf[slot].astype(BF16)

        @pl.when(j + 1 < pl.num_programs(0))
        def _():
            copy(j + 1, 1 - slot).start()

    acc = lax.dot_general(h_ref[...], wb[...], (((1,), (1,)), ((), ())), preferred_element_type=F32)
    o_ref[...] = _sigmoid(acc).astype(o_ref.dtype)


def gate_projection(h, w_t, layer, first, n, tm=1024, tn=1024):
    m, k = h.shape
    assert first % 8 == 0 and n % tn == 0 and m % tm == 0
    kern = functools.partial(_gate_proj_kernel, layer=layer, first=first)
    return pl.pallas_call(
        kern,
        out_shape=jax.ShapeDtypeStruct((m, n), BF16),
        grid=(n // tn, m // tm),
        in_specs=[pl.BlockSpec((tm, k), lambda j, i: (i, 0)), pl.BlockSpec(memory_space=pl.ANY)],
        out_specs=pl.BlockSpec((tm, tn), lambda j, i: (i, j)),
        scratch_shapes=[pltpu.VMEM((2, tn, k), F32), pltpu.VMEM((tn, k), BF16),
                        pltpu.SemaphoreType.DMA((2,))],
        compiler_params=_params("arbitrary", "arbitrary"),
        name="in_proj_gates",
    )(h, w_t)


def _ep_plain(accs, ex):
    return accs[0]


def _ep_residual(accs, ex):
    return ex[0] + accs[0]


def _ep_merge(accs, ex):
    return ex[0].astype(F32) * accs[0] + ex[1].astype(F32) * accs[1]


def _ep_swiglu(accs, ex):
    return _silu(accs[0]) * accs[1]


def _gate_kernel(h_ref, wlr_ref, w2_ref, b_ref, o_ref):
    lr = lax.dot_general(h_ref[...], wlr_ref[...].astype(BF16), (((1,), (1,)), ((), ())),
                         preferred_element_type=F32)
    z = _dot3(lr, w2_ref[...]) + b_ref[...]
    o_ref[...] = _log_sigmoid(z) * (1.0 / GLA_GATE_TAU)


def gla_gates(h, w_lr_t, gate_w2, gate_b, tm=512):
    m, d = h.shape
    r = GLA_LOWRANK
    n = 2 * GLA_DK_TOTAL
    wlr = jnp.zeros((LANES, d), F32).at[:2 * r].set(w_lr_t)
    w2 = jnp.zeros((LANES, n), F32)
    w2 = w2.at[:r, :GLA_DK_TOTAL].set(gate_w2[0]).at[r:2 * r, GLA_DK_TOTAL:].set(gate_w2[1])
    b = gate_b.reshape(1, n)
    return pl.pallas_call(
        _gate_kernel,
        out_shape=jax.ShapeDtypeStruct((m, n), F32),
        grid=(m // tm,),
        in_specs=[pl.BlockSpec((tm, d), lambda i: (i, 0)),
                  pl.BlockSpec((LANES, d), lambda i: (0, 0)),
                  pl.BlockSpec((LANES, n), lambda i: (0, 0)),
                  pl.BlockSpec((1, n), lambda i: (0, 0))],
        out_specs=pl.BlockSpec((tm, n), lambda i: (i, 0)),
        compiler_params=_params("parallel"),
        name="gla_gates",
    )(h, wlr, w2, b)


GLA_SUB = 4


def _gla_chunk(q, k, v, la, st_ref, h, reverse):
    c = GLA_CHUNK
    row = lax.broadcasted_iota(jnp.int32, (c, c), 0)
    col = lax.broadcasted_iota(jnp.int32, (c, c), 1)
    keep = (col > row) if reverse else (col <= row)
    b = la
    rows = lax.broadcasted_iota(jnp.int32, la.shape, 0)
    step = 1
    while step < c:
        if reverse:
            b = b + jnp.where(rows < c - step, pltpu.roll(b, c - step, 0), 0.0)
        else:
            b = b + jnp.where(rows >= step, pltpu.roll(b, step, 0), 0.0)
        step *= 2
    tot = b[0:1] if reverse else b[c - 1:c]
    q_in = (q * ((GLA_DK ** -0.5) * jnp.exp(b))).astype(BF16)
    k_dec = k * jnp.exp(-b)
    k_in = k_dec.astype(BF16)
    k_out = (k_dec * jnp.exp(tot)).astype(BF16)
    vb = v.astype(BF16)
    s = lax.dot_general(q_in, k_in, (((1,), (1,)), ((), ())), preferred_element_type=F32)
    s = jnp.where(keep, s, 0.0).astype(BF16)
    st = st_ref[h]
    o = _dot(s, vb) + lax.dot_general(q_in, st.astype(BF16), (((1,), (1,)), ((), ())),
                                      preferred_element_type=F32)
    upd = lax.dot_general(vb, k_out, (((0,), (0,)), ((), ())), preferred_element_type=F32)
    st_ref[h] = st * jnp.exp(tot) + upd
    return o


def _gla_kernel(qf_ref, kf_ref, vf_ref, lf_ref, qb_ref, kb_ref, vb_ref, lb_ref,
                of_ref, ob_ref, sf_ref, sb_ref):
    @pl.when(pl.program_id(0) == 0)
    def _():
        sf_ref[...] = jnp.zeros_like(sf_ref)
        sb_ref[...] = jnp.zeros_like(sb_ref)

    def body(s, carry):
        rf = pl.multiple_of(s * GLA_CHUNK, GLA_CHUNK)
        rb = pl.multiple_of((GLA_SUB - 1 - s) * GLA_CHUNK, GLA_CHUNK)
        for h in range(GLA_HEADS):
            kk = slice(h * GLA_DK, (h + 1) * GLA_DK)
            vv = slice(h * GLA_DV, (h + 1) * GLA_DV)
            of_ref[pl.ds(rf, GLA_CHUNK), vv] = _gla_chunk(
                qf_ref[pl.ds(rf, GLA_CHUNK), kk], kf_ref[pl.ds(rf, GLA_CHUNK), kk],
                vf_ref[pl.ds(rf, GLA_CHUNK), vv], lf_ref[pl.ds(rf, GLA_CHUNK), kk],
                sf_ref, h, False)
            ob_ref[pl.ds(rb, GLA_CHUNK), vv] = _gla_chunk(
                qb_ref[pl.ds(rb, GLA_CHUNK), kk], kb_ref[pl.ds(rb, GLA_CHUNK), kk],
                vb_ref[pl.ds(rb, GLA_CHUNK), vv], lb_ref[pl.ds(rb, GLA_CHUNK), kk],
                sb_ref, h, True)
        return carry

    lax.fori_loop(0, GLA_SUB, body, 0, unroll=True)


def gla_scan(qkvr, first, la):
    L = qkvr.shape[0]
    rows = GLA_CHUNK * GLA_SUB
    n = L // rows
    dk, dv = GLA_DK_TOTAL, GLA_DV_TOTAL
    assert first % dv == 0
    qb, vb = first // dk, (first + 2 * dk) // dv
    specs = []
    for rev in (False, True):
        blk = (lambda i: n - 1 - i) if rev else (lambda i: i)
        specs += [pl.BlockSpec((rows, dk), lambda i, blk=blk: (blk(i), qb)),
                  pl.BlockSpec((rows, dk), lambda i, blk=blk: (blk(i), qb + 1)),
                  pl.BlockSpec((rows, dv), lambda i, blk=blk: (blk(i), vb)),
                  pl.BlockSpec((rows, dk), lambda i, blk=blk, c=int(rev): (blk(i), c))]
    return pl.pallas_call(
        _gla_kernel,
        out_shape=(jax.ShapeDtypeStruct((L, dv), F32), jax.ShapeDtypeStruct((L, dv), F32)),
        grid=(n,),
        in_specs=specs,
        out_specs=(pl.BlockSpec((rows, dv), lambda i: (i, 0)),
                   pl.BlockSpec((rows, dv), lambda i: (n - 1 - i, 0))),
        scratch_shapes=[pltpu.VMEM((GLA_HEADS, GLA_DV, GLA_DK), F32),
                        pltpu.VMEM((GLA_HEADS, GLA_DV, GLA_DK), F32)],
        compiler_params=_params("arbitrary"),
        name="gla_scan",
    )(qkvr, qkvr, qkvr, la, qkvr, qkvr, qkvr, la)


def _gla_post_kernel(of_ref, ob_ref, r_ref, g_ref, o_ref):
    g = g_ref[...]
    for h in range(GLA_HEADS):
        vv = slice(h * GLA_DV, (h + 1) * GLA_DV)
        o = of_ref[:, vv] + ob_ref[:, vv]
        ms = jnp.mean(o * o, axis=-1, keepdims=True)
        y = o * lax.rsqrt(ms + RMS_EPS) * g
        o_ref[:, vv] = (y * _silu(r_ref[:, vv])).astype(o_ref.dtype)


def gla_post(o_f, o_b, qkvr, first, norm_g, tm=1024):
    L, dv = o_f.shape
    assert first % dv == 0
    return pl.pallas_call(
        _gla_post_kernel,
        out_shape=jax.ShapeDtypeStruct((L, dv), BF16),
        grid=(L // tm,),
        in_specs=[pl.BlockSpec((tm, dv), lambda i: (i, 0)),
                  pl.BlockSpec((tm, dv), lambda i: (i, 0)),
                  pl.BlockSpec((tm, dv), lambda i: (i, first // dv)),
                  pl.BlockSpec((1, GLA_DV), lambda i: (0, 0))],
        out_specs=pl.BlockSpec((tm, dv), lambda i: (i, 0)),
        compiler_params=_params("parallel"),
        name="gla_post",
    )(o_f, o_b, qkvr, norm_g.reshape(1, GLA_DV))


CT = 256
CT_G = 1024
CT_COL = 128
MAX_SLABS = 16
SUB = 8


def _fft_dims(L):
    n = 2 * L
    na = 1 << ((n.bit_length() - 1) // 2)
    nb = n // na
    assert na * nb == n and nb % SUB == 0 and (na // 2) % 8 == 0
    return n, na, nb, na // 2 + 1


def _slabs_per_step(kp):
    return max(s for s in range(1, MAX_SLABS + 1) if kp % s == 0)


def _cis(num, den):
    ang = (2.0 * math.pi / den) * (num % den).astype(F32)
    return jnp.cos(ang), jnp.sin(ang)


def _expand_kernel(t_ref, o_ref):
    t = t_ref[0].astype(BF16)
    cc, cols = t.shape[1], o_ref.shape[2]
    shift, low = SUB.bit_length() - 1, SUB - 1
    src = lax.broadcasted_iota(jnp.int32, (cc, cols), 0)
    dst = lax.broadcasted_iota(jnp.int32, (cc, cols), 1)
    spread = jnp.where(lax.shift_right_logical(dst, shift) == src, 1.0, 0.0).astype(BF16)
    x = _dot(t, spread)
    row = lax.broadcasted_iota(jnp.int32, x.shape, 0)
    col = lax.broadcasted_iota(jnp.int32, x.shape, 1)
    o_ref[0] = jnp.where((col & low) == (row & low), x, 0.0).astype(o_ref.dtype)


def _expand_block_diag(t):
    g, r, c_in = t.shape
    c = -(-c_in // 16) * 16
    if c != c_in:
        return _expand_block_diag(jnp.pad(t, ((0, 0), (0, 0), (0, c - c_in))))[:, :, :c_in * SUB]
    return pl.pallas_call(
        _expand_kernel,
        out_shape=jax.ShapeDtypeStruct((g, r, c * SUB), BF16),
        grid=(g,),
        in_specs=[pl.BlockSpec((1, r, c), lambda i: (i, 0, 0))],
        out_specs=pl.BlockSpec((1, r, c * SUB), lambda i: (i, 0, 0)),
        compiler_params=_params("parallel"),
        name="dft_table_expand",
    )(t)


def _fft_tables(L):
    n, na, nb, kp = _fft_dims(L)
    ha, ng = na // 2, nb // SUB
    ar = lambda m: jnp.arange(m, dtype=jnp.int32)
    g_, ka_, bl_, a_ = ar(ng)[:, None, None, None], ar(kp)[None, :, None, None], \
        ar(SUB)[None, None, :, None], ar(ha)[None, None, None, :]
    c, s = _cis(ka_ * (nb * a_ + SUB * g_ + bl_), n)
    cs = jnp.stack([c, -s], axis=2)
    m8 = _expand_block_diag(cs.reshape(ng, kp * 2 * SUB, ha))
    c, s = _cis(ar(nb)[:, None] * ar(nb)[None, :], nb)
    f2 = jnp.stack([jnp.stack([c, s], axis=1), jnp.stack([-s, c], axis=1)], axis=0)
    f2 = f2.reshape(2, nb, 2, ng, SUB).transpose(0, 1, 3, 2, 4).reshape(2 * nb, 2 * nb)
    ca, sa = _cis(ar(kp)[:, None] * ar(nb)[None, :], n)
    cb, sb = _cis(ar(nb)[:, None] * ar(nb)[None, :], nb)
    ca, sa = (t.reshape(kp, ng, 1, SUB, 1, 1) / n for t in (ca, sa))
    cb, sb = (t.reshape(1, ng, 1, SUB, 1, nb) for t in (cb, sb))
    c, s = ca * cb - sa * sb, sa * cb + ca * sb
    gi = jnp.concatenate([jnp.concatenate([c, -s], axis=4),
                          jnp.concatenate([s, c], axis=4)], axis=2)
    gi = gi.reshape(kp, 2 * nb, 2 * nb)
    kc = ar(kp)[None, :]
    wgt = jnp.where((kc == 0) | (kc == ha), 1.0, jnp.where(kc < ha, 2.0, 0.0))
    c, s = _cis(ar(ha)[:, None] * kc, na)
    pm = jnp.stack([wgt * c, -wgt * s], axis=2).reshape(ha, 1, kp * 2)
    p8 = _expand_block_diag(jnp.broadcast_to(pm, (ha, SUB, kp * 2)).reshape(1, ha * SUB, kp * 2))[0]
    return m8, f2.astype(BF16), gi.astype(BF16), p8


def _short_conv_kernel(u_ref, w_ref, b_ref, o_ref):
    u = u_ref[...]
    L = u.shape[0]
    row = lax.broadcasted_iota(jnp.int32, u.shape, 0)
    prev = jnp.where(row == 0, 0.0, pltpu.roll(u, 1, 0))
    nxt = jnp.where(row == L - 1, 0.0, pltpu.roll(u, L - 1, 0))
    w = w_ref[...]
    o_ref[...] = prev * w[0:1] + u * w[1:2] + nxt * w[2:3] + b_ref[...]


def short_conv(u, w, b):
    L, c = u.shape[0], w.shape[1]
    w8 = jnp.zeros((8, c), F32).at[:SHORT_CONV].set(w)
    return pl.pallas_call(
        _short_conv_kernel,
        out_shape=jax.ShapeDtypeStruct((L, c), F32),
        grid=(c // CT_COL,),
        in_specs=[pl.BlockSpec((L, CT_COL), lambda j: (0, j)),
                  pl.BlockSpec((8, CT_COL), lambda j: (0, j)),
                  pl.BlockSpec((1, CT_COL), lambda j: (0, j))],
        out_specs=pl.BlockSpec((L, CT_COL), lambda j: (0, j)),
        compiler_params=_params("parallel"),
        name="short_conv",
    )(u, w8, b.reshape(1, c))


def _fft_s1_kernel(u_ref, m8_ref, a_ref):
    ha, sub, ct = u_ref.shape
    x = u_ref[...].reshape(ha * sub, ct).astype(BF16)
    r = _dot(m8_ref[0], x)
    a_ref[:, 0, :, :] = r.astype(a_ref.dtype).reshape(a_ref.shape[0], 2 * sub, ct)


def fft_stage1(u, ncols, m8, L):
    n, na, nb, kp = _fft_dims(L)
    ha, ng = na // 2, nb // SUB
    u3 = u.reshape(ha, nb, u.shape[1])
    return pl.pallas_call(
        _fft_s1_kernel,
        out_shape=jax.ShapeDtypeStruct((kp, ng, 2 * SUB, ncols), BF16),
        grid=(ng, ncols // CT_G),
        in_specs=[pl.BlockSpec((ha, SUB, CT_G), lambda g, j: (0, g, j)),
                  pl.BlockSpec((1, kp * 2 * SUB, ha * SUB), lambda g, j: (g, 0, 0))],
        out_specs=pl.BlockSpec((kp, 1, 2 * SUB, CT_G), lambda g, j: (0, g, 0, j)),
        compiler_params=_params("parallel", "parallel"),
        name="fft_stage1",
    )(u3, m8)


def _filter_hidden_kernel(zt_ref, w1_ref, b1_ref, f1_ref, w2_ref, b2_ref, f2_ref, w3_ref, b3_ref,
                          f3_ref, o_ref):
    h = jnp.sin(f1_ref[...] * (_dot3(w1_ref[...], zt_ref[...]) + b1_ref[...]))
    h = jnp.sin(f2_ref[...] * (_dot3(w2_ref[...], h) + b2_ref[...]))
    h = jnp.sin(f3_ref[...] * (_dot3(w3_ref[...], h) + b3_ref[...]))
    o_ref[...] = h.T


def filter_hidden(L, w1, b1, f1, w2, b2, f2, w3, b3, f3, tt=1024):
    hdim = FILTER_HIDDEN
    tt = min(tt, L)
    t = jnp.linspace(0.0, 1.0, L, dtype=F32)[None, :]
    omega = 2.0 * math.pi * jnp.arange(L, dtype=F32)[None, :] / L
    bands = jnp.linspace(1e-4, POS_BANDS - 1, POS_BANDS, dtype=F32)[:, None]
    zt = jnp.concatenate([t, jnp.cos(bands * omega), -jnp.sin(bands * omega)], axis=0)
    zt = jnp.zeros((LANES, L), F32).at[:POS_EMB_DIM].set(zt)
    w1t = jnp.zeros((hdim, LANES), F32).at[:, :POS_EMB_DIM].set(w1.T)
    col = lambda v: v.reshape(hdim, 1)
    full = lambda shape: pl.BlockSpec(shape, lambda i: (0, 0))
    return pl.pallas_call(
        _filter_hidden_kernel,
        out_shape=jax.ShapeDtypeStruct((L, hdim), F32),
        grid=(L // tt,),
        in_specs=[pl.BlockSpec((LANES, tt), lambda i: (0, i)),
                  full((hdim, LANES)), full((hdim, 1)), full((hdim, 1)),
                  full((hdim, hdim)), full((hdim, 1)), full((hdim, 1)),
                  full((hdim, hdim)), full((hdim, 1)), full((hdim, 1))],
        out_specs=pl.BlockSpec((tt, hdim), lambda i: (i, 0)),
        compiler_params=_params("parallel"),
        name="hyena_filter_hidden",
    )(zt, w1t, col(b1), col(f1), w2.T, col(b2), col(f2), w3.T, col(b3), col(f3))


def _filter_s1_kernel(h_ref, w4_ref, dl_ref, m8_ref, a_ref, nrm_ref, *, nb, L, cb):
    g, j = pl.program_id(0), pl.program_id(1)
    ha, sub, hdim = h_ref.shape
    rows = ha * sub
    wh = w4_ref[...].astype(BF16)
    ct = wh.shape[1]
    k = _dot(h_ref[...].reshape(rows, hdim).astype(BF16), wh).reshape(ha, sub, ct)
    a3 = lax.broadcasted_iota(jnp.int32, (ha, sub, ct), 0)
    b3 = lax.broadcasted_iota(jnp.int32, (ha, sub, ct), 1)
    tpos = a3 * nb + (b3 + g * sub)
    k = k * jnp.exp(-(tpos.astype(F32) * (1.0 / (L - 1))) * dl_ref[...])
    k = jnp.where(jnp.logical_and((j // cb) % 2 == 1, tpos == 0), 0.0, k).reshape(rows, ct)
    nrm_ref[0] = jnp.sum(jnp.abs(k), axis=0, keepdims=True)
    q = _dot(m8_ref[0], k.astype(BF16))
    a_ref[:, 0, :, :] = q.astype(a_ref.dtype).reshape(a_ref.shape[0], 2 * sub, ct)


def filter_stage1(L, w1, b1, f1, w2, b2, f2, w3, b3, f3, w4, m8):
    n, na, nb, kp = _fft_dims(L)
    ha, ng = na // 2, nb // SUB
    c = HYENA_WIDTH
    hdim = FILTER_HIDDEN
    hid = filter_hidden(L, w1, b1, f1, w2, b2, f2, w3, b3, f3).reshape(ha, nb, hdim)
    deltas = jnp.abs(jnp.linspace(MIN_DECAY, MAX_DECAY, c, dtype=F32)).reshape(1, c)
    nc = w4.shape[1]
    cb = c // CT_G
    kern = functools.partial(_filter_s1_kernel, nb=nb, L=L, cb=cb)
    return pl.pallas_call(
        kern,
        out_shape=(jax.ShapeDtypeStruct((kp, ng, 2 * SUB, nc), BF16),
                   jax.ShapeDtypeStruct((ng, 1, nc), F32)),
        grid=(ng, nc // CT_G),
        in_specs=[pl.BlockSpec((ha, SUB, hdim), lambda g, j: (0, g, 0)),
                  pl.BlockSpec((hdim, CT_G), lambda g, j: (0, j)),
                  pl.BlockSpec((1, CT_G), lambda g, j: (0, j % cb)),
                  pl.BlockSpec((1, kp * 2 * SUB, ha * SUB), lambda g, j: (g, 0, 0))],
        out_specs=(pl.BlockSpec((kp, 1, 2 * SUB, CT_G), lambda g, j: (0, g, 0, j)),
                   pl.BlockSpec((1, 1, CT_G), lambda g, j: (g, 0, j))),
        compiler_params=_params("parallel", "parallel"),
        name="hyena_filter_stage1",
    )(hid, w4, deltas, m8)


def _slab(ref, s):
    _, ng, rows, ct = ref.shape
    return ref[s].reshape(ng * rows, ct)


def _filter_spec_kernel(af_ref, ab_ref, f2_ref, nf_ref, nbk_ref, h_ref):
    f2 = f2_ref[...]
    nb = f2.shape[0] // 2
    inv = 1.0 / (jnp.sum(nf_ref[...], axis=0) + jnp.sum(nbk_ref[...], axis=0))
    for s in range(af_ref.shape[0]):
        xf = _dot(f2, _slab(af_ref, s))
        xb = _dot(f2, _slab(ab_ref, s))
        h_ref[0, s] = ((xf[:nb] + xb[:nb]) * inv).astype(h_ref.dtype)
        h_ref[1, s] = ((xf[nb:] - xb[nb:]) * inv).astype(h_ref.dtype)


def filter_spectrum(a, nrm, f2, L):
    n, na, nb, kp = _fft_dims(L)
    ng = nb // SUB
    c = HYENA_WIDTH
    cb = c // CT
    ms = _slabs_per_step(kp)
    fcol = lambda j: (j // cb) * 2 * cb + j % cb
    return pl.pallas_call(
        _filter_spec_kernel,
        out_shape=jax.ShapeDtypeStruct((2, kp, nb, HYENA_ORDER * c), BF16),
        grid=(HYENA_ORDER * cb, kp // ms),
        in_specs=[pl.BlockSpec((ms, ng, 2 * SUB, CT), lambda j, k: (k, 0, 0, fcol(j))),
                  pl.BlockSpec((ms, ng, 2 * SUB, CT), lambda j, k: (k, 0, 0, fcol(j) + cb)),
                  pl.BlockSpec((2 * nb, 2 * nb), lambda j, k: (0, 0)),
                  pl.BlockSpec((ng, 1, CT), lambda j, k: (0, 0, fcol(j))),
                  pl.BlockSpec((ng, 1, CT), lambda j, k: (0, 0, fcol(j) + cb))],
        out_specs=pl.BlockSpec((2, ms, nb, CT), lambda j, k: (0, k, 0, j)),
        compiler_params=_params("parallel", "parallel"),
        name="hyena_filter_spectrum",
    )(a, a, f2, nrm, nrm)


def _fft_mid_kernel(a_ref, h_ref, f2_ref, g_ref, q_ref):
    f2 = f2_ref[...]
    nb = f2.shape[0] // 2
    slabs, ng, rows, ct = q_ref.shape
    for s in range(slabs):
        x = _dot(f2, _slab(a_ref, s))
        xr, xi = x[:nb], x[nb:]
        hr, hi = h_ref[0, s].astype(F32), h_ref[1, s].astype(F32)
        y = jnp.concatenate([xr * hr - xi * hi, xr * hi + xi * hr], axis=0).astype(BF16)
        q_ref[s] = _dot(g_ref[s], y).astype(q_ref.dtype).reshape(ng, rows, ct)


def fft_mid(a, hf, order, f2, g, L):
    n, na, nb, kp = _fft_dims(L)
    ng = nb // SUB
    c = a.shape[3]
    cb = c // CT
    ms = _slabs_per_step(kp)
    return pl.pallas_call(
        _fft_mid_kernel,
        out_shape=jax.ShapeDtypeStruct(a.shape, BF16),
        grid=(kp // ms, cb),
        in_specs=[pl.BlockSpec((ms, ng, 2 * SUB, CT), lambda k, j: (k, 0, 0, j)),
                  pl.BlockSpec((2, ms, nb, CT), lambda k, j: (0, k, 0, order * cb + j)),
                  pl.BlockSpec((2 * nb, 2 * nb), lambda k, j: (0, 0)),
                  pl.BlockSpec((ms, 2 * nb, 2 * nb), lambda k, j: (k, 0, 0))],
        out_specs=pl.BlockSpec((ms, ng, 2 * SUB, CT), lambda k, j: (k, 0, 0, j)),
        compiler_params=_params("parallel", "parallel"),
        name="fft_mid",
    )(a, hf, f2, g)


def _fft_i2_kernel(q_ref, p8_ref, u_ref, x_ref, s_ref, o_ref):
    kp, _, rows, ct = q_ref.shape
    q = q_ref[:, 0, :, :].reshape(kp * rows, ct)
    y = _dot(p8_ref[...], q).reshape(o_ref.shape)
    o_ref[...] = x_ref[...] * (y + u_ref[...] * s_ref[...])


def fft_stage_out(q, u, gate_src, gate_blk, skip, p8, L):
    n, na, nb, kp = _fft_dims(L)
    ha, ng = na // 2, nb // SUB
    c = q.shape[3]
    u3 = u.reshape(ha, nb, u.shape[1])
    g3 = gate_src.reshape(ha, nb, gate_src.shape[1])
    out = pl.pallas_call(
        _fft_i2_kernel,
        out_shape=jax.ShapeDtypeStruct((ha, nb, c), F32),
        grid=(ng, c // CT_G),
        in_specs=[pl.BlockSpec((kp, 1, 2 * SUB, CT_G), lambda g, j: (0, g, 0, j)),
                  pl.BlockSpec((ha * SUB, kp * 2 * SUB), lambda g, j: (0, 0)),
                  pl.BlockSpec((ha, SUB, CT_G), lambda g, j: (0, g, j)),
                  pl.BlockSpec((ha, SUB, CT_G), lambda g, j: (0, g, gate_blk + j)),
                  pl.BlockSpec((1, 1, CT_G), lambda g, j: (0, 0, j))],
        out_specs=pl.BlockSpec((ha, SUB, CT_G), lambda g, j: (0, g, j)),
        compiler_params=_params("parallel", "parallel"),
        name="fft_stage_out",
    )(q, p8, u3, g3, skip.reshape(1, 1, c))
    return out.reshape(L, c)


def hyena_mixer(u_hy, conv_w, conv_b, filt, skip, tables):
    L = u_hy.shape[0]
    c = HYENA_WIDTH
    m8, f2, g, p8 = tables
    uc = short_conv(u_hy, conv_w, conv_b)
    a_filt, nrm = filter_stage1(L, *filt, m8)
    hf = filter_spectrum(a_filt, nrm, f2, L)
    z = uc
    for order in range(HYENA_ORDER):
        a = fft_stage1(z, c, m8, L)
        q = fft_mid(a, hf, order, f2, g, L)
        z = fft_stage_out(q, z, uc, (order + 1) * (c // CT_G), skip[order], p8, L)
    return z


MOE_TM = 512
MOE_TN = 512
COMBINE_TM = 512


def _moe_plan(route, counts, L):
    tm = MOE_TM
    nt = -(-(2 * L + N_EXPERTS * (tm - 1)) // tm)
    e_flat = jnp.concatenate([route[:, ROUTE_E1], route[:, ROUTE_E2]]).astype(jnp.int32)
    rank = jnp.concatenate([route[:, ROUTE_R1], route[:, ROUTE_R2]]).astype(jnp.int32)
    counts = counts[0, :N_EXPERTS].astype(jnp.int32)
    padded = ((counts + tm - 1) // tm) * tm
    ends = jnp.cumsum(padded)
    dest = (ends - padded)[e_flat] + rank
    tok = jnp.tile(jnp.arange(L, dtype=jnp.int32), 2)
    row_token = jnp.zeros((nt * tm,), jnp.int32).at[dest].set(tok, unique_indices=True)
    n_used = ends[-1] // tm
    tile_row = jnp.arange(nt, dtype=jnp.int32) * tm
    tile_expert = jnp.sum((tile_row[:, None] >= ends[None, :]).astype(jnp.int32), axis=1)
    last = jnp.take(tile_expert, jnp.maximum(n_used - 1, 0))
    tile_expert = jnp.where(jnp.arange(nt) < n_used, tile_expert, last)
    tile_first = jnp.concatenate([jnp.ones((1,), jnp.int32),
                                  (tile_expert[1:] != tile_expert[:-1]).astype(jnp.int32)])
    tile_group = jnp.cumsum(tile_first) - 1
    group_expert = jnp.zeros((N_EXPERTS,), jnp.int32).at[tile_group].set(tile_expert)
    n_groups = jnp.take(tile_group, jnp.maximum(n_used - 1, 0)) + 1
    i32 = lambda v: v.reshape(1).astype(jnp.int32)
    last_row = (ends - padded + counts)[tile_expert]
    tile_valid = jnp.where(jnp.arange(nt) < n_used, jnp.clip(last_row - tile_row, 0, tm), 0)
    tile_chunks = ((tile_valid + GATHER_CHUNK - 1) // GATHER_CHUNK).astype(jnp.int32)
    return dict(nt=nt, row_token=row_token, dest=dest, tile_chunks=tile_chunks,
                scalars=(tile_expert, tile_first, i32(n_used), tile_group.astype(jnp.int32),
                         i32(n_groups), group_expert))


def _row_copy(src_hbm, idx, buf, slot, r, sem):
    return pltpu.make_async_copy(src_hbm.at[pl.ds(idx, 1)], buf.at[slot, pl.ds(r, 1)],
                                 sem.at[slot])


def _gather_rows(idx_ref, src_hbm, buf, slot, sem, n_rows, start):
    if not start:
        pltpu.make_async_copy(src_hbm.at[pl.ds(0, n_rows)], buf.at[slot], sem.at[slot]).wait()
        return

    def body(r, carry):
        _row_copy(src_hbm, idx_ref[0, 0, r], buf, slot, r, sem).start(priority=1)
        return carry
    lax.fori_loop(0, n_rows, body, 0, unroll=8)


def _pipelined_gather(cur_ref, nxt_ref, src_hbm, buf, sem, n_rows):
    i, n = pl.program_id(0), pl.num_programs(0)
    slot = lax.rem(i, 2)

    @pl.when(i == 0)
    def _():
        _gather_rows(cur_ref, src_hbm, buf, 0, sem, n_rows, True)

    @pl.when(i + 1 < n)
    def _():
        _gather_rows(nxt_ref, src_hbm, buf, 1 - slot, sem, n_rows, True)

    _gather_rows(cur_ref, src_hbm, buf, slot, sem, n_rows, False)
    return slot


GATHER_CHUNK = 32


def _dispatch_rows(idx_ref, h_hbm, buf, slot, sem, n_chunks, start):
    def body(c, carry):
        base = pl.multiple_of(c * GATHER_CHUNK, GATHER_CHUNK)
        if start:
            for r in range(GATHER_CHUNK):
                _row_copy(h_hbm, idx_ref[0, 0, base + r], buf, slot, base + r, sem).start(priority=1)
        else:
            pltpu.make_async_copy(h_hbm.at[pl.ds(0, GATHER_CHUNK)],
                                  buf.at[slot, pl.ds(base, GATHER_CHUNK)], sem.at[slot]).wait()
        return carry
    lax.fori_loop(0, n_chunks, body, 0)


def _dispatch_kernel(nch_ref, cur_ref, nxt_ref, h_hbm, o_ref, buf, sem):
    i, n = pl.program_id(0), pl.num_programs(0)
    slot = lax.rem(i, 2)

    @pl.when(i == 0)
    def _():
        buf[...] = jnp.zeros_like(buf)
        _dispatch_rows(cur_ref, h_hbm, buf, 0, sem, nch_ref[0], True)

    @pl.when(i + 1 < n)
    def _():
        _dispatch_rows(nxt_ref, h_hbm, buf, 1 - slot, sem, nch_ref[jnp.minimum(i + 1, n - 1)], True)

    _dispatch_rows(cur_ref, h_hbm, buf, slot, sem, nch_ref[i], False)
    o_ref[...] = buf[slot].astype(o_ref.dtype)


def moe_dispatch(h, row_token, tile_chunks, nt):
    L, d = h.shape
    idx = row_token.reshape(nt, 1, MOE_TM)
    smem = lambda f: pl.BlockSpec((1, 1, MOE_TM), f, memory_space=pltpu.SMEM)
    return pl.pallas_call(
        _dispatch_kernel,
        out_shape=jax.ShapeDtypeStruct((nt * MOE_TM, d), BF16),
        grid_spec=pltpu.PrefetchScalarGridSpec(
            num_scalar_prefetch=1,
            grid=(nt,),
            in_specs=[smem(lambda i, nch: (i, 0, 0)),
                      smem(lambda i, nch: (jnp.minimum(i + 1, nt - 1), 0, 0)),
                      pl.BlockSpec(memory_space=pl.ANY)],
            out_specs=pl.BlockSpec((MOE_TM, d), lambda i, nch: (i, 0)),
            scratch_shapes=[pltpu.VMEM((2, MOE_TM, d), F32), pltpu.SemaphoreType.DMA((2,))]),
        compiler_params=_params("arbitrary"),
        name="moe_dispatch",
    )(tile_chunks, idx, idx, h)


def _weight_copy(w_hbm, e, jcol, wbuf, k, slot, sem):
    tn = wbuf.shape[3]
    return pltpu.make_async_copy(w_hbm.at[e, :, pl.ds(pl.multiple_of(jcol * tn, tn), tn)],
                                 wbuf.at[k, slot], sem.at[k, slot])


def _refresh_expert_weights(scalars, w_hbms, wbuf, wbs, sem):
    te_ref, tf_ref, _, tg_ref, ng_ref, ge_ref = scalars
    j, i = pl.program_id(0), pl.program_id(1)
    nj = pl.num_programs(0)

    @pl.when(tf_ref[i] == 1)
    def _():
        g, ng = tg_ref[i], ng_ref[0]
        blk = j * ng + g
        slot = lax.rem(blk, 2)

        @pl.when(blk == 0)
        def _():
            for k, w in enumerate(w_hbms):
                _weight_copy(w, te_ref[i], j, wbuf, k, 0, sem).start()

        for k, w in enumerate(w_hbms):
            _weight_copy(w, 0, 0, wbuf, k, slot, sem).wait()
        for k, wb in enumerate(wbs):
            wb[...] = wbuf[k, slot].astype(BF16)

        wrap = g + 1 == ng
        nxt_e = ge_ref[jnp.where(wrap, 0, g + 1)]
        nxt_j = j + wrap.astype(jnp.int32)

        @pl.when(nxt_j < nj)
        def _():
            for k, w in enumerate(w_hbms):
                _weight_copy(w, nxt_e, nxt_j, wbuf, k, 1 - slot, sem).start()


def _moe_up_kernel(te_ref, tf_ref, nu_ref, tg_ref, ng_ref, ge_ref, a_ref, wg_hbm, wu_hbm, o_ref,
                   wbuf, wgb, wub, sem):
    i = pl.program_id(1)
    _refresh_expert_weights((te_ref, tf_ref, nu_ref, tg_ref, ng_ref, ge_ref),
                            (wg_hbm, wu_hbm), wbuf, (wgb, wub), sem)

    @pl.when(i < nu_ref[0])
    def _():
        a = a_ref[...]
        o_ref[...] = (_silu(_dot(a, wgb[...])) * _dot(a, wub[...])).astype(o_ref.dtype)

    @pl.when(i >= nu_ref[0])
    def _():
        o_ref[...] = jnp.zeros_like(o_ref)


def moe_up(hs, w_gate, w_up, plan):
    r, d = hs.shape
    f = w_gate.shape[2]
    nt = plan["nt"]
    anyspec = pl.BlockSpec(memory_space=pl.ANY)
    return pl.pallas_call(
        _moe_up_kernel,
        out_shape=jax.ShapeDtypeStruct((r, f), BF16),
        grid_spec=pltpu.PrefetchScalarGridSpec(
            num_scalar_prefetch=6,
            grid=(f // MOE_TN, nt),
            in_specs=[pl.BlockSpec((MOE_TM, d), lambda j, i, *_: (i, 0)), anyspec, anyspec],
            out_specs=pl.BlockSpec((MOE_TM, MOE_TN), lambda j, i, *_: (i, j)),
            scratch_shapes=[pltpu.VMEM((2, 2, d, MOE_TN), F32),
                            pltpu.VMEM((d, MOE_TN), BF16), pltpu.VMEM((d, MOE_TN), BF16),
                            pltpu.SemaphoreType.DMA((2, 2))]),
        compiler_params=_params("arbitrary", "arbitrary"),
        name="moe_up",
    )(*plan["scalars"], hs, w_gate, w_up)


def _moe_down_kernel(te_ref, tf_ref, nu_ref, tg_ref, ng_ref, ge_ref, t_ref, wd_hbm, o_ref,
                     wbuf, wdb, sem):
    i = pl.program_id(1)
    _refresh_expert_weights((te_ref, tf_ref, nu_ref, tg_ref, ng_ref, ge_ref),
                            (wd_hbm,), wbuf, (wdb,), sem)

    @pl.when(i < nu_ref[0])
    def _():
        o_ref[...] = _dot(t_ref[...], wdb[...])

    @pl.when(i >= nu_ref[0])
    def _():
        o_ref[...] = jnp.zeros_like(o_ref)


def moe_down(t, w_down, plan):
    r, f = t.shape
    d = w_down.shape[2]
    nt = plan["nt"]
    return pl.pallas_call(
        _moe_down_kernel,
        out_shape=jax.ShapeDtypeStruct((r, d), F32),
        grid_spec=pltpu.PrefetchScalarGridSpec(
            num_scalar_prefetch=6,
            grid=(d // MOE_TN, nt),
            in_specs=[pl.BlockSpec((MOE_TM, f), lambda j, i, *_: (i, 0)),
                      pl.BlockSpec(memory_space=pl.ANY)],
            out_specs=pl.BlockSpec((MOE_TM, MOE_TN), lambda j, i, *_: (i, j)),
            scratch_shapes=[pltpu.VMEM((1, 2, f, MOE_TN), F32), pltpu.VMEM((f, MOE_TN), BF16),
                            pltpu.SemaphoreType.DMA((1, 2))]),
        compiler_params=_params("arbitrary", "arbitrary"),
        name="moe_down",
    )(*plan["scalars"], t, w_down)


def _combine_kernel(cur_ref, nxt_ref, y_hbm, x_ref, route_ref, g_ref, o_ref, buf, sem, *, final):
    tm = x_ref.shape[0]
    slot = _pipelined_gather(cur_ref, nxt_ref, y_hbm, buf, sem, 2 * tm)
    route = route_ref[...]
    g1 = route[:, ROUTE_G1:ROUTE_G1 + 1]
    g2 = route[:, ROUTE_G2:ROUTE_G2 + 1]
    x = x_ref[...] + g1 * buf[slot, :tm, :] + g2 * buf[slot, tm:, :]
    if final:
        ms = jnp.mean(x * x, axis=-1, keepdims=True)
        x = x * lax.rsqrt(ms + RMS_EPS) * g_ref[...]
    o_ref[...] = x


def moe_combine(x, y, dest, route, final_gain):
    L, d = x.shape
    tm = COMBINE_TM
    nt = L // tm
    idx = jnp.concatenate([dest[:L].reshape(nt, 1, tm), dest[L:].reshape(nt, 1, tm)], axis=2)
    smem = lambda f: pl.BlockSpec((1, 1, 2 * tm), f, memory_space=pltpu.SMEM)
    final = final_gain is not None
    gain = (final_gain if final else jnp.ones((d,), F32)).reshape(1, d)
    kern = functools.partial(_combine_kernel, final=final)
    return pl.pallas_call(
        kern,
        out_shape=jax.ShapeDtypeStruct((L, d), F32),
        grid=(nt,),
        in_specs=[smem(lambda i: (i, 0, 0)),
                  smem(lambda i: (jnp.minimum(i + 1, nt - 1), 0, 0)),
                  pl.BlockSpec(memory_space=pl.ANY),
                  pl.BlockSpec((tm, d), lambda i: (i, 0)),
                  pl.BlockSpec((tm, LANES), lambda i: (i, 0)),
                  pl.BlockSpec((1, d), lambda i: (0, 0))],
        out_specs=pl.BlockSpec((tm, d), lambda i: (i, 0)),
        scratch_shapes=[pltpu.VMEM((2, 2 * tm, d), F32), pltpu.SemaphoreType.DMA((2,))],
        compiler_params=_params("arbitrary"),
        name="moe_combine",
    )(idx, idx, y, x, route, gain)


def moe_ffn(x, norm_g, router_w, w_gate, w_up, w_down, final_gain):
    L = x.shape[0]
    h, route, counts = rmsnorm_router(x, norm_g, router_w)
    plan = _moe_plan(route, counts, L)
    hs = moe_dispatch(h, plan["row_token"], plan["tile_chunks"], plan["nt"])
    t = moe_up(hs, w_gate, w_up, plan)
    y = moe_down(t, w_down, plan)
    return moe_combine(x, y, plan["dest"], route, final_gain)


def _layer_mixers(x, l, tables, norm_mix, w_in_t, conv_w, conv_b, filt, hyena_skip, w_branch_a,
                  gla_gate_w2, gla_gate_b, gla_norm, w_branch_b, w_out):
    hc = (HYENA_ORDER + 1) * HYENA_WIDTH
    o_lr = hc + 2 * GLA_DK_TOTAL + 2 * GLA_DV_TOTAL
    o_g = o_lr + 2 * GLA_LOWRANK
    h = rmsnorm(x, norm_mix[l], BF16)
    proj = matmul([h], [(w_in_t, l, 0)], o_lr, [(0, 0)], _ep_plain, [], F32,
                  tm=1024, tn=1024, name="in_proj", w_transposed=True)
    gates = gate_projection(h, w_in_t, l, o_g, 2 * D_MODEL)
    la = gla_gates(h, w_in_t[l, o_lr:o_g], gla_gate_w2[l], gla_gate_b[l])

    z_a = hyena_mixer(proj, conv_w[l], conv_b[l], filt, hyena_skip[l], tables)
    o_f, o_b = gla_scan(proj, hc, la)
    z_b = gla_post(o_f, o_b, proj, hc + 2 * GLA_DK_TOTAL + GLA_DV_TOTAL, gla_norm[l])

    mixed = matmul([z_a, z_b], [(w_branch_a, l, 0), (w_branch_b, l, 0)], D_MODEL,
                   [(0, 0), (1, 1)], _ep_merge, [(gates, 0), (gates, D_MODEL)], BF16,
                   tm=1024, tn=1024, name="branch_merge")
    return matmul([mixed], [(w_out, l, 0)], D_MODEL, [(0, 0)], _ep_residual, [(x, 0)], F32,
                  tm=1024, tn=512, name="out_proj")


def _swiglu_ffn(x, h, wg, wu, wd, i):
    f = wg.shape[2]
    t = matmul([h], [(wg, i, 0), (wu, i, 0)], f, [(0, 0), (0, 1)], _ep_swiglu, [], BF16,
               tm=1024, tn=512, name="ffn_up")
    return matmul([t], [(wd, i, 0)], wd.shape[2], [(0, 0)], _ep_residual, [(x, 0)], F32,
                  tm=512, tn=512, name="ffn_down")


def kernel(x, norm_mix, w_in, conv_w, conv_b, filt_w1, filt_b1, filt_freq1, filt_w2, filt_b2, filt_freq2, filt_w3, filt_b3, filt_freq3, filt_w4, hyena_skip, w_branch_a, gla_gate_w2, gla_gate_b, gla_norm, w_branch_b, w_out, norm_ffn, dense_w_gate, dense_w_up, dense_w_down, router_w, moe_w_gate, moe_w_up, moe_w_down, norm_final):
    b, L, d = x.shape
    assert b == 1
    x = x.reshape(L, d)
    tables = _fft_tables(L)
    w_in_t = jnp.swapaxes(w_in, 1, 2)
    for l in range(DEPTH):
        filt = (filt_w1[l], filt_b1[l], filt_freq1[l], filt_w2[l], filt_b2[l], filt_freq2[l],
                filt_w3[l], filt_b3[l], filt_freq3[l], filt_w4[l])
        x = _layer_mixers(x, l, tables, norm_mix, w_in_t, conv_w, conv_b, filt, hyena_skip,
                          w_branch_a, gla_gate_w2, gla_gate_b, gla_norm, w_branch_b, w_out)
        i = l // 2
        last = l == DEPTH - 1
        if l % 2 == 0:
            h = rmsnorm(x, norm_ffn[l], BF16)
            x = _swiglu_ffn(x, h, dense_w_gate, dense_w_up, dense_w_down, i)
            if last:
                x = rmsnorm(x, norm_final, F32)
        else:
            x = moe_ffn(x, norm_ffn[l], router_w[i], moe_w_gate[i], moe_w_up[i], moe_w_down[i],
                        norm_final if last else None)
    return x.reshape(b, L, d)
```
